```python
import math
import jax, jax.numpy as jnp
from jax import lax
import numpy as np

D_MODEL = 1024
BATCH = 4
SEQ = 4096
DEPTH = 2
DEC_BATCH = 16
DEC_SEQ = 32
PAST_LEN = 1024

CHUNK = 64
N_AB = (DEPTH + 1) // 2
N_ATT = DEPTH // 2
CONV_W = 4
D_A = D_MODEL
NB_A = 8
BS_A = D_A // NB_A
RG_C = 8.0
D_B = D_MODEL
NH_B = 4
DH_B = D_B // NH_B
N_HEADS = 16
HEAD_DIM = D_MODEL // N_HEADS
N_KV = 4
GROUP = N_HEADS // N_KV
WINDOW = 128
NUM_BUCKETS = 32
MAX_DISTANCE = 128
D_FF = 3 * D_MODEL
N_EXPERTS = 8
TOP_K = 2
D_FF_E = 7 * D_MODEL // 2
EPS = 1e-6

kernel_name = 'hybrid_rglru_mlstm_swa_stream_step'

F32 = jnp.float32


def rms_norm(x, g):
    xf = x.astype(F32)
    y = xf * lax.rsqrt(jnp.mean(xf * xf, axis=-1, keepdims=True) + EPS) * g.astype(F32)
    return y.astype(x.dtype)


def causal_conv(x, buf, w, b):
    T = x.shape[1]
    xp = jnp.concatenate([buf.astype(x.dtype), x], axis=1)
    y = b
    for j in range(CONV_W):
        y = y + xp[:, j:j + T] * w[j]
    return y, xp[:, xp.shape[1] - (CONV_W - 1):]


def swiglu(x, w1, w3, w2):
    return (jax.nn.silu(x @ w1) * (x @ w3)) @ w2


def _lin_combine(e1, e2):
    a1, b1 = e1
    a2, b2 = e2
    return a1 * a2, a2 * b1 + b2


def rg_lru(x, h0, w_a, b_a, w_x, b_x, lam):
    B, T, _ = x.shape
    xf = x.astype(F32)
    xb = xf.reshape(B, T, NB_A, BS_A)
    r = jax.nn.sigmoid(jnp.einsum('btnc,nce->btne', xb, w_a.astype(F32)).reshape(B, T, D_A) + b_a)
    i = jax.nn.sigmoid(jnp.einsum('btnc,nce->btne', xb, w_x.astype(F32)).reshape(B, T, D_A) + b_x)
    log_a = -RG_C * r * jax.nn.softplus(-lam.astype(F32))
    a = jnp.exp(log_a)
    u = xf * i * jnp.sqrt(-jnp.expm1(2.0 * log_a))
    u = u.at[:, 0].add(a[:, 0] * h0.astype(F32))
    _, h = lax.associative_scan(_lin_combine, (a, u), axis=1)
    return h, h[:, -1]


def mlstm_chunkwise(q, k, v, i_pre, log_f, C0, n0, m0, chunk):
    B, T, H, d = q.shape
    nck = T // chunk
    causal = jnp.tril(jnp.ones((chunk, chunk), bool))

    def to_chunks(a):
        return jnp.moveaxis(a.reshape((B, nck, chunk) + a.shape[2:]), 1, 0)

    def step(carry, inp):
        C, n, m = carry
        qc, kc, vc, ic, fc = inp
        bt = jnp.moveaxis(jnp.cumsum(fc, axis=1), 2, 1)
        it = jnp.moveaxis(ic, 2, 1)
        dmat = bt[:, :, :, None] - bt[:, :, None, :] + it[:, :, None, :]
        dmat = jnp.where(causal, dmat, -jnp.inf)
        m_inter = bt + m[:, :, None]
        m_t = jnp.maximum(m_inter, dmat.max(-1))
        p = jnp.einsum('bthd,bshd->bhts', qc, kc) * jnp.exp(dmat - m_t[..., None])
        sc = jnp.exp(m_inter - m_t)
        num = jnp.einsum('bhts,bshd->bhtd', p, vc) + sc[..., None] * jnp.einsum('bthd,bhde->bhte', qc, C)
        den = p.sum(-1) + sc * jnp.einsum('bthd,bhd->bht', qc, n)
        h = num / jnp.maximum(jnp.abs(den), jnp.exp(-m_t))[..., None]
        m_new = m_t[:, :, -1]
        decay = jnp.exp(bt[:, :, -1:] - bt + it - m_new[..., None])
        keep = jnp.exp(bt[:, :, -1] + m - m_new)
        C_new = keep[..., None, None] * C + jnp.einsum('bhs,bshd,bshe->bhde', decay, kc, vc)
        n_new = keep[..., None] * n + jnp.einsum('bhs,bshd->bhd', decay, kc)
        return (C_new, n_new, m_new), jnp.moveaxis(h, 1, 2)

    (C, n, m), hs = lax.scan(step, (C0, n0, m0), (to_chunks(q), to_chunks(k), to_chunks(v), to_chunks(i_pre), to_chunks(log_f)))
    return jnp.moveaxis(hs, 0, 1).reshape(B, T, H, d), C, n, m


def mlstm_mixer(xm, z, conv_buf, C0, n0, m0, w_conv, b_conv, w_q, w_k, w_v, w_if, b_if, g_h, skip):
    B, T, _ = xm.shape
    xc, conv_new = causal_conv(xm, conv_buf, w_conv, b_conv)
    xc = jax.nn.silu(xc)
    xch = xc.reshape(B, T, NH_B, DH_B)
    xmh = xm.reshape(B, T, NH_B, DH_B)
    q = jnp.einsum('bthd,hde->bthe', xch, w_q).astype(F32)
    k = jnp.einsum('bthd,hde->bthe', xch, w_k).astype(F32)
    v = jnp.einsum('bthd,hde->bthe', xmh, w_v).astype(F32)
    qkv = jnp.concatenate([q.reshape(B, T, D_B), k.reshape(B, T, D_B), v.reshape(B, T, D_B)], axis=-1)
    gates = qkv @ w_if.astype(F32) + b_if.astype(F32)
    i_pre, f_pre = gates[..., :NH_B], gates[..., NH_B:]
    h, C, n, m = mlstm_chunkwise(q, k * (DH_B ** -0.5), v, i_pre, jax.nn.log_sigmoid(f_pre),
                                 C0.astype(F32), n0.astype(F32), m0.astype(F32), min(T, CHUNK))
    mu = h.mean(-1, keepdims=True)
    var = jnp.mean(jnp.square(h - mu), -1, keepdims=True)
    hn = (h - mu) * lax.rsqrt(var + EPS) * g_h.astype(F32)
    out = (hn.reshape(B, T, D_B) + skip.astype(F32) * xc.astype(F32)) * jax.nn.silu(z.astype(F32))
    dt = xm.dtype
    return out.astype(dt), conv_new, C.astype(dt), n.astype(dt), m.astype(dt)


def ab_mixer(h, conv_a, h0, conv_b, C0, n0, m0, w_in, w_conv_a, b_conv_a, w_rg_a, b_rg_a, w_rg_x, b_rg_x,
             rg_lam, w_conv_b, b_conv_b, w_q, w_k, w_v, w_if, b_if, g_h, skip, w_out):
    xa, ga, xm, z = jnp.split(h @ w_in, [D_A, 2 * D_A, 2 * D_A + D_B], axis=-1)
    xa_c, conv_a_new = causal_conv(xa, conv_a, w_conv_a, b_conv_a)
    ha, h_last = rg_lru(xa_c, h0, w_rg_a, b_rg_a, w_rg_x, b_rg_x, rg_lam)
    ya = (ha * jax.nn.gelu(ga.astype(F32))).astype(h.dtype)
    yb, conv_b_new, C, n, m = mlstm_mixer(xm, z, conv_b, C0, n0, m0, w_conv_b, b_conv_b, w_q, w_k, w_v,
                                          w_if, b_if, g_h, skip)
    out = jnp.concatenate([ya, yb], axis=-1) @ w_out
    return out, conv_a_new, h_last.astype(h.dtype), conv_b_new, C, n, m


def t5_bucket(rel):
    n = -rel
    half = NUM_BUCKETS // 2
    ret = jnp.where(n < 0, half, 0)
    n = jnp.abs(n)
    max_exact = half // 2
    nf = jnp.maximum(n, 1).astype(F32)
    large = max_exact + (jnp.log(nf / max_exact) / math.log(MAX_DISTANCE / max_exact)
                         * (half - max_exact)).astype(jnp.int32)
    large = jnp.minimum(large, half - 1)
    return ret + jnp.where(n < max_exact, n, large)


def rel_bias_lookup(rel_bias, rel):
    return jnp.moveaxis(rel_bias[t5_bucket(rel)], -1, 0)


def attn_qkv(h, w_in, g_q, g_k):
    B, T, _ = h.shape
    q, k, v = jnp.split(h @ w_in, [N_HEADS * HEAD_DIM, N_HEADS * HEAD_DIM + N_KV * HEAD_DIM], axis=-1)
    q = rms_norm(q.reshape(B, T, N_HEADS, HEAD_DIM), g_q)
    k = rms_norm(k.reshape(B, T, N_KV, HEAD_DIM), g_k)
    return q, k, v.reshape(B, T, N_KV, HEAD_DIM)


def banded_attention(q, k, v, bias, valid, sinks):
    B, N, Q, _, _ = q.shape
    K = k.shape[2]
    qg = q.reshape(B, N, Q, N_KV, GROUP, HEAD_DIM)
    s = jnp.einsum('bnqkgd,bnskd->bnkgqs', qg, k).astype(F32) * (HEAD_DIM ** -0.5)
    s = s + bias.reshape(N_KV, GROUP, Q, K).astype(F32)
    s = jnp.where(valid[None, :, None, None, None, :], s, -jnp.inf)
    sink = jnp.broadcast_to(sinks.astype(F32).reshape(N_KV, GROUP, 1, 1), s.shape[:-1] + (1,))
    p = jax.nn.softmax(jnp.concatenate([s, sink], axis=-1), axis=-1)[..., :-1]
    o = jnp.einsum('bnkgqs,bnskd->bnqkgd', p.astype(v.dtype), v)
    return o.reshape(B, N, Q, N_HEADS * HEAD_DIM)


def swa_prompt(h, w_in, g_q, g_k, sinks, rel_bias, w_out):
    q, k, v = attn_qkv(h, w_in, g_q, g_k)
    B, T = h.shape[0], h.shape[1]
    NC = T // CHUNK
    NWC = WINDOW // CHUNK
    Kn = (NWC + 1) * CHUNK
    qb = q.reshape(B, NC, CHUNK, N_HEADS, HEAD_DIM)

    def band(a):
        ap = jnp.pad(a, ((0, 0), (WINDOW, 0), (0, 0), (0, 0))).reshape(B, NC + NWC, CHUNK, N_KV, HEAD_DIM)
        return jnp.concatenate([ap[:, j:j + NC] for j in range(NWC + 1)], axis=2)

    kpos = jnp.arange(Kn) - WINDOW
    valid = (jnp.arange(NC)[:, None] * CHUNK + kpos[None, :]) >= 0
    bias = rel_bias_lookup(rel_bias, kpos[None, :] - jnp.arange(CHUNK)[:, None])
    o = banded_attention(qb, band(k), band(v), bias, valid, sinks).reshape(B, T, N_HEADS * HEAD_DIM)
    return o @ w_out, k[:, T - WINDOW:], v[:, T - WINDOW:]


def swa_sample(h, cache_k, cache_v, w_in, g_q, g_k, sinks, rel_bias, w_out):
    q, k, v = attn_qkv(h, w_in, g_q, g_k)
    B, T = h.shape[0], h.shape[1]
    W = cache_k.shape[1]
    kc = jnp.concatenate([cache_k.astype(k.dtype), k], axis=1)
    vc = jnp.concatenate([cache_v.astype(v.dtype), v], axis=1)
    kpos = jnp.arange(W + T) - W
    valid = jnp.ones((1, W + T), bool)
    bias = rel_bias_lookup(rel_bias, kpos[None, :] - jnp.arange(T)[:, None])
    o = banded_attention(q[:, None], kc[:, None], vc[:, None], bias, valid, sinks).reshape(B, T, N_HEADS * HEAD_DIM)
    return o @ w_out, kc[:, T:], vc[:, T:]


def moe_ffn(x, w_router, w1, w3, w2):
    logits = (x @ w_router).astype(F32)
    top_v, top_i = lax.top_k(logits, TOP_K)
    gate = jax.nn.softmax(top_v, axis=-1)
    comb = jnp.sum(jax.nn.one_hot(top_i, N_EXPERTS, dtype=F32) * gate[..., None], axis=-2)
    out = jnp.zeros_like(x)
    for e in range(N_EXPERTS):
        out = out + comb[..., e:e + 1].astype(x.dtype) * swiglu(x, w1[e], w3[e], w2[e])
    return out


def setup_inputs(seed: int = 0) -> dict:
    key = jax.random.key(seed)
    ks = iter(jax.random.split(key, 64))

    def nrm(shape, scale):
        return scale * jax.random.normal(next(ks), shape, F32)

    def gain(shape):
        return 1.0 + nrm(shape, 0.02)

    cache_win = min(WINDOW, PAST_LEN)
    u = jax.random.uniform(next(ks), (N_AB, D_A), F32, 0.9, 0.999)
    a_root = u ** (1.0 / RG_C)
    rg_lambda = jnp.log(a_root) - jnp.log1p(-a_root)
    f_bias = jnp.broadcast_to(jnp.linspace(3.0, 6.0, NH_B, dtype=F32), (N_AB, NH_B)) + nrm((N_AB, NH_B), 0.1)
    b_if_b = jnp.concatenate([nrm((N_AB, NH_B), 0.1), f_bias], axis=-1)
    d_in_ab = 2 * D_A + 2 * D_B
    d_in_att = N_HEADS * HEAD_DIM + 2 * N_KV * HEAD_DIM
    return {
        'x_prompt': nrm((BATCH, SEQ, D_MODEL), 1.0),
        'x_sample': nrm((DEC_BATCH, DEC_SEQ, D_MODEL), 1.0),
        'state_conv_a': nrm((N_AB, DEC_BATCH, CONV_W - 1, D_A), 1.0),
        'state_rglru_h': nrm((N_AB, DEC_BATCH, D_A), 0.5),
        'state_conv_b': nrm((N_AB, DEC_BATCH, CONV_W - 1, D_B), 1.0),
        'state_mlstm_c': nrm((N_AB, DEC_BATCH, NH_B, DH_B, DH_B), 0.1),
        'state_mlstm_n': nrm((N_AB, DEC_BATCH, NH_B, DH_B), 0.1),
        'state_mlstm_m': nrm((N_AB, DEC_BATCH, NH_B), 0.5),
        'cache_swa_k': nrm((N_ATT, DEC_BATCH, cache_win, N_KV, HEAD_DIM), 1.0),
        'cache_swa_v': nrm((N_ATT, DEC_BATCH, cache_win, N_KV, HEAD_DIM), 1.0),
        'norm_mix': gain((DEPTH, D_MODEL)),
        'norm_ffn': gain((DEPTH, D_MODEL)),
        'w_in_ab': nrm((N_AB, D_MODEL, d_in_ab), D_MODEL ** -0.5),
        'w_conv_a': nrm((N_AB, CONV_W, D_A), CONV_W ** -0.5),
        'b_conv_a': nrm((N_AB, D_A), 0.02),
        'w_rg_a': nrm((N_AB, NB_A, BS_A, BS_A), BS_A ** -0.5),
        'b_rg_a': nrm((N_AB, D_A), 0.1),
        'w_rg_x': nrm((N_AB, NB_A, BS_A, BS_A), BS_A ** -0.5),
        'b_rg_x': nrm((N_AB, D_A), 0.1),
        'rg_lambda': rg_lambda,
        'w_conv_b': nrm((N_AB, CONV_W, D_B), CONV_W ** -0.5),
        'b_conv_b': nrm((N_AB, D_B), 0.02),
        'w_q_b': nrm((N_AB, NH_B, DH_B, DH_B), DH_B ** -0.5),
        'w_k_b': nrm((N_AB, NH_B, DH_B, DH_B), DH_B ** -0.5),
        'w_v_b': nrm((N_AB, NH_B, DH_B, DH_B), DH_B ** -0.5),
        'w_if_b': nrm((N_AB, 3 * D_B, 2 * NH_B), (3 * D_B) ** -0.5),
        'b_if_b': b_if_b,
        'g_hnorm_b': gain((N_AB, NH_B, DH_B)),
        'skip_b': gain((N_AB, D_B)),
        'w_out_ab': nrm((N_AB, D_A + D_B, D_MODEL), (D_A + D_B) ** -0.5),
        'w1_dense': nrm((N_AB, D_MODEL, D_FF), D_MODEL ** -0.5),
        'w3_dense': nrm((N_AB, D_MODEL, D_FF), D_MODEL ** -0.5),
        'w2_dense': nrm((N_AB, D_FF, D_MODEL), D_FF ** -0.5),
        'w_in_att': nrm((N_ATT, D_MODEL, d_in_att), D_MODEL ** -0.5),
        'g_qnorm': gain((N_ATT, HEAD_DIM)),
        'g_knorm': gain((N_ATT, HEAD_DIM)),
        'sinks': nrm((N_ATT, N_HEADS), 0.5),
        'w_out_att': nrm((N_ATT, N_HEADS * HEAD_DIM, D_MODEL), (N_HEADS * HEAD_DIM) ** -0.5),
        'rel_bias': nrm((NUM_BUCKETS, N_HEADS), 0.5),
        'w_router': nrm((N_ATT, D_MODEL, N_EXPERTS), D_MODEL ** -0.5),
        'w1_moe': nrm((N_ATT, N_EXPERTS, D_MODEL, D_FF_E), D_MODEL ** -0.5),
        'w3_moe': nrm((N_ATT, N_EXPERTS, D_MODEL, D_FF_E), D_MODEL ** -0.5),
        'w2_moe': nrm((N_ATT, N_EXPERTS, D_FF_E, D_MODEL), D_FF_E ** -0.5),
    }


def reference(x_prompt, x_sample, state_conv_a, state_rglru_h, state_conv_b, state_mlstm_c, state_mlstm_n,
              state_mlstm_m, cache_swa_k, cache_swa_v, norm_mix, norm_ffn, w_in_ab, w_conv_a, b_conv_a,
              w_rg_a, b_rg_a, w_rg_x, b_rg_x, rg_lambda, w_conv_b, b_conv_b, w_q_b, w_k_b, w_v_b, w_if_b,
              b_if_b, g_hnorm_b, skip_b, w_out_ab, w1_dense, w3_dense, w2_dense, w_in_att, g_qnorm, g_knorm,
              sinks, w_out_att, rel_bias, w_router, w1_moe, w3_moe, w2_moe):
    yp, ys = x_prompt, x_sample
    bp = x_prompt.shape[0]
    dt = x_prompt.dtype
    ab_p, ab_s, att_p, att_s = [], [], [], []
    for layer in range(DEPTH):
        e = layer // 2
        if layer % 2 == 0:
            wts = (w_in_ab[e], w_conv_a[e], b_conv_a[e], w_rg_a[e], b_rg_a[e], w_rg_x[e], b_rg_x[e],
                   rg_lambda[e], w_conv_b[e], b_conv_b[e], w_q_b[e], w_k_b[e], w_v_b[e], w_if_b[e],
                   b_if_b[e], g_hnorm_b[e], skip_b[e], w_out_ab[e])
            mp, *stp = ab_mixer(rms_norm(yp, norm_mix[layer]),
                                jnp.zeros((bp, CONV_W - 1, D_A), dt), jnp.zeros((bp, D_A), dt),
                                jnp.zeros((bp, CONV_W - 1, D_B), dt), jnp.zeros((bp, NH_B, DH_B, DH_B), dt),
                                jnp.zeros((bp, NH_B, DH_B), dt), jnp.zeros((bp, NH_B), dt), *wts)
            ms, *sts = ab_mixer(rms_norm(ys, norm_mix[layer]), state_conv_a[e], state_rglru_h[e],
                                state_conv_b[e], state_mlstm_c[e], state_mlstm_n[e], state_mlstm_m[e], *wts)
            yp = yp + mp
            ys = ys + ms
            ab_p.append(stp)
            ab_s.append(sts)
            yp = yp + swiglu(rms_norm(yp, norm_ffn[layer]), w1_dense[e], w3_dense[e], w2_dense[e])
            ys = ys + swiglu(rms_norm(ys, norm_ffn[layer]), w1_dense[e], w3_dense[e], w2_dense[e])
        else:
            mp, kp, vp = swa_prompt(rms_norm(yp, norm_mix[layer]), w_in_att[e], g_qnorm[e], g_knorm[e],
                                    sinks[e], rel_bias, w_out_att[e])
            ms, ks_, vs_ = swa_sample(rms_norm(ys, norm_mix[layer]), cache_swa_k[e], cache_swa_v[e],
                                      w_in_att[e], g_qnorm[e], g_knorm[e], sinks[e], rel_bias, w_out_att[e])
            yp = yp + mp
            ys = ys + ms
            att_p.append((kp, vp))
            att_s.append((ks_, vs_))
            yp = yp + moe_ffn(rms_norm(yp, norm_ffn[layer]), w_router[e], w1_moe[e], w3_moe[e], w2_moe[e])
            ys = ys + moe_ffn(rms_norm(ys, norm_ffn[layer]), w_router[e], w1_moe[e], w3_moe[e], w2_moe[e])

    def stk(lst, i):
        return jnp.stack([s[i] for s in lst])

    return (yp, ys,
            stk(ab_p, 0), stk(ab_p, 1), stk(ab_p, 2), stk(ab_p, 3), stk(ab_p, 4), stk(ab_p, 5),
            stk(att_p, 0), stk(att_p, 1),
            stk(ab_s, 0), stk(ab_s, 1), stk(ab_s, 2), stk(ab_s, 3), stk(ab_s, 4), stk(ab_s, 5),
            stk(att_s, 0), stk(att_s, 1))
```

```python
import functools
import math

import jax
import jax.numpy as jnp
from jax import lax
from jax.experimental import pallas as pl
from jax.experimental.pallas import tpu as pltpu

F32 = jnp.float32
BF16 = jnp.bfloat16

D = 1024
CHUNK = 64
CONV_W = 4
NB_A = 8
BS_A = D // NB_A
RG_C = 8.0
NH_B = 4
DH_B = D // NH_B
N_HEADS = 16
HEAD_DIM = D // N_HEADS
N_KV = 4
GROUP = N_HEADS // N_KV
D_KV = N_KV * HEAD_DIM
WINDOW = 128
NUM_BUCKETS = 32
MAX_DISTANCE = 128
N_EXPERTS = 8
EPS = 1e-6
LANES = 128
NEG_INF = float("-inf")

TOKEN_TILE = 512
EXPERT_TILE = 512
FF_SPLIT_DENSE = 3
FF_SPLIT_MOE = 2
SEQ_BLOCK = 256
VMEM_LIMIT = 56 * 1024 * 1024


def _params(sem):
    return pltpu.CompilerParams(dimension_semantics=sem, vmem_limit_bytes=VMEM_LIMIT)


def _full(shape):
    return pl.BlockSpec(shape, lambda *_: (0,) * len(shape))


def _resident(shape):
    return pl.BlockSpec(shape, lambda *_: (0,) * len(shape), pipeline_mode=pl.Buffered(1))


def _rows(tm, c):
    return pl.BlockSpec((tm, c), lambda i: (i, 0))


def _pair(tm, c, n_p):
    return [pl.BlockSpec((tm, c), lambda i: (jnp.minimum(i, n_p - 1), 0)),
            pl.BlockSpec((tm, c), lambda i: (jnp.maximum(i - n_p, 0), 0))]


def _rms(x, g):
    ms = jnp.mean(x * x, axis=-1, keepdims=True)
    return x * lax.rsqrt(ms + EPS) * g


def _silu(x):
    return x * jax.nn.sigmoid(x)


def _softplus(x):
    return jnp.maximum(x, 0.0) + jnp.log1p(jnp.exp(-jnp.abs(x)))


def _gelu_tanh(x):
    c = math.sqrt(2.0 / math.pi)
    return x * (0.5 * (1.0 + jnp.tanh(c * (x + 0.044715 * (x * x * x)))))


def _dot(a, b):
    return jnp.dot(a, b, preferred_element_type=F32)


def _dot_nt(a, b):
    return lax.dot_general(a, b, (((1,), (1,)), ((), ())), preferred_element_type=F32)


def _dot_tn(a, b):
    return lax.dot_general(a, b, (((0,), (0,)), ((), ())), preferred_element_type=F32)


def _in_proj_ab_kernel(n_p, xp_ref, xs_ref, g_ref, w_ref, *o_refs):
    i = pl.program_id(0)
    x = jnp.where(i < n_p, xp_ref[...], xs_ref[...])
    xn = _rms(x, g_ref[...]).astype(BF16)
    for c, o_ref in enumerate(o_refs):
        o_ref[...] = _dot(xn, w_ref[:, c * D:(c + 1) * D])


def _in_proj_ab(xp, xs, g, w, tm):
    n_p, n_s = xp.shape[0] // tm, xs.shape[0] // tm
    n = xp.shape[0] + xs.shape[0]
    n_out = w.shape[1] // D
    return pl.pallas_call(
        functools.partial(_in_proj_ab_kernel, n_p),
        grid=(n_p + n_s,),
        in_specs=_pair(tm, D, n_p) + [_full((1, D)), _resident(w.shape)],
        out_specs=[_rows(tm, D)] * n_out,
        out_shape=[jax.ShapeDtypeStruct((n, D), F32)] * n_out,
        compiler_params=_params(("parallel",)),
        name="in_proj_ab",
    )(xp, xs, g, w)


def _out_proj_kernel(n_p, n_res, n_a, *refs):
    i = pl.program_id(0)
    res_refs = refs[:n_res]
    a_refs = refs[n_res:n_res + 2 * n_a]
    w_refs = refs[n_res + 2 * n_a:n_res + 3 * n_a]
    o_ref = refs[-1]
    if n_res == 2:
        acc = jnp.where(i < n_p, res_refs[0][...], res_refs[1][...])
    else:
        acc = res_refs[0][...]
    for k in range(n_a):
        a = jnp.where(i < n_p, a_refs[2 * k][...], a_refs[2 * k + 1][...])
        acc = acc + _dot(a, w_refs[k][...])
    o_ref[...] = acc


def _out_proj(res, a_pairs, ws, tm, n_p):
    res = res if isinstance(res, tuple) else (res,)
    n = sum(a.shape[0] for a in a_pairs[0])
    res_specs = _pair(tm, D, n_p) if len(res) == 2 else [_rows(tm, D)]
    a_specs = []
    for a in a_pairs:
        a_specs += _pair(tm, a[0].shape[1], n_p)
    return pl.pallas_call(
        functools.partial(_out_proj_kernel, n_p, len(res), len(a_pairs)),
        grid=(n // tm,),
        in_specs=res_specs + a_specs + [_resident(w.shape) for w in ws],
        out_specs=_rows(tm, D),
        out_shape=jax.ShapeDtypeStruct((n, D), F32),
        compiler_params=_params(("parallel",)),
        name="out_proj",
    )(*res, *[x for a in a_pairs for x in a], *ws)


def _ffn_dense_kernel(x_ref, g_ref, w1_ref, w3_ref, w2_ref, o_ref):
    x = x_ref[...]
    xn = _rms(x, g_ref[...]).astype(BF16)
    step = w1_ref.shape[1] // FF_SPLIT_DENSE
    acc = x
    for c in range(FF_SPLIT_DENSE):
        sl = slice(c * step, (c + 1) * step)
        a = (_silu(_dot(xn, w1_ref[:, sl])) * _dot(xn, w3_ref[:, sl])).astype(BF16)
        acc = acc + _dot(a, w2_ref[sl, :])
    o_ref[...] = acc


def _ffn_dense(x, g, w1, w3, w2, tm):
    n = x.shape[0]
    return pl.pallas_call(
        _ffn_dense_kernel,
        grid=(n // tm,),
        in_specs=[_rows(tm, D), _full((1, D)), _resident(w1.shape), _resident(w3.shape), _resident(w2.shape)],
        out_specs=_rows(tm, D),
        out_shape=jax.ShapeDtypeStruct((n, D), F32),
        compiler_params=_params(("parallel",)),
        name="ffn_dense",
    )(x, g, w1, w3, w2)


def _in_proj_att_kernel(x_ref, g_ref, w_ref, gq_ref, gk_ref, q_ref, k_ref, v_ref):
    xn = _rms(x_ref[...], g_ref[...]).astype(BF16)
    cw = D_KV
    r = lax.broadcasted_iota(jnp.int32, (cw, cw), 0) // HEAD_DIM
    c = lax.broadcasted_iota(jnp.int32, (cw, cw), 1) // HEAD_DIM
    group_ones = (r == c).astype(BF16)

    def head_norm(y, gain):
        y2 = y * y
        hi = y2.astype(BF16)
        lo = (y2 - hi.astype(F32)).astype(BF16)
        ms = (_dot(hi, group_ones) + _dot(lo, group_ones)) * (1.0 / HEAD_DIM)
        return y * lax.rsqrt(ms + EPS) * gain

    for b in range(D // cw):
        y = _dot(xn, w_ref[:, b * cw:(b + 1) * cw])
        q_ref[:, b * cw:(b + 1) * cw] = (head_norm(y, gq_ref[...]) * (HEAD_DIM ** -0.5)).astype(BF16)
    k_ref[...] = head_norm(_dot(xn, w_ref[:, D:D + D_KV]), gk_ref[...])
    v_ref[...] = _dot(xn, w_ref[:, D + D_KV:D + 2 * D_KV])


def _in_proj_att(x, g, w, gq, gk, tm):
    n = x.shape[0]
    return pl.pallas_call(
        _in_proj_att_kernel,
        grid=(n // tm,),
        in_specs=[_rows(tm, D), _full((1, D)), _resident(w.shape), _full((1, D_KV)), _full((1, D_KV))],
        out_specs=[_rows(tm, D), _rows(tm, D_KV), _rows(tm, D_KV)],
        out_shape=[jax.ShapeDtypeStruct((n, D), BF16), jax.ShapeDtypeStruct((n, D_KV), F32),
                   jax.ShapeDtypeStruct((n, D_KV), F32)],
        compiler_params=_params(("parallel",)),
        name="in_proj_att",
    )(x, g, w, gq, gk)


def _causal_conv(x, xbuf, conv0_ref, wc_ref, bc_ref, first):
    tb = x.shape[0]

    @pl.when(first)
    def _():
        xbuf[5:8, :] = conv0_ref[0]

    xbuf[8:8 + tb, :] = x
    y = bc_ref[...] + xbuf[5:5 + tb, :] * wc_ref[0:1, :]
    for j in range(1, CONV_W):
        y = y + xbuf[5 + j:5 + j + tb, :] * wc_ref[j:j + 1, :]
    xbuf[5:8, :] = xbuf[5 + tb:8 + tb, :]
    return y


def _seq_specs(bsz, seq, tb, row0):
    nc = seq // tb
    off = row0 // tb
    return nc, (lambda bi, ci: (off + bi * nc + ci, 0)), (lambda bi, ci: (bi * nc + ci, 0))


def _rglru_kernel(xa_ref, ga_ref, conv0_ref, h0_ref, wc_ref, bc_ref, wa_ref, ba_ref, wx_ref, bx_ref, lam_ref,
                  ya_ref, hl_ref, xbuf):
    first = pl.program_id(1) == 0
    tb = xa_ref.shape[0]

    @pl.when(first)
    def _():
        hl_ref[0] = h0_ref[0]

    y = _causal_conv(xa_ref[...], xbuf, conv0_ref, wc_ref, bc_ref, first)
    row = lax.broadcasted_iota(jnp.int32, (tb, BS_A), 0)
    for n in range(NB_A):
        sl = slice(n * BS_A, (n + 1) * BS_A)
        yn = y[:, sl]
        yb = yn.astype(BF16)
        r = jax.nn.sigmoid(_dot(yb, wa_ref[n]) + ba_ref[:, sl])
        ig = jax.nn.sigmoid(_dot(yb, wx_ref[n]) + bx_ref[:, sl])
        log_a = (-RG_C) * r * _softplus(-lam_ref[:, sl])
        a = jnp.exp(log_a)
        th = jnp.tanh(log_a)
        u = yn * ig * jnp.sqrt(-2.0 * th / (1.0 - th))
        d = 1
        while d < tb:
            keep = row >= d
            u = jnp.where(keep, a * pltpu.roll(u, d, 0) + u, u)
            a = jnp.where(keep, a * pltpu.roll(a, d, 0), a)
            d *= 2
        h = a * hl_ref[0, :, sl] + u
        hl_ref[0, :, sl] = h[tb - 1:tb, :]
        ya_ref[:, sl] = (h * _gelu_tanh(ga_ref[:, sl])).astype(BF16)


def _rglru(xa, ga, row0, seq, conv0, h0, wc, bc, wa, ba, wx, bx, lam):
    bsz = conv0.shape[0]
    tb = min(seq, SEQ_BLOCK)
    nc, in_map, out_map = _seq_specs(bsz, seq, tb, row0)
    state3 = lambda bi, ci: (bi, 0, 0)
    return pl.pallas_call(
        _rglru_kernel,
        grid=(bsz, nc),
        in_specs=[pl.BlockSpec((tb, D), in_map), pl.BlockSpec((tb, D), in_map),
                  pl.BlockSpec((1, CONV_W - 1, D), state3), pl.BlockSpec((1, 1, D), state3),
                  _full(wc.shape), _full(bc.shape), _full(wa.shape), _full(ba.shape), _full(wx.shape),
                  _full(bx.shape), _full(lam.shape)],
        out_specs=[pl.BlockSpec((tb, D), out_map), pl.BlockSpec((1, 1, D), state3)],
        out_shape=[jax.ShapeDtypeStruct((bsz * seq, D), BF16), jax.ShapeDtypeStruct((bsz, 1, D), F32)],
        scratch_shapes=[pltpu.VMEM((tb + 8, D), F32)],
        compiler_params=_params(("parallel", "arbitrary")),
        name="rglru",
    )(xa, ga, conv0, h0, wc, bc, wa, ba, wx, bx, lam)


def _mlstm_pre_kernel(chunk, xm_ref, conv0_ref, wc_ref, bc_ref, wq_ref, wk_ref, wv_ref, wif_ref, bif_ref,
                      xc_ref, q_ref, k_ref, v_ref, g_ref, xbuf):
    first = pl.program_id(1) == 0
    tb = xm_ref.shape[0]
    x = xm_ref[...]
    xc = _silu(_causal_conv(x, xbuf, conv0_ref, wc_ref, bc_ref, first))
    xc_ref[...] = xc
    gates = bif_ref[...]
    for h in range(NH_B):
        sl = slice(h * DH_B, (h + 1) * DH_B)
        xch = xc[:, sl].astype(BF16)
        q = _dot(xch, wq_ref[h]).astype(BF16)
        k = _dot(xch, wk_ref[h]).astype(BF16)
        v = _dot(x[:, sl].astype(BF16), wv_ref[h]).astype(BF16)
        q_ref[:, sl] = q
        k_ref[:, sl] = k
        v_ref[:, sl] = v
        gates = (gates + _dot_nt(wif_ref[:, sl], q) + _dot_nt(wif_ref[:, D + h * DH_B:D + (h + 1) * DH_B], k)
                 + _dot_nt(wif_ref[:, 2 * D + h * DH_B:2 * D + (h + 1) * DH_B], v))
    row = lax.broadcasted_iota(jnp.int32, gates.shape, 0)
    gates = jnp.where(row < NH_B, gates, -_softplus(-gates))
    for j in range(tb // chunk):
        g_ref[j] = gates[:, j * chunk:(j + 1) * chunk]


def _mlstm_pre(xm, row0, seq, conv0, wc, bc, wq, wk, wv, wif_t, bif):
    bsz = conv0.shape[0]
    tb = min(seq, SEQ_BLOCK)
    chunk = min(seq, CHUNK)
    nc, in_map, out_map = _seq_specs(bsz, seq, tb, row0)
    n = bsz * seq
    return pl.pallas_call(
        functools.partial(_mlstm_pre_kernel, chunk),
        grid=(bsz, nc),
        in_specs=[pl.BlockSpec((tb, D), in_map), pl.BlockSpec((1, CONV_W - 1, D), lambda bi, ci: (bi, 0, 0)),
                  _full(wc.shape), _full(bc.shape), _full(wq.shape), _full(wk.shape), _full(wv.shape),
                  _full(wif_t.shape), _full(bif.shape)],
        out_specs=[pl.BlockSpec((tb, D), out_map)] * 4
                  + [pl.BlockSpec((tb // chunk, 2 * NH_B, chunk), lambda bi, ci: (bi * nc + ci, 0, 0))],
        out_shape=[jax.ShapeDtypeStruct((n, D), F32)] + [jax.ShapeDtypeStruct((n, D), BF16)] * 3
                  + [jax.ShapeDtypeStruct((n // chunk, 2 * NH_B, chunk), F32)],
        scratch_shapes=[pltpu.VMEM((tb + 8, D), F32)],
        compiler_params=_params(("parallel", "arbitrary")),
        name="mlstm_pre",
    )(xm, conv0, wc, bc, wq, wk, wv, wif_t, bif)


def _mlstm_kernel(chunk, q_ref, k_ref, v_ref, g_ref, xc_ref, z_ref, c0_ref, n0_ref, m0_ref, gh_ref, skip_ref,
                  yb_ref, c_ref, n_ref, m_ref):
    tb = q_ref.shape[0]
    L = chunk

    @pl.when(pl.program_id(1) == 0)
    def _():
        c_ref[...] = c0_ref[...]
        n_ref[...] = n0_ref[...]
        m_ref[...] = m0_ref[...]

    tt = lax.broadcasted_iota(jnp.int32, (L, L), 0)
    ss = lax.broadcasted_iota(jnp.int32, (L, L), 1)
    eye = tt == ss
    causal = ss <= tt

    def to_col(row_vec):
        return jnp.sum(jnp.where(eye, row_vec, 0.0), axis=1, keepdims=True)

    def chunk_step(j):
        r0 = pl.multiple_of(j * L, L)
        rows = pl.ds(r0, L)
        g = g_ref[j]
        for h in range(NH_B):
            sl = slice(h * DH_B, (h + 1) * DH_B)
            q = q_ref[rows, sl]
            k = k_ref[rows, sl]
            v = v_ref[rows, sl]
            i_row = g[h:h + 1, :]
            f_row = g[NH_B + h:NH_B + h + 1, :]
            f_col = to_col(f_row)
            i_col = to_col(i_row)
            bt_col = jnp.sum(jnp.where(causal, f_row, 0.0), axis=1, keepdims=True)
            bt_row = jnp.sum(jnp.where(tt <= ss, f_col, 0.0), axis=0, keepdims=True)
            m_prev = m_ref[0, h:h + 1, 0:1]
            dmat = jnp.where(causal, bt_col - bt_row + i_row, NEG_INF)
            m_inter = bt_col + m_prev
            m_t = jnp.maximum(m_inter, jnp.max(dmat, axis=1, keepdims=True))
            p = _dot_nt(q, k) * (DH_B ** -0.5) * jnp.exp(dmat - m_t)
            sc = jnp.exp(m_inter - m_t)
            c_old = c_ref[0, h]
            n_old = n_ref[0, h:h + 1, :]
            num = _dot(p.astype(BF16), v) + sc * _dot(q, c_old.astype(BF16))
            den = jnp.sum(p, axis=1, keepdims=True) + sc * jnp.sum(q.astype(F32) * n_old, axis=1, keepdims=True)
            hs = num / jnp.maximum(jnp.abs(den), jnp.exp(-m_t))
            m_new = m_t[L - 1:L, :]
            bt_last = bt_col[L - 1:L, :]
            decay = jnp.exp(bt_last - bt_col + i_col - m_new)
            keep = jnp.exp(bt_last + m_prev - m_new)
            kd = k.astype(F32) * (decay * (DH_B ** -0.5))
            c_ref[0, h] = keep * c_old + _dot_tn(kd.astype(BF16), v)
            n_ref[0, h:h + 1, :] = keep * n_old + jnp.sum(kd, axis=0, keepdims=True)
            m_ref[0, h:h + 1, :] = jnp.broadcast_to(m_new, (1, LANES))
            mu = jnp.mean(hs, axis=1, keepdims=True)
            dev = hs - mu
            var = jnp.mean(dev * dev, axis=1, keepdims=True)
            hn = dev * lax.rsqrt(var + EPS) * gh_ref[:, sl]
            out = (hn + skip_ref[:, sl] * xc_ref[rows, sl]) * _silu(z_ref[rows, sl])
            yb_ref[rows, sl] = out.astype(BF16)

    if tb == L:
        chunk_step(0)
    else:
        def body(j, carry):
            chunk_step(j)
            return carry
        lax.fori_loop(0, tb // L, body, 0)


def _mlstm(q, k, v, g, xc, z, row0, seq, c0, n0, m0, gh, skip):
    bsz = c0.shape[0]
    tb = min(seq, SEQ_BLOCK)
    chunk = min(seq, CHUNK)
    nc, z_map, own_map = _seq_specs(bsz, seq, tb, row0)
    st4 = lambda bi, ci: (bi, 0, 0, 0)
    st3 = lambda bi, ci: (bi, 0, 0)
    return pl.pallas_call(
        functools.partial(_mlstm_kernel, chunk),
        grid=(bsz, nc),
        in_specs=[pl.BlockSpec((tb, D), own_map)] * 3
                 + [pl.BlockSpec((tb // chunk, 2 * NH_B, chunk), lambda bi, ci: (bi * nc + ci, 0, 0)),
                    pl.BlockSpec((tb, D), own_map), pl.BlockSpec((tb, D), z_map),
                    pl.BlockSpec((1, NH_B, DH_B, DH_B), st4), pl.BlockSpec((1, NH_B, DH_B), st3),
                    pl.BlockSpec((1, NH_B, LANES), st3), _full(gh.shape), _full(skip.shape)],
        out_specs=[pl.BlockSpec((tb, D), own_map), pl.BlockSpec((1, NH_B, DH_B, DH_B), st4),
                   pl.BlockSpec((1, NH_B, DH_B), st3), pl.BlockSpec((1, NH_B, LANES), st3)],
        out_shape=[jax.ShapeDtypeStruct((bsz * seq, D), BF16), jax.ShapeDtypeStruct(c0.shape, F32),
                   jax.ShapeDtypeStruct(n0.shape, F32), jax.ShapeDtypeStruct(m0.shape, F32)],
        compiler_params=_params(("parallel", "arbitrary")),
        name="mlstm",
    )(q, k, v, g, xc, z, c0, n0, m0, gh, skip)


def _attn_kernel(masked, q_ref, kh0_ref, kh1_ref, ko_ref, vh0_ref, vh1_ref, vo_ref, bucket_ref, relb_ref, sink_ref,
                 o_ref, bias_s):
    L = q_ref.shape[0]
    nk = WINDOW + L
    ci = pl.program_id(1)

    @pl.when((pl.program_id(0) == 0) & (ci == 0))
    def _():
        bucket = bucket_ref[...]
        for h in range(N_HEADS):
            acc = jnp.zeros((L, nk), F32)
            for b in range(NUM_BUCKETS):
                acc = jnp.where(bucket == b, relb_ref[b, h], acc)
            bias_s[h * L:(h + 1) * L, :] = acc

    kcat = jnp.concatenate([kh0_ref[...], kh1_ref[...], ko_ref[...]], axis=0).astype(BF16)
    vcat = jnp.concatenate([vh0_ref[...], vh1_ref[...], vo_ref[...]], axis=0).astype(BF16)
    q = q_ref[...]
    if masked:
        kk = lax.broadcasted_iota(jnp.int32, (1, nk), 1)
        half = WINDOW // 2
        valid = (kk >= WINDOW) | ((kk >= half) & (ci >= 1)) | (ci >= 2)
    outs = []
    for g in range(N_KV):
        kg = kcat[:, g * HEAD_DIM:(g + 1) * HEAD_DIM]
        vg = vcat[:, g * HEAD_DIM:(g + 1) * HEAD_DIM]
        heads = range(g * GROUP, (g + 1) * GROUP)
        qg = jnp.concatenate([q[:, h * HEAD_DIM:(h + 1) * HEAD_DIM] for h in heads], axis=0)
        s = _dot_nt(qg, kg) + bias_s[g * GROUP * L:(g + 1) * GROUP * L, :]
        if masked:
            s = jnp.where(valid, s, NEG_INF)
        sink = jnp.concatenate([jnp.full((L, 1), sink_ref[0, h], F32) for h in heads], axis=0)
        mx = jnp.maximum(jnp.max(s, axis=1, keepdims=True), sink)
        e = jnp.exp(s - mx)
        den = jnp.sum(e, axis=1, keepdims=True) + jnp.exp(sink - mx)
        o = _dot((e / den).astype(BF16), vg)
        outs += [o[j * L:(j + 1) * L, :] for j in range(GROUP)]
    o_ref[...] = jnp.concatenate(outs, axis=1).astype(BF16)


def _attention(q, k_own, v_own, k_hist, v_hist, hist_maps, row0, seq, bsz, masked, bucket, rel_bias, sinks):
    L = min(seq, CHUNK)
    nc, in_map, out_map = _seq_specs(bsz, seq, L, row0)
    half = WINDOW // 2
    h0_map, h1_map = hist_maps
    smem = functools.partial(pl.BlockSpec, memory_space=pltpu.SMEM)
    return pl.pallas_call(
        functools.partial(_attn_kernel, masked),
        grid=(bsz, nc),
        in_specs=[pl.BlockSpec((L, D), in_map),
                  pl.BlockSpec((half, D_KV), h0_map), pl.BlockSpec((half, D_KV), h1_map),
                  pl.BlockSpec((L, D_KV), in_map),
                  pl.BlockSpec((half, D_KV), h0_map), pl.BlockSpec((half, D_KV), h1_map),
                  pl.BlockSpec((L, D_KV), in_map),
                  _full(bucket.shape), smem(), smem()],
        out_specs=pl.BlockSpec((L, D), out_map),
        out_shape=jax.ShapeDtypeStruct((bsz * seq, D), BF16),
        scratch_shapes=[pltpu.VMEM((N_HEADS * L, WINDOW + L), F32)],
        compiler_params=_params(("arbitrary", "arbitrary")),
        name="swa",
    )(q, k_hist, k_hist, k_own, v_hist, v_hist, v_own, bucket, rel_bias, sinks)


def _router_kernel(x_ref, g_ref, wr_ref, info_ref, cnt_ref):
    tm = x_ref.shape[0]

    @pl.when(pl.program_id(0) == 0)
    def _():
        cnt_ref[...] = jnp.zeros_like(cnt_ref)

    xn = _rms(x_ref[...], g_ref[...])
    hi = xn.astype(BF16)
    lo = (xn - hi.astype(F32)).astype(BF16)
    w = wr_ref[...]
    whi = w.astype(BF16)
    wlo = (w - whi.astype(F32)).astype(BF16)
    logits = _dot(hi, whi) + _dot(hi, wlo) + _dot(lo, whi)
    lane = lax.broadcasted_iota(jnp.int32, (tm, LANES), 1)
    logits = jnp.where(lane < N_EXPERTS, logits, NEG_INF)
    m1 = jnp.max(logits, axis=1, keepdims=True)
    i1 = jnp.min(jnp.where(logits == m1, lane, LANES), axis=1, keepdims=True)
    rest = jnp.where(lane == i1, NEG_INF, logits)
    m2 = jnp.max(rest, axis=1, keepdims=True)
    i2 = jnp.min(jnp.where(rest == m2, lane, LANES), axis=1, keepdims=True)
    e2 = jnp.exp(m2 - m1)
    g1 = 1.0 / (1.0 + e2)
    g2 = e2 / (1.0 + e2)
    sel = (lane == i1) | (lane == i2)
    tri = (lax.broadcasted_iota(jnp.int32, (tm, tm), 0) > lax.broadcasted_iota(jnp.int32, (tm, tm), 1)).astype(BF16)
    rank = cnt_ref[...] + _dot(tri, sel.astype(BF16))
    r1 = jnp.sum(jnp.where(lane == i1, rank, 0.0), axis=1, keepdims=True)
    r2 = jnp.sum(jnp.where(lane == i2, rank, 0.0), axis=1, keepdims=True)
    cnt_ref[...] = cnt_ref[...] + jnp.sum(sel.astype(F32), axis=0, keepdims=True)
    info = jnp.where(lane == 0, i1.astype(F32), 0.0)
    info = jnp.where(lane == 1, i2.astype(F32), info)
    info = jnp.where(lane == 2, g1, info)
    info = jnp.where(lane == 3, g2, info)
    info = jnp.where(lane == 4, r1, info)
    info = jnp.where(lane == 5, r2, info)
    info_ref[...] = info


def _router(x, g, wr, tm):
    n = x.shape[0]
    return pl.pallas_call(
        _router_kernel,
        grid=(n // tm,),
        in_specs=[_rows(tm, D), _full((1, D)), _full(wr.shape)],
        out_specs=[_rows(tm, LANES), _full((1, LANES))],
        out_shape=[jax.ShapeDtypeStruct((n, LANES), F32), jax.ShapeDtypeStruct((1, LANES), F32)],
        compiler_params=_params(("arbitrary",)),
        name="moe_router",
    )(x, g, wr)


def _row_copy(src, dst, sem):
    return pltpu.make_async_copy(src, dst, sem)


def _moe_scatter_kernel(te, zpos_ref, dest_ref, x_ref, xs_hbm, zero_s, sem):
    tm = x_ref.shape[0]

    @pl.when(pl.program_id(0) == 0)
    def _():
        zero_s[...] = jnp.zeros_like(zero_s)
        for e in range(N_EXPERTS):
            _row_copy(zero_s, xs_hbm.at[pl.ds(pl.multiple_of(zpos_ref[e], te), te)], sem).start()
        for e in range(N_EXPERTS):
            _row_copy(zero_s, xs_hbm.at[pl.ds(pl.multiple_of(zpos_ref[e], te), te)], sem).wait()

        def clear_tail(t, carry):
            tail = _row_copy(zero_s, xs_hbm.at[pl.ds(pl.multiple_of(t * te, te), te)], sem)
            tail.start()
            tail.wait()
            return carry

        lax.fori_loop(zpos_ref[N_EXPERTS] // te, xs_hbm.shape[0] // te, clear_tail, 0)

    def start(r, carry):
        for s in range(2):
            _row_copy(x_ref.at[pl.ds(r, 1)], xs_hbm.at[pl.ds(dest_ref[s * tm + r], 1)], sem).start()
        return carry

    def wait(r, carry):
        for s in range(2):
            _row_copy(x_ref.at[pl.ds(r, 1)], xs_hbm.at[pl.ds(dest_ref[s * tm + r], 1)], sem).wait()
        return carry

    lax.fori_loop(0, tm, start, 0)
    lax.fori_loop(0, tm, wait, 0)


def _moe_scatter(x, dest, zpos, n_rows, tm, te):
    n = x.shape[0]
    return pl.pallas_call(
        functools.partial(_moe_scatter_kernel, te),
        grid_spec=pltpu.PrefetchScalarGridSpec(
            num_scalar_prefetch=1,
            grid=(n // tm,),
            in_specs=[pl.BlockSpec((2 * tm,), lambda i, zp: (i,), memory_space=pltpu.SMEM),
                      pl.BlockSpec((tm, D), lambda i, zp: (i, 0))],
            out_specs=pl.BlockSpec(memory_space=pl.ANY),
            scratch_shapes=[pltpu.VMEM((te, D), F32), pltpu.SemaphoreType.DMA(())],
        ),
        out_shape=jax.ShapeDtypeStruct((n_rows, D), F32),
        compiler_params=_params(("arbitrary",)),
        name="moe_scatter",
    )(zpos, dest, x)


def _moe_ffn_kernel(n_ff, te_ref, nu_ref, xs_ref, g_ref, w1_ref, w3_ref, w2_ref, ys_ref, xn_s, acc_s):
    i, j = pl.program_id(0), pl.program_id(1)
    used = i < nu_ref[0]

    @pl.when(used & (j == 0))
    def _():
        xn_s[...] = _rms(xs_ref[...], g_ref[...]).astype(BF16)
        acc_s[...] = jnp.zeros_like(acc_s)

    @pl.when(used)
    def _():
        xn = xn_s[...]
        a = (_silu(_dot(xn, w1_ref[0])) * _dot(xn, w3_ref[0])).astype(BF16)
        acc_s[...] += _dot(a, w2_ref[0])

    @pl.when(j == n_ff - 1)
    def _():
        ys_ref[...] = jnp.where(used, acc_s[...], 0.0)


def _moe_ffn(xs, g, w1, w3, w2, tile_expert, n_used, te):
    n_tiles = tile_expert.shape[0]
    d_ff = w1.shape[2]
    n_ff = FF_SPLIT_MOE
    tf = d_ff // n_ff

    def row_map(i, j, te_r, nu_r):
        return (jnp.minimum(i, nu_r[0] - 1), 0)

    def ff(i, j, nu_r):
        return jnp.where(i < nu_r[0], j, n_ff - 1)

    return pl.pallas_call(
        functools.partial(_moe_ffn_kernel, n_ff),
        grid_spec=pltpu.PrefetchScalarGridSpec(
            num_scalar_prefetch=2,
            grid=(n_tiles, n_ff),
            in_specs=[pl.BlockSpec((te, D), row_map),
                      pl.BlockSpec((1, D), lambda i, j, te_r, nu_r: (0, 0)),
                      pl.BlockSpec((1, D, tf), lambda i, j, te_r, nu_r: (te_r[i], 0, ff(i, j, nu_r))),
                      pl.BlockSpec((1, D, tf), lambda i, j, te_r, nu_r: (te_r[i], 0, ff(i, j, nu_r))),
                      pl.BlockSpec((1, tf, D), lambda i, j, te_r, nu_r: (te_r[i], ff(i, j, nu_r), 0))],
            out_specs=pl.BlockSpec((te, D), lambda i, j, te_r, nu_r: (i, 0)),
            scratch_shapes=[pltpu.VMEM((te, D), BF16), pltpu.VMEM((te, D), F32)],
        ),
        out_shape=jax.ShapeDtypeStruct((n_tiles * te, D), F32),
        compiler_params=_params(("arbitrary", "arbitrary")),
        name="moe_ffn",
    )(tile_expert, n_used, xs, g, w1, w3, w2)


def _moe_combine_kernel(n_p, dest_ref, info_ref, x_ref, ys_hbm, op_ref, os_ref, buf, sem):
    tm = x_ref.shape[0]
    i = pl.program_id(0)

    def start(r, carry):
        for s in range(2):
            _row_copy(ys_hbm.at[pl.ds(dest_ref[s * tm + r], 1)], buf.at[s, pl.ds(r, 1)], sem).start()
        return carry

    def wait(r, carry):
        for s in range(2):
            _row_copy(ys_hbm.at[pl.ds(dest_ref[s * tm + r], 1)], buf.at[s, pl.ds(r, 1)], sem).wait()
        return carry

    lax.fori_loop(0, tm, start, 0)
    lax.fori_loop(0, tm, wait, 0)
    out = x_ref[...] + info_ref[:, 2:3] * buf[0] + info_ref[:, 3:4] * buf[1]

    @pl.when(i < n_p)
    def _():
        op_ref[...] = out

    @pl.when(i >= n_p)
    def _():
        os_ref[...] = out


def _moe_combine(x, info, dest, ys, tm, n_p, n_s):
    out_p, out_s = _pair(tm, D, n_p)
    return pl.pallas_call(
        functools.partial(_moe_combine_kernel, n_p),
        grid=(n_p + n_s,),
        in_specs=[pl.BlockSpec((2 * tm,), lambda i: (i,), memory_space=pltpu.SMEM),
                  _rows(tm, LANES), _rows(tm, D), pl.BlockSpec(memory_space=pl.ANY)],
        out_specs=[out_p, out_s],
        out_shape=[jax.ShapeDtypeStruct((n_p * tm, D), F32), jax.ShapeDtypeStruct((n_s * tm, D), F32)],
        scratch_shapes=[pltpu.VMEM((2, tm, D), F32), pltpu.SemaphoreType.DMA(())],
        compiler_params=_params(("arbitrary",)),
        name="moe_combine",
    )(dest, info, x, ys)


def _moe(x, g, w_router, w1, w3, w2, tm, n_p, n_s):
    n = x.shape[0]
    te = min(EXPERT_TILE, tm)
    wr = jnp.zeros((D, LANES), F32).at[:, :N_EXPERTS].set(w_router)
    info, cnt = _router(x, g, wr, tm)
    e1, e2 = info[:, 0].astype(jnp.int32), info[:, 1].astype(jnp.int32)
    r1, r2 = info[:, 4].astype(jnp.int32), info[:, 5].astype(jnp.int32)
    counts = cnt[0, :N_EXPERTS].astype(jnp.int32)
    padded = (counts + te - 1) // te * te
    ends = jnp.cumsum(padded)
    starts = ends - padded
    dest = jnp.stack([(starts[e1] + r1).reshape(n // tm, tm), (starts[e2] + r2).reshape(n // tm, tm)], axis=1)
    dest = dest.reshape(-1)
    n_tiles = (2 * n + N_EXPERTS * (te - 1)) // te
    tile_start = jnp.arange(n_tiles, dtype=jnp.int32) * te
    tile_expert = jnp.minimum(jnp.sum(tile_start[:, None] >= ends[None, :], axis=1), N_EXPERTS - 1).astype(jnp.int32)
    n_used = (ends[-1:] // te).astype(jnp.int32)
    last_tile = jnp.maximum(ends - te, 0).astype(jnp.int32)
    xs = _moe_scatter(x, dest, jnp.concatenate([last_tile, ends[-1:].astype(jnp.int32)]), n_tiles * te, tm, te)
    ys = _moe_ffn(xs, g, w1, w3, w2, tile_expert, n_used, te)
    return _moe_combine(x, info, dest, ys, tm, n_p, n_s)


def _t5_bucket(rel):
    n = -rel
    half = NUM_BUCKETS // 2
    ret = jnp.where(n < 0, half, 0)
    n = jnp.abs(n)
    max_exact = half // 2
    nf = jnp.maximum(n, 1).astype(F32)
    large = max_exact + (jnp.log(nf / max_exact) / math.log(MAX_DISTANCE / max_exact)
                         * (half - max_exact)).astype(jnp.int32)
    large = jnp.minimum(large, half - 1)
    return ret + jnp.where(n < max_exact, n, large)


def _bucket_table(length):
    kpos = jnp.arange(WINDOW + length) - WINDOW
    return _t5_bucket(kpos[None, :] - jnp.arange(length)[:, None]).astype(jnp.int32)


def kernel(x_prompt, x_sample, state_conv_a, state_rglru_h, state_conv_b, state_mlstm_c, state_mlstm_n, state_mlstm_m, cache_swa_k, cache_swa_v, norm_mix, norm_ffn, w_in_ab, w_conv_a, b_conv_a, w_rg_a, b_rg_a, w_rg_x, b_rg_x, rg_lambda, w_conv_b, b_conv_b, w_q_b, w_k_b, w_v_b, w_if_b, b_if_b, g_hnorm_b, skip_b, w_out_ab, w1_dense, w3_dense, w2_dense, w_in_att, g_qnorm, g_knorm, sinks, w_out_att, rel_bias, w_router, w1_moe, w3_moe, w2_moe):
    bp, tp, _ = x_prompt.shape
    bs, ts, _ = x_sample.shape
    assert norm_mix.shape[0] == 2 and w_in_ab.shape[0] == 1 and w_in_att.shape[0] == 1
    assert tp % CHUNK == 0 and ts <= CHUNK and cache_swa_k.shape[2] == WINDOW
    rows_p, rows_s = bp * tp, bs * ts
    tm = math.gcd(math.gcd(rows_p, rows_s), TOKEN_TILE)
    n_p, n_s = rows_p // tm, rows_s // tm
    xp = x_prompt.reshape(rows_p, D)
    xs = x_sample.reshape(rows_s, D)
    bf = lambda w: w.astype(BF16)
    vec = lambda v: v.reshape(1, -1)

    xa, ga, xm, z = _in_proj_ab(xp, xs, vec(norm_mix[0]), bf(w_in_ab[0]), tm)
    rg_w = (w_conv_a[0], vec(b_conv_a[0]), bf(w_rg_a[0]), vec(b_rg_a[0]), bf(w_rg_x[0]), vec(b_rg_x[0]),
            vec(rg_lambda[0]))
    ya_p, hl_p = _rglru(xa, ga, 0, tp, jnp.zeros((bp, CONV_W - 1, D), F32), jnp.zeros((bp, 1, D), F32), *rg_w)
    ya_s, hl_s = _rglru(xa, ga, rows_p, ts, state_conv_a[0], state_rglru_h[0].reshape(bs, 1, D), *rg_w)

    pre_w = (w_conv_b[0], vec(b_conv_b[0]), bf(w_q_b[0]), bf(w_k_b[0]), bf(w_v_b[0]), bf(w_if_b[0].T),
             b_if_b[0].reshape(2 * NH_B, 1))
    rec_w = (g_hnorm_b[0].reshape(1, D), vec(skip_b[0]))
    lanes = lambda m: jnp.broadcast_to(m[:, :, None], m.shape + (LANES,))
    pre_p = _mlstm_pre(xm, 0, tp, jnp.zeros((bp, CONV_W - 1, D), F32), *pre_w)
    pre_s = _mlstm_pre(xm, rows_p, ts, state_conv_b[0], *pre_w)
    yb_p, c_p, nn_p, m_p = _mlstm(*pre_p[1:], pre_p[0], z, 0, tp, jnp.zeros((bp, NH_B, DH_B, DH_B), F32),
                                  jnp.zeros((bp, NH_B, DH_B), F32), jnp.zeros((bp, NH_B, LANES), F32), *rec_w)
    yb_s, c_s, nn_s, m_s = _mlstm(*pre_s[1:], pre_s[0], z, rows_p, ts, state_mlstm_c[0], state_mlstm_n[0],
                                  lanes(state_mlstm_m[0]), *rec_w)

    w_out = bf(w_out_ab[0])
    y = _out_proj((xp, xs), [(ya_p, ya_s), (yb_p, yb_s)], [w_out[:D], w_out[D:]], tm, n_p)
    y = _ffn_dense(y, vec(norm_ffn[0]), bf(w1_dense[0]), bf(w3_dense[0]), bf(w2_dense[0]), tm)

    tile2 = lambda gain: jnp.tile(gain, D_KV // HEAD_DIM).reshape(1, D_KV)
    q, k, v = _in_proj_att(y, vec(norm_mix[1]), bf(w_in_att[0]), tile2(g_qnorm[0]), tile2(g_knorm[0]), tm)
    half = WINDOW // 2
    ncp = tp // CHUNK
    sinks2 = sinks[0].reshape(1, N_HEADS)
    hist_p = (lambda bi, ci: (bi * ncp + jnp.maximum(ci - 2, 0), 0), lambda bi, ci: (bi * ncp + jnp.maximum(ci - 1, 0), 0))
    o_p = _attention(q, k, v, k, v, hist_p, 0, tp, bp, True, _bucket_table(CHUNK), rel_bias, sinks2)
    ck = cache_swa_k[0].reshape(bs * WINDOW, D_KV)
    cv = cache_swa_v[0].reshape(bs * WINDOW, D_KV)
    hist_s = (lambda bi, ci: (2 * bi, 0), lambda bi, ci: (2 * bi + 1, 0))
    o_s = _attention(q, k, v, ck, cv, hist_s, rows_p, ts, bs, False, _bucket_table(ts), rel_bias, sinks2)
    y = _out_proj(y, [(o_p, o_s)], [bf(w_out_att[0])], tm, n_p)
    yp, ys = _moe(y, vec(norm_ffn[1]), w_router[0], bf(w1_moe[0]), bf(w3_moe[0]), bf(w2_moe[0]), tm, n_p, n_s)

    def tail(a, rows, b, t, keep):
        return a[rows:rows + b * t].reshape(b, t, -1)[:, t - keep:]

    kv4 = lambda a, b: a.reshape(b, -1, N_KV, HEAD_DIM)
    one = lambda a: a[None]
    k_s = jnp.concatenate([cache_swa_k[0][:, ts:], kv4(k[rows_p:], bs)], axis=1)
    v_s = jnp.concatenate([cache_swa_v[0][:, ts:], kv4(v[rows_p:], bs)], axis=1)
    return (yp.reshape(bp, tp, D), ys.reshape(bs, ts, D),
            one(tail(xa, 0, bp, tp, CONV_W - 1)), one(hl_p.reshape(bp, D)), one(tail(xm, 0, bp, tp, CONV_W - 1)),
            one(c_p), one(nn_p), one(m_p[:, :, 0]),
            one(kv4(tail(k, 0, bp, tp, WINDOW), bp)), one(kv4(tail(v, 0, bp, tp, WINDOW), bp)),
            one(tail(xa, rows_p, bs, ts, CONV_W - 1)), one(hl_s.reshape(bs, D)),
            one(tail(xm, rows_p, bs, ts, CONV_W - 1)), one(c_s), one(nn_s), one(m_s[:, :, 0]),
            one(k_s), one(v_s))
```

```python
import functools
import math

import jax
import jax.numpy as jnp
from jax import lax
from jax.experimental import pallas as pl
from jax.experimental.pallas import tpu as pltpu

F32 = jnp.float32
BF16 = jnp.bfloat16

D = 1024
CHUNK = 64
CONV_W = 4
NB_A = 8
BS_A = D // NB_A
RG_C = 8.0
NH_B = 4
DH_B = D // NH_B
N_HEADS = 16
HEAD_DIM = D // N_HEADS
N_KV = 4
GROUP = N_HEADS // N_KV
D_KV = N_KV * HEAD_DIM
WINDOW = 128
NUM_BUCKETS = 32
MAX_DISTANCE = 128
N_EXPERTS = 8
EPS = 1e-6
LANES = 128
NEG_INF = float("-inf")

TOKEN_TILE = 512
EXPERT_TILE = 512
FF_SPLIT_DENSE = 3
FF_SPLIT_MOE = 2
SEQ_BLOCK = 256
VMEM_LIMIT = 56 * 1024 * 1024


def _params(sem):
    return pltpu.CompilerParams(dimension_semantics=sem, vmem_limit_bytes=VMEM_LIMIT)


def _full(shape):
    return pl.BlockSpec(shape, lambda *_: (0,) * len(shape))


def _resident(shape):
    return pl.BlockSpec(shape, lambda *_: (0,) * len(shape), pipeline_mode=pl.Buffered(1))


def _rows(tm, c):
    return pl.BlockSpec((tm, c), lambda i: (i, 0))


def _pair(tm, c, n_p):
    return [pl.BlockSpec((tm, c), lambda i: (jnp.minimum(i, n_p - 1), 0)),
            pl.BlockSpec((tm, c), lambda i: (jnp.maximum(i - n_p, 0), 0))]


def _rms(x, g):
    ms = jnp.mean(x * x, axis=-1, keepdims=True)
    return x * lax.rsqrt(ms + EPS) * g


def _silu(x):
    return x * jax.nn.sigmoid(x)


def _softplus(x):
    return jnp.maximum(x, 0.0) + jnp.log1p(jnp.exp(-jnp.abs(x)))


def _gelu_tanh(x):
    c = math.sqrt(2.0 / math.pi)
    return x * (0.5 * (1.0 + jnp.tanh(c * (x + 0.044715 * (x * x * x)))))


def _dot(a, b):
    return jnp.dot(a, b, preferred_element_type=F32)


def _dot_nt(a, b):
    return lax.dot_general(a, b, (((1,), (1,)), ((), ())), preferred_element_type=F32)


def _dot_tn(a, b):
    return lax.dot_general(a, b, (((0,), (0,)), ((), ())), preferred_element_type=F32)


def _in_proj_ab_kernel(n_p, xp_ref, xs_ref, g_ref, w_ref, *o_refs):
    i = pl.program_id(0)
    x = jnp.where(i < n_p, xp_ref[...], xs_ref[...])
    xn = _rms(x, g_ref[...]).astype(BF16)
    for c, o_ref in enumerate(o_refs):
        o_ref[...] = _dot(xn, w_ref[:, c * D:(c + 1) * D])


def _in_proj_ab(xp, xs, g, w, tm):
    n_p, n_s = xp.shape[0] // tm, xs.shape[0] // tm
    n = xp.shape[0] + xs.shape[0]
    n_out = w.shape[1] // D
    return pl.pallas_call(
        functools.partial(_in_proj_ab_kernel, n_p),
        grid=(n_p + n_s,),
        in_specs=_pair(tm, D, n_p) + [_full((1, D)), _resident(w.shape)],
        out_specs=[_rows(tm, D)] * n_out,
        out_shape=[jax.ShapeDtypeStruct((n, D), F32)] * n_out,
        compiler_params=_params(("parallel",)),
        name="in_proj_ab",
    )(xp, xs, g, w)


def _out_proj_kernel(n_p, n_res, n_a, *refs):
    i = pl.program_id(0)
    res_refs = refs[:n_res]
    a_refs = refs[n_res:n_res + 2 * n_a]
    w_refs = refs[n_res + 2 * n_a:n_res + 3 * n_a]
    o_ref = refs[-1]
    if n_res == 2:
        acc = jnp.where(i < n_p, res_refs[0][...], res_refs[1][...])
    else:
        acc = res_refs[0][...]
    for k in range(n_a):
        a = jnp.where(i < n_p, a_refs[2 * k][...], a_refs[2 * k + 1][...])
        acc = acc + _dot(a, w_refs[k][...])
    o_ref[...] = acc


def _out_proj(res, a_pairs, ws, tm, n_p):
    res = res if isinstance(res, tuple) else (res,)
    n = sum(a.shape[0] for a in a_pairs[0])
    res_specs = _pair(tm, D, n_p) if len(res) == 2 else [_rows(tm, D)]
    a_specs = []
    for a in a_pairs:
        a_specs += _pair(tm, a[0].shape[1], n_p)
    return pl.pallas_call(
        functools.partial(_out_proj_kernel, n_p, len(res), len(a_pairs)),
        grid=(n // tm,),
        in_specs=res_specs + a_specs + [_resident(w.shape) for w in ws],
        out_specs=_rows(tm, D),
        out_shape=jax.ShapeDtypeStruct((n, D), F32),
        compiler_params=_params(("parallel",)),
        name="out_proj",
    )(*res, *[x for a in a_pairs for x in a], *ws)


def _ffn_dense_kernel(x_ref, g_ref, w1_ref, w3_ref, w2_ref, o_ref):
    x = x_ref[...]
    xn = _rms(x, g_ref[...]).astype(BF16)
    step = w1_ref.shape[1] // FF_SPLIT_DENSE
    acc = x
    for c in range(FF_SPLIT_DENSE):
        sl = slice(c * step, (c + 1) * step)
        a = (_silu(_dot(xn, w1_ref[:, sl])) * _dot(xn, w3_ref[:, sl])).astype(BF16)
        acc = acc + _dot(a, w2_ref[sl, :])
    o_ref[...] = acc


def _ffn_dense(x, g, w1, w3, w2, tm):
    n = x.shape[0]
    return pl.pallas_call(
        _ffn_dense_kernel,
        grid=(n // tm,),
        in_specs=[_rows(tm, D), _full((1, D)), _resident(w1.shape), _resident(w3.shape), _resident(w2.shape)],
        out_specs=_rows(tm, D),
        out_shape=jax.ShapeDtypeStruct((n, D), F32),
        compiler_params=_params(("parallel",)),
        name="ffn_dense",
    )(x, g, w1, w3, w2)


def _in_proj_att_kernel(x_ref, g_ref, w_ref, gq_ref, gk_ref, q_ref, kw_ref, vw_ref, k_ref, v_ref):
    xn = _rms(x_ref[...], g_ref[...]).astype(BF16)
    cw = D_KV
    r = lax.broadcasted_iota(jnp.int32, (cw, cw), 0) // HEAD_DIM
    c = lax.broadcasted_iota(jnp.int32, (cw, cw), 1) // HEAD_DIM
    group_ones = (r == c).astype(BF16)
    er = lax.broadcasted_iota(jnp.int32, (D_KV, D), 0)
    ec = lax.broadcasted_iota(jnp.int32, (D_KV, D), 1)
    widen = ((er // HEAD_DIM == ec // (GROUP * HEAD_DIM)) & (er % HEAD_DIM == ec % HEAD_DIM)).astype(BF16)

    def head_norm(y, gain):
        ms = _dot((y * y).astype(BF16), group_ones) * (1.0 / HEAD_DIM)
        return y * lax.rsqrt(ms + EPS) * gain

    for b in range(D // cw):
        y = _dot(xn, w_ref[:, b * cw:(b + 1) * cw])
        q_ref[:, b * cw:(b + 1) * cw] = (head_norm(y, gq_ref[...]) * (HEAD_DIM ** -0.5)).astype(BF16)
    k = head_norm(_dot(xn, w_ref[:, D:D + D_KV]), gk_ref[...])
    v = _dot(xn, w_ref[:, D + D_KV:D + 2 * D_KV])
    k_ref[...] = k
    v_ref[...] = v
    kw_ref[...] = _dot(k.astype(BF16), widen).astype(BF16)
    vw_ref[...] = _dot(v.astype(BF16), widen).astype(BF16)


def _in_proj_att(x, g, w, gq, gk, tm):
    n = x.shape[0]
    return pl.pallas_call(
        _in_proj_att_kernel,
        grid=(n // tm,),
        in_specs=[_rows(tm, D), _full((1, D)), _resident(w.shape), _full((1, D_KV)), _full((1, D_KV))],
        out_specs=[_rows(tm, D)] * 3 + [_rows(tm, D_KV)] * 2,
        out_shape=[jax.ShapeDtypeStruct((n, D), BF16)] * 3 + [jax.ShapeDtypeStruct((n, D_KV), F32)] * 2,
        compiler_params=_params(("parallel",)),
        name="in_proj_att",
    )(x, g, w, gq, gk)


def _causal_conv(x, xbuf, conv0_ref, wc_ref, bc_ref, first):
    tb = x.shape[0]

    @pl.when(first)
    def _():
        xbuf[5:8, :] = conv0_ref[0]

    xbuf[8:8 + tb, :] = x
    y = bc_ref[...] + xbuf[5:5 + tb, :] * wc_ref[0:1, :]
    for j in range(1, CONV_W):
        y = y + xbuf[5 + j:5 + j + tb, :] * wc_ref[j:j + 1, :]
    xbuf[5:8, :] = xbuf[5 + tb:8 + tb, :]
    return y


def _seq_specs(bsz, seq, tb, row0):
    nc = seq // tb
    off = row0 // tb
    return nc, (lambda bi, ci: (off + bi * nc + ci, 0)), (lambda bi, ci: (bi * nc + ci, 0))


def _rglru_kernel(xa_ref, ga_ref, conv0_ref, h0_ref, wc_ref, bc_ref, wa_ref, ba_ref, wx_ref, bx_ref, lam_ref,
                  ya_ref, hl_ref, xbuf):
    first = pl.program_id(1) == 0
    tb = xa_ref.shape[0]

    @pl.when(first)
    def _():
        hl_ref[0] = h0_ref[0]

    y = _causal_conv(xa_ref[...], xbuf, conv0_ref, wc_ref, bc_ref, first)
    row = lax.broadcasted_iota(jnp.int32, (tb, BS_A), 0)
    for n in range(NB_A):
        sl = slice(n * BS_A, (n + 1) * BS_A)
        yn = y[:, sl]
        yb = yn.astype(BF16)
        r = jax.nn.sigmoid(_dot(yb, wa_ref[n]) + ba_ref[:, sl])
        ig = jax.nn.sigmoid(_dot(yb, wx_ref[n]) + bx_ref[:, sl])
        log_a = (-RG_C) * r * _softplus(-lam_ref[:, sl])
        a = jnp.exp(log_a)
        th = jnp.tanh(log_a)
        u = yn * ig * jnp.sqrt(-2.0 * th / (1.0 - th))
        d = 1
        while d < tb:
            keep = row >= d
            u = jnp.where(keep, a * pltpu.roll(u, d, 0) + u, u)
            a = jnp.where(keep, a * pltpu.roll(a, d, 0), a)
            d *= 2
        h = a * hl_ref[0, :, sl] + u
        hl_ref[0, :, sl] = h[tb - 1:tb, :]
        ya_ref[:, sl] = (h * _gelu_tanh(ga_ref[:, sl])).astype(BF16)


def _rglru(xa, ga, row0, seq, conv0, h0, wc, bc, wa, ba, wx, bx, lam):
    bsz = conv0.shape[0]
    tb = min(seq, SEQ_BLOCK)
    nc, in_map, out_map = _seq_specs(bsz, seq, tb, row0)
    state3 = lambda bi, ci: (bi, 0, 0)
    return pl.pallas_call(
        _rglru_kernel,
        grid=(bsz, nc),
        in_specs=[pl.BlockSpec((tb, D), in_map), pl.BlockSpec((tb, D), in_map),
                  pl.BlockSpec((1, CONV_W - 1, D), state3), pl.BlockSpec((1, 1, D), state3),
                  _full(wc.shape), _full(bc.shape), _full(wa.shape), _full(ba.shape), _full(wx.shape),
                  _full(bx.shape), _full(lam.shape)],
        out_specs=[pl.BlockSpec((tb, D), out_map), pl.BlockSpec((1, 1, D), state3)],
        out_shape=[jax.ShapeDtypeStruct((bsz * seq, D), BF16), jax.ShapeDtypeStruct((bsz, 1, D), F32)],
        scratch_shapes=[pltpu.VMEM((tb + 8, D), F32)],
        compiler_params=_params(("parallel", "arbitrary")),
        name="rglru",
    )(xa, ga, conv0, h0, wc, bc, wa, ba, wx, bx, lam)


def _mlstm_pre_kernel(chunk, xm_ref, conv0_ref, wc_ref, bc_ref, wq_ref, wk_ref, wv_ref, wif_ref, bif_ref,
                      xc_ref, q_ref, k_ref, v_ref, g_ref, xbuf):
    first = pl.program_id(1) == 0
    tb = xm_ref.shape[0]
    x = xm_ref[...]
    xc = _silu(_causal_conv(x, xbuf, conv0_ref, wc_ref, bc_ref, first))
    xc_ref[...] = xc
    gates = bif_ref[...]
    for h in range(NH_B):
        sl = slice(h * DH_B, (h + 1) * DH_B)
        xch = xc[:, sl].astype(BF16)
        q = _dot(xch, wq_ref[h]).astype(BF16)
        k = _dot(xch, wk_ref[h]).astype(BF16)
        v = _dot(x[:, sl].astype(BF16), wv_ref[h]).astype(BF16)
        q_ref[:, sl] = q
        k_ref[:, sl] = k
        v_ref[:, sl] = v
        gates = (gates + _dot_nt(wif_ref[:, sl], q) + _dot_nt(wif_ref[:, D + h * DH_B:D + (h + 1) * DH_B], k)
                 + _dot_nt(wif_ref[:, 2 * D + h * DH_B:2 * D + (h + 1) * DH_B], v))
    row = lax.broadcasted_iota(jnp.int32, gates.shape, 0)
    gates = jnp.where(row < NH_B, gates, -_softplus(-gates))
    for j in range(tb // chunk):
        g_ref[j] = gates[:, j * chunk:(j + 1) * chunk]


def _mlstm_pre(xm, row0, seq, conv0, wc, bc, wq, wk, wv, wif_t, bif):
    bsz = conv0.shape[0]
    tb = min(seq, SEQ_BLOCK)
    chunk = min(seq, CHUNK)
    nc, in_map, out_map = _seq_specs(bsz, seq, tb, row0)
    n = bsz * seq
    return pl.pallas_call(
        functools.partial(_mlstm_pre_kernel, chunk),
        grid=(bsz, nc),
        in_specs=[pl.BlockSpec((tb, D), in_map), pl.BlockSpec((1, CONV_W - 1, D), lambda bi, ci: (bi, 0, 0)),
                  _full(wc.shape), _full(bc.shape), _full(wq.shape), _full(wk.shape), _full(wv.shape),
                  _full(wif_t.shape), _full(bif.shape)],
        out_specs=[pl.BlockSpec((tb, D), out_map)] * 4
                  + [pl.BlockSpec((tb // chunk, 2 * NH_B, chunk), lambda bi, ci: (bi * nc + ci, 0, 0))],
        out_shape=[jax.ShapeDtypeStruct((n, D), F32)] + [jax.ShapeDtypeStruct((n, D), BF16)] * 3
                  + [jax.ShapeDtypeStruct((n // chunk, 2 * NH_B, chunk), F32)],
        scratch_shapes=[pltpu.VMEM((tb + 8, D), F32)],
        compiler_params=_params(("parallel", "arbitrary")),
        name="mlstm_pre",
    )(xm, conv0, wc, bc, wq, wk, wv, wif_t, bif)


def _mlstm_kernel(chunk, q_ref, k_ref, v_ref, g_ref, xc_ref, z_ref, c0_ref, n0_ref, m0_ref, gh_ref, skip_ref,
                  yb_ref, c_ref, n_ref, m_ref):
    tb = q_ref.shape[0]
    L = chunk

    @pl.when(pl.program_id(1) == 0)
    def _():
        c_ref[...] = c0_ref[...]
        n_ref[...] = n0_ref[...]
        m_ref[...] = m0_ref[...]

    tt = lax.broadcasted_iota(jnp.int32, (L, L), 0)
    ss = lax.broadcasted_iota(jnp.int32, (L, L), 1)
    eye = tt == ss
    causal = ss <= tt

    def to_col(row_vec):
        return jnp.sum(jnp.where(eye, row_vec, 0.0), axis=1, keepdims=True)

    def chunk_step(j):
        r0 = pl.multiple_of(j * L, L)
        rows = pl.ds(r0, L)
        g = g_ref[j]
        for h in range(NH_B):
            sl = slice(h * DH_B, (h + 1) * DH_B)
            q = q_ref[rows, sl]
            k = k_ref[rows, sl]
            v = v_ref[rows, sl]
            i_row = g[h:h + 1, :]
            f_row = g[NH_B + h:NH_B + h + 1, :]
            f_col = to_col(f_row)
            i_col = to_col(i_row)
            bt_col = jnp.sum(jnp.where(causal, f_row, 0.0), axis=1, keepdims=True)
            bt_row = jnp.sum(jnp.where(tt <= ss, f_col, 0.0), axis=0, keepdims=True)
            m_prev = m_ref[0, h:h + 1, 0:1]
            dmat = jnp.where(causal, bt_col - bt_row + i_row, NEG_INF)
            m_inter = bt_col + m_prev
            m_t = jnp.maximum(m_inter, jnp.max(dmat, axis=1, keepdims=True))
            p = _dot_nt(q, k) * (DH_B ** -0.5) * jnp.exp(dmat - m_t)
            sc = jnp.exp(m_inter - m_t)
            c_old = c_ref[0, h]
            n_old = n_ref[0, h:h + 1, :]
            num = _dot(p.astype(BF16), v) + sc * _dot(q, c_old.astype(BF16))
            den = jnp.sum(p, axis=1, keepdims=True) + sc * jnp.sum(q.astype(F32) * n_old, axis=1, keepdims=True)
            hs = num / jnp.maximum(jnp.abs(den), jnp.exp(-m_t))
            m_new = m_t[L - 1:L, :]
            bt_last = bt_col[L - 1:L, :]
            decay = jnp.exp(bt_last - bt_col + i_col - m_new)
            keep = jnp.exp(bt_last + m_prev - m_new)
            kd = k.astype(F32) * (decay * (DH_B ** -0.5))
            c_ref[0, h] = keep * c_old + _dot_tn(kd.astype(BF16), v)
            n_ref[0, h:h + 1, :] = keep * n_old + jnp.sum(kd, axis=0, keepdims=True)
            m_ref[0, h:h + 1, :] = jnp.broadcast_to(m_new, (1, LANES))
            mu = jnp.mean(hs, axis=1, keepdims=True)
            dev = hs - mu
            var = jnp.mean(dev * dev, axis=1, keepdims=True)
            hn = dev * lax.rsqrt(var + EPS) * gh_ref[:, sl]
            out = (hn + skip_ref[:, sl] * xc_ref[rows, sl]) * _silu(z_ref[rows, sl])
            yb_ref[rows, sl] = out.astype(BF16)

    if tb == L:
        chunk_step(0)
    else:
        def body(j, carry):
            chunk_step(j)
            return carry
        lax.fori_loop(0, tb // L, body, 0)


def _mlstm(q, k, v, g, xc, z, row0, seq, c0, n0, m0, gh, skip):
    bsz = c0.shape[0]
    tb = min(seq, SEQ_BLOCK)
    chunk = min(seq, CHUNK)
    nc, z_map, own_map = _seq_specs(bsz, seq, tb, row0)
    st4 = lambda bi, ci: (bi, 0, 0, 0)
    st3 = lambda bi, ci: (bi, 0, 0)
    return pl.pallas_call(
        functools.partial(_mlstm_kernel, chunk),
        grid=(bsz, nc),
        in_specs=[pl.BlockSpec((tb, D), own_map)] * 3
                 + [pl.BlockSpec((tb // chunk, 2 * NH_B, chunk), lambda bi, ci: (bi * nc + ci, 0, 0)),
                    pl.BlockSpec((tb, D), own_map), pl.BlockSpec((tb, D), z_map),
                    pl.BlockSpec((1, NH_B, DH_B, DH_B), st4), pl.BlockSpec((1, NH_B, DH_B), st3),
                    pl.BlockSpec((1, NH_B, LANES), st3), _full(gh.shape), _full(skip.shape)],
        out_specs=[pl.BlockSpec((tb, D), own_map), pl.BlockSpec((1, NH_B, DH_B, DH_B), st4),
                   pl.BlockSpec((1, NH_B, DH_B), st3), pl.BlockSpec((1, NH_B, LANES), st3)],
        out_shape=[jax.ShapeDtypeStruct((bsz * seq, D), BF16), jax.ShapeDtypeStruct(c0.shape, F32),
                   jax.ShapeDtypeStruct(n0.shape, F32), jax.ShapeDtypeStruct(m0.shape, F32)],
        compiler_params=_params(("parallel", "arbitrary")),
        name="mlstm",
    )(q, k, v, g, xc, z, c0, n0, m0, gh, skip)


def _attn_kernel(masked, q_ref, kh0_ref, kh1_ref, ko_ref, vh0_ref, vh1_ref, vo_ref, bucket_ref, relb_ref, sink_ref,
                 o_ref, bias_s, s_scr, p_scr):
    L = q_ref.shape[0]
    nk = WINDOW + L
    ci = pl.program_id(1)

    @pl.when((pl.program_id(0) == 0) & (ci == 0))
    def _():
        bucket = bucket_ref[...]
        for h in range(N_HEADS):
            acc = jnp.zeros((L, nk), F32)
            for b in range(NUM_BUCKETS):
                acc = jnp.where(bucket == b, relb_ref[b, h], acc)
            bias_s[h * L:(h + 1) * L, :] = acc

    kcat = jnp.concatenate([kh0_ref[...], kh1_ref[...], ko_ref[...]], axis=0)
    vcat = jnp.concatenate([vh0_ref[...], vh1_ref[...], vo_ref[...]], axis=0)
    gw = GROUP * HEAD_DIM
    slot = lax.broadcasted_iota(jnp.int32, (L, gw), 1) // HEAD_DIM
    if masked:
        kk = lax.broadcasted_iota(jnp.int32, (1, nk), 1)
        half = WINDOW // 2
        valid = (kk >= WINDOW) | ((kk >= half) & (ci >= 1)) | (ci >= 2)
    gl = GROUP * L
    for g in range(N_KV):
        gsl = slice(g * gw, (g + 1) * gw)
        qg = q_ref[:, gsl]
        zero = jnp.zeros_like(qg)
        qs = jnp.concatenate([jnp.where(slot == j, qg, zero) for j in range(GROUP)], axis=0)
        s_scr[g * gl:(g + 1) * gl, :] = _dot_nt(qs, kcat[:, gsl])
    for h in range(N_HEADS):
        rows = slice(h * L, (h + 1) * L)
        s = s_scr[rows, :] + bias_s[rows, :]
        if masked:
            s = jnp.where(valid, s, NEG_INF)
        sink = sink_ref[0, h]
        mx = jnp.maximum(jnp.max(s, axis=1, keepdims=True), sink)
        e = jnp.exp(s - mx)
        den = jnp.sum(e, axis=1, keepdims=True) + jnp.exp(sink - mx)
        p_scr[rows, :] = (e / den).astype(BF16)
    for g in range(N_KV):
        gsl = slice(g * gw, (g + 1) * gw)
        ow = _dot(p_scr[g * gl:(g + 1) * gl, :], vcat[:, gsl])
        og = jnp.where(slot == 0, ow[0:L, :], 0.0)
        for j in range(1, GROUP):
            og = jnp.where(slot == j, ow[j * L:(j + 1) * L, :], og)
        o_ref[:, gsl] = og.astype(BF16)


def _attention(q, k_own, v_own, k_hist, v_hist, hist_maps, row0, seq, bsz, masked, bucket, rel_bias, sinks):
    L = min(seq, CHUNK)
    nc, in_map, out_map = _seq_specs(bsz, seq, L, row0)
    half = WINDOW // 2
    h0_map, h1_map = hist_maps
    smem = functools.partial(pl.BlockSpec, memory_space=pltpu.SMEM)
    return pl.pallas_call(
        functools.partial(_attn_kernel, masked),
        grid=(bsz, nc),
        in_specs=[pl.BlockSpec((L, D), in_map),
                  pl.BlockSpec((half, D), h0_map), pl.BlockSpec((half, D), h1_map), pl.BlockSpec((L, D), in_map),
                  pl.BlockSpec((half, D), h0_map), pl.BlockSpec((half, D), h1_map), pl.BlockSpec((L, D), in_map),
                  _full(bucket.shape), smem(), smem()],
        out_specs=pl.BlockSpec((L, D), out_map),
        out_shape=jax.ShapeDtypeStruct((bsz * seq, D), BF16),
        scratch_shapes=[pltpu.VMEM((N_HEADS * L, WINDOW + L), F32)] * 2 + [pltpu.VMEM((N_HEADS * L, WINDOW + L), BF16)],
        compiler_params=_params(("arbitrary", "arbitrary")),
        name="swa",
    )(q, k_hist, k_hist, k_own, v_hist, v_hist, v_own, bucket, rel_bias, sinks)


def _router_kernel(x_ref, g_ref, wr_ref, info_ref, cnt_ref):
    tm = x_ref.shape[0]

    @pl.when(pl.program_id(0) == 0)
    def _():
        cnt_ref[...] = jnp.zeros_like(cnt_ref)

    xn = _rms(x_ref[...], g_ref[...])
    hi = xn.astype(BF16)
    lo = (xn - hi.astype(F32)).astype(BF16)
    w = wr_ref[...]
    whi = w.astype(BF16)
    wlo = (w - whi.astype(F32)).astype(BF16)
    logits = _dot(hi, whi) + _dot(hi, wlo) + _dot(lo, whi)
    lane = lax.broadcasted_iota(jnp.int32, (tm, LANES), 1)
    logits = jnp.where(lane < N_EXPERTS, logits, NEG_INF)
    m1 = jnp.max(logits, axis=1, keepdims=True)
    i1 = jnp.min(jnp.where(logits == m1, lane, LANES), axis=1, keepdims=True)
    rest = jnp.where(lane == i1, NEG_INF, logits)
    m2 = jnp.max(rest, axis=1, keepdims=True)
    i2 = jnp.min(jnp.where(rest == m2, lane, LANES), axis=1, keepdims=True)
    e2 = jnp.exp(m2 - m1)
    g1 = 1.0 / (1.0 + e2)
    g2 = e2 / (1.0 + e2)
    sel = (lane == i1) | (lane == i2)
    tri = (lax.broadcasted_iota(jnp.int32, (tm, tm), 0) > lax.broadcasted_iota(jnp.int32, (tm, tm), 1)).astype(BF16)
    rank = cnt_ref[...] + _dot(tri, sel.astype(BF16))
    r1 = jnp.sum(jnp.where(lane == i1, rank, 0.0), axis=1, keepdims=True)
    r2 = jnp.sum(jnp.where(lane == i2, rank, 0.0), axis=1, keepdims=True)
    cnt_ref[...] = cnt_ref[...] + jnp.sum(sel.astype(F32), axis=0, keepdims=True)
    info = jnp.where(lane == 0, i1.astype(F32), 0.0)
    info = jnp.where(lane == 1, i2.astype(F32), info)
    info = jnp.where(lane == 2, g1, info)
    info = jnp.where(lane == 3, g2, info)
    info = jnp.where(lane == 4, r1, info)
    info = jnp.where(lane == 5, r2, info)
    info_ref[...] = info


def _router(x, g, wr, tm):
    n = x.shape[0]
    return pl.pallas_call(
        _router_kernel,
        grid=(n // tm,),
        in_specs=[_rows(tm, D), _full((1, D)), _full(wr.shape)],
        out_specs=[_rows(tm, LANES), _full((1, LANES))],
        out_shape=[jax.ShapeDtypeStruct((n, LANES), F32), jax.ShapeDtypeStruct((1, LANES), F32)],
        compiler_params=_params(("arbitrary",)),
        name="moe_router",
    )(x, g, wr)


def _row_copy(src, dst, sem):
    return pltpu.make_async_copy(src, dst, sem)


def _moe_scatter_kernel(te, zpos_ref, dest_ref, x_ref, xs_hbm, zero_s, sem):
    tm = x_ref.shape[0]

    @pl.when(pl.program_id(0) == 0)
    def _():
        zero_s[...] = jnp.zeros_like(zero_s)
        for e in range(N_EXPERTS):
            _row_copy(zero_s, xs_hbm.at[pl.ds(pl.multiple_of(zpos_ref[e], te), te)], sem).start()
        for e in range(N_EXPERTS):
            _row_copy(zero_s, xs_hbm.at[pl.ds(pl.multiple_of(zpos_ref[e], te), te)], sem).wait()

        def clear_tail(t, carry):
            tail = _row_copy(zero_s, xs_hbm.at[pl.ds(pl.multiple_of(t * te, te), te)], sem)
            tail.start()
            tail.wait()
            return carry

        lax.fori_loop(zpos_ref[N_EXPERTS] // te, xs_hbm.shape[0] // te, clear_tail, 0)

    def start(r, carry):
        for s in range(2):
            _row_copy(x_ref.at[pl.ds(r, 1)], xs_hbm.at[pl.ds(dest_ref[s * tm + r], 1)], sem).start()
        return carry

    def wait(r, carry):
        for s in range(2):
            _row_copy(x_ref.at[pl.ds(r, 1)], xs_hbm.at[pl.ds(dest_ref[s * tm + r], 1)], sem).wait()
        return carry

    lax.fori_loop(0, tm, start, 0)
    lax.fori_loop(0, tm, wait, 0)


def _moe_scatter(x, dest, zpos, n_rows, tm, te):
    n = x.shape[0]
    return pl.pallas_call(
        functools.partial(_moe_scatter_kernel, te),
        grid_spec=pltpu.PrefetchScalarGridSpec(
            num_scalar_prefetch=1,
            grid=(n // tm,),
            in_specs=[pl.BlockSpec((2 * tm,), lambda i, zp: (i,), memory_space=pltpu.SMEM),
                      pl.BlockSpec((tm, D), lambda i, zp: (i, 0))],
            out_specs=pl.BlockSpec(memory_space=pl.ANY),
            scratch_shapes=[pltpu.VMEM((te, D), F32), pltpu.SemaphoreType.DMA(())],
        ),
        out_shape=jax.ShapeDtypeStruct((n_rows, D), F32),
        compiler_params=_params(("arbitrary",)),
        name="moe_scatter",
    )(zpos, dest, x)


def _moe_ffn_kernel(n_ff, te_ref, nu_ref, xs_ref, g_ref, w1_ref, w3_ref, w2_ref, ys_ref, xn_s, acc_s):
    i, j = pl.program_id(0), pl.program_id(1)
    used = i < nu_ref[0]

    @pl.when(used & (j == 0))
    def _():
        xn_s[...] = _rms(xs_ref[...], g_ref[...]).astype(BF16)
        acc_s[...] = jnp.zeros_like(acc_s)

    @pl.when(used)
    def _():
        xn = xn_s[...]
        a = (_silu(_dot(xn, w1_ref[0])) * _dot(xn, w3_ref[0])).astype(BF16)
        acc_s[...] += _dot(a, w2_ref[0])

    @pl.when(j == n_ff - 1)
    def _():
        ys_ref[...] = jnp.where(used, acc_s[...], 0.0)


def _moe_ffn(xs, g, w1, w3, w2, tile_expert, n_used, te):
    n_tiles = tile_expert.shape[0]
    d_ff = w1.shape[2]
    n_ff = FF_SPLIT_MOE
    tf = d_ff // n_ff

    def row_map(i, j, te_r, nu_r):
        return (jnp.minimum(i, nu_r[0] - 1), 0)

    def ff(i, j, nu_r):
        return jnp.where(i < nu_r[0], j, n_ff - 1)

    return pl.pallas_call(
        functools.partial(_moe_ffn_kernel, n_ff),
        grid_spec=pltpu.PrefetchScalarGridSpec(
            num_scalar_prefetch=2,
            grid=(n_tiles, n_ff),
            in_specs=[pl.BlockSpec((te, D), row_map),
                      pl.BlockSpec((1, D), lambda i, j, te_r, nu_r: (0, 0)),
                      pl.BlockSpec((1, D, tf), lambda i, j, te_r, nu_r: (te_r[i], 0, ff(i, j, nu_r))),
                      pl.BlockSpec((1, D, tf), lambda i, j, te_r, nu_r: (te_r[i], 0, ff(i, j, nu_r))),
                      pl.BlockSpec((1, tf, D), lambda i, j, te_r, nu_r: (te_r[i], ff(i, j, nu_r), 0))],
            out_specs=pl.BlockSpec((te, D), lambda i, j, te_r, nu_r: (i, 0)),
            scratch_shapes=[pltpu.VMEM((te, D), BF16), pltpu.VMEM((te, D), F32)],
        ),
        out_shape=jax.ShapeDtypeStruct((n_tiles * te, D), F32),
        compiler_params=_params(("arbitrary", "arbitrary")),
        name="moe_ffn",
    )(tile_expert, n_used, xs, g, w1, w3, w2)


def _moe_combine_kernel(n_p, dest_ref, info_ref, x_ref, ys_hbm, op_ref, os_ref, buf, sem):
    tm = x_ref.shape[0]
    i = pl.program_id(0)

    def start(r, carry):
        for s in range(2):
            _row_copy(ys_hbm.at[pl.ds(dest_ref[s * tm + r], 1)], buf.at[s, pl.ds(r, 1)], sem).start()
        return carry

    def wait(r, carry):
        for s in range(2):
            _row_copy(ys_hbm.at[pl.ds(dest_ref[s * tm + r], 1)], buf.at[s, pl.ds(r, 1)], sem).wait()
        return carry

    lax.fori_loop(0, tm, start, 0)
    lax.fori_loop(0, tm, wait, 0)
    out = x_ref[...] + info_ref[:, 2:3] * buf[0] + info_ref[:, 3:4] * buf[1]

    @pl.when(i < n_p)
    def _():
        op_ref[...] = out

    @pl.when(i >= n_p)
    def _():
        os_ref[...] = out


def _moe_combine(x, info, dest, ys, tm, n_p, n_s):
    out_p, out_s = _pair(tm, D, n_p)
    return pl.pallas_call(
        functools.partial(_moe_combine_kernel, n_p),
        grid=(n_p + n_s,),
        in_specs=[pl.BlockSpec((2 * tm,), lambda i: (i,), memory_space=pltpu.SMEM),
                  _rows(tm, LANES), _rows(tm, D), pl.BlockSpec(memory_space=pl.ANY)],
        out_specs=[out_p, out_s],
        out_shape=[jax.ShapeDtypeStruct((n_p * tm, D), F32), jax.ShapeDtypeStruct((n_s * tm, D), F32)],
        scratch_shapes=[pltpu.VMEM((2, tm, D), F32), pltpu.SemaphoreType.DMA(())],
        compiler_params=_params(("arbitrary",)),
        name="moe_combine",
    )(dest, info, x, ys)


def _moe(x, g, w_router, w1, w3, w2, tm, n_p, n_s):
    n = x.shape[0]
    te = min(EXPERT_TILE, tm)
    wr = jnp.zeros((D, LANES), F32).at[:, :N_EXPERTS].set(w_router)
    info, cnt = _router(x, g, wr, tm)
    e1, e2 = info[:, 0].astype(jnp.int32), info[:, 1].astype(jnp.int32)
    r1, r2 = info[:, 4].astype(jnp.int32), info[:, 5].astype(jnp.int32)
    counts = cnt[0, :N_EXPERTS].astype(jnp.int32)
    padded = (counts + te - 1) // te * te
    ends = jnp.cumsum(padded)
    starts = ends - padded
    dest = jnp.stack([(starts[e1] + r1).reshape(n // tm, tm), (starts[e2] + r2).reshape(n // tm, tm)], axis=1)
    dest = dest.reshape(-1)
    n_tiles = (2 * n + N_EXPERTS * (te - 1)) // te
    tile_start = jnp.arange(n_tiles, dtype=jnp.int32) * te
    tile_expert = jnp.minimum(jnp.sum(tile_start[:, None] >= ends[None, :], axis=1), N_EXPERTS - 1).astype(jnp.int32)
    n_used = (ends[-1:] // te).astype(jnp.int32)
    last_tile = jnp.maximum(ends - te, 0).astype(jnp.int32)
    xs = _moe_scatter(x, dest, jnp.concatenate([last_tile, ends[-1:].astype(jnp.int32)]), n_tiles * te, tm, te)
    ys = _moe_ffn(xs, g, w1, w3, w2, tile_expert, n_used, te)
    return _moe_combine(x, info, dest, ys, tm, n_p, n_s)


def _t5_bucket(rel):
    n = -rel
    half = NUM_BUCKETS // 2
    ret = jnp.where(n < 0, half, 0)
    n = jnp.abs(n)
    max_exact = half // 2
    nf = jnp.maximum(n, 1).astype(F32)
    large = max_exact + (jnp.log(nf / max_exact) / math.log(MAX_DISTANCE / max_exact)
                         * (half - max_exact)).astype(jnp.int32)
    large = jnp.minimum(large, half - 1)
    return ret + jnp.where(n < max_exact, n, large)


def _bucket_table(length):
    kpos = jnp.arange(WINDOW + length) - WINDOW
    return _t5_bucket(kpos[None, :] - jnp.arange(length)[:, None]).astype(jnp.int32)


def kernel(x_prompt, x_sample, state_conv_a, state_rglru_h, state_conv_b, state_mlstm_c, state_mlstm_n, state_mlstm_m, cache_swa_k, cache_swa_v, norm_mix, norm_ffn, w_in_ab, w_conv_a, b_conv_a, w_rg_a, b_rg_a, w_rg_x, b_rg_x, rg_lambda, w_conv_b, b_conv_b, w_q_b, w_k_b, w_v_b, w_if_b, b_if_b, g_hnorm_b, skip_b, w_out_ab, w1_dense, w3_dense, w2_dense, w_in_att, g_qnorm, g_knorm, sinks, w_out_att, rel_bias, w_router, w1_moe, w3_moe, w2_moe):
    bp, tp, _ = x_prompt.shape
    bs, ts, _ = x_sample.shape
    assert norm_mix.shape[0] == 2 and w_in_ab.shape[0] == 1 and w_in_att.shape[0] == 1
    assert tp % CHUNK == 0 and ts <= CHUNK and cache_swa_k.shape[2] == WINDOW
    rows_p, rows_s = bp * tp, bs * ts
    tm = math.gcd(math.gcd(rows_p, rows_s), TOKEN_TILE)
    n_p, n_s = rows_p // tm, rows_s // tm
    xp = x_prompt.reshape(rows_p, D)
    xs = x_sample.reshape(rows_s, D)
    bf = lambda w: w.astype(BF16)
    vec = lambda v: v.reshape(1, -1)

    xa, ga, xm, z = _in_proj_ab(xp, xs, vec(norm_mix[0]), bf(w_in_ab[0]), tm)
    rg_w = (w_conv_a[0], vec(b_conv_a[0]), bf(w_rg_a[0]), vec(b_rg_a[0]), bf(w_rg_x[0]), vec(b_rg_x[0]),
            vec(rg_lambda[0]))
    ya_p, hl_p = _rglru(xa, ga, 0, tp, jnp.zeros((bp, CONV_W - 1, D), F32), jnp.zeros((bp, 1, D), F32), *rg_w)
    ya_s, hl_s = _rglru(xa, ga, rows_p, ts, state_conv_a[0], state_rglru_h[0].reshape(bs, 1, D), *rg_w)

    pre_w = (w_conv_b[0], vec(b_conv_b[0]), bf(w_q_b[0]), bf(w_k_b[0]), bf(w_v_b[0]), bf(w_if_b[0].T),
             b_if_b[0].reshape(2 * NH_B, 1))
    rec_w = (g_hnorm_b[0].reshape(1, D), vec(skip_b[0]))
    lanes = lambda m: jnp.broadcast_to(m[:, :, None], m.shape + (LANES,))
    pre_p = _mlstm_pre(xm, 0, tp, jnp.zeros((bp, CONV_W - 1, D), F32), *pre_w)
    pre_s = _mlstm_pre(xm, rows_p, ts, state_conv_b[0], *pre_w)
    yb_p, c_p, nn_p, m_p = _mlstm(*pre_p[1:], pre_p[0], z, 0, tp, jnp.zeros((bp, NH_B, DH_B, DH_B), F32),
                                  jnp.zeros((bp, NH_B, DH_B), F32), jnp.zeros((bp, NH_B, LANES), F32), *rec_w)
    yb_s, c_s, nn_s, m_s = _mlstm(*pre_s[1:], pre_s[0], z, rows_p, ts, state_mlstm_c[0], state_mlstm_n[0],
                                  lanes(state_mlstm_m[0]), *rec_w)

    w_out = bf(w_out_ab[0])
    y = _out_proj((xp, xs), [(ya_p, ya_s), (yb_p, yb_s)], [w_out[:D], w_out[D:]], tm, n_p)
    y = _ffn_dense(y, vec(norm_ffn[0]), bf(w1_dense[0]), bf(w3_dense[0]), bf(w2_dense[0]), tm)

    tile2 = lambda gain: jnp.tile(gain, D_KV // HEAD_DIM).reshape(1, D_KV)
    q, kw, vw, k, v = _in_proj_att(y, vec(norm_mix[1]), bf(w_in_att[0]), tile2(g_qnorm[0]), tile2(g_knorm[0]), tm)
    half = WINDOW // 2
    ncp = tp // CHUNK
    sinks2 = sinks[0].reshape(1, N_HEADS)
    hist_p = (lambda bi, ci: (bi * ncp + jnp.maximum(ci - 2, 0), 0), lambda bi, ci: (bi * ncp + jnp.maximum(ci - 1, 0), 0))
    o_p = _attention(q, kw, vw, kw, vw, hist_p, 0, tp, bp, True, _bucket_table(CHUNK), rel_bias, sinks2)

    def widen(cache):
        wide = jnp.broadcast_to(cache[:, :, :, None, :], (bs, WINDOW, N_KV, GROUP, HEAD_DIM))
        return wide.reshape(bs * WINDOW, D).astype(BF16)

    ck, cv = widen(cache_swa_k[0]), widen(cache_swa_v[0])
    hist_s = (lambda bi, ci: (2 * bi, 0), lambda bi, ci: (2 * bi + 1, 0))
    o_s = _attention(q, kw, vw, ck, cv, hist_s, rows_p, ts, bs, False, _bucket_table(ts), rel_bias, sinks2)
    y = _out_proj(y, [(o_p, o_s)], [bf(w_out_att[0])], tm, n_p)
    yp, ys = _moe(y, vec(norm_ffn[1]), w_router[0], bf(w1_moe[0]), bf(w3_moe[0]), bf(w2_moe[0]), tm, n_p, n_s)

    def tail(a, rows, b, t, keep):
        if rows == 0:
            return jnp.stack([a[(i + 1) * t - keep:(i + 1) * t] for i in range(b)])
        return a[rows:rows + b * t].reshape(b, t, -1)[:, t - keep:]

    kv4 = lambda a, b: a.reshape(b, -1, N_KV, HEAD_DIM)
    one = lambda a: a[None]
    k_s = jnp.concatenate([cache_swa_k[0][:, ts:], kv4(k[rows_p:], bs)], axis=1)
    v_s = jnp.concatenate([cache_swa_v[0][:, ts:], kv4(v[rows_p:], bs)], axis=1)
    return (yp.reshape(bp, tp, D), ys.reshape(bs, ts, D),
            one(tail(xa, 0, bp, tp, CONV_W - 1)), one(hl_p.reshape(bp, D)), one(tail(xm, 0, bp, tp, CONV_W - 1)),
            one(c_p), one(nn_p), one(m_p[:, :, 0]),
            one(kv4(tail(k, 0, bp, tp, WINDOW), bp)), one(kv4(tail(v, 0, bp, tp, WINDOW), bp)),
            one(tail(xa, rows_p, bs, ts, CONV_W - 1)), one(hl_s.reshape(bs, D)),
            one(tail(xm, rows_p, bs, ts, CONV_W - 1)), one(c_s), one(nn_s), one(m_s[:, :, 0]),
            one(k_s), one(v_s))
```

```python
import functools
import math

import jax
import jax.numpy as jnp
from jax import lax
from jax.experimental import pallas as pl
from jax.experimental.pallas import tpu as pltpu

F32 = jnp.float32
BF16 = jnp.bfloat16

D = 1024
CHUNK = 64
CONV_W = 4
NB_A = 8
BS_A = D // NB_A
RG_C = 8.0
NH_B = 4
DH_B = D // NH_B
N_HEADS = 16
HEAD_DIM = D // N_HEADS
N_KV = 4
GROUP = N_HEADS // N_KV
D_KV = N_KV * HEAD_DIM
WINDOW = 128
NUM_BUCKETS = 32
MAX_DISTANCE = 128
N_EXPERTS = 8
EPS = 1e-6
LANES = 128
NEG_INF = float("-inf")

TOKEN_TILE = 512
EXPERT_TILE = 512
FF_SPLIT_DENSE = 3
FF_SPLIT_MOE = 2
SEQ_BLOCK = 256
VMEM_LIMIT = 56 * 1024 * 1024


def _params(sem):
    return pltpu.CompilerParams(dimension_semantics=sem, vmem_limit_bytes=VMEM_LIMIT)


def _full(shape):
    return pl.BlockSpec(shape, lambda *_: (0,) * len(shape))


def _resident(shape):
    return pl.BlockSpec(shape, lambda *_: (0,) * len(shape), pipeline_mode=pl.Buffered(1))


def _rows(tm, c):
    return pl.BlockSpec((tm, c), lambda i: (i, 0))


def _pair(tm, c, n_p):
    return [pl.BlockSpec((tm, c), lambda i: (jnp.minimum(i, n_p - 1), 0)),
            pl.BlockSpec((tm, c), lambda i: (jnp.maximum(i - n_p, 0), 0))]


def _rms(x, g):
    ms = jnp.mean(x * x, axis=-1, keepdims=True)
    return x * lax.rsqrt(ms + EPS) * g


def _silu(x):
    return x * jax.nn.sigmoid(x)


def _sigmoid_tanh(x):
    return 0.5 * jnp.tanh(0.5 * x) + 0.5


def _softplus(x):
    return jnp.maximum(x, 0.0) + jnp.log1p(jnp.exp(-jnp.abs(x)))


def _gelu_tanh(x):
    c = math.sqrt(2.0 / math.pi)
    return x * (0.5 * (1.0 + jnp.tanh(c * (x + 0.044715 * (x * x * x)))))


def _dot(a, b):
    return jnp.dot(a, b, preferred_element_type=F32)


def _dot_nt(a, b):
    return lax.dot_general(a, b, (((1,), (1,)), ((), ())), preferred_element_type=F32)


def _dot_tn(a, b):
    return lax.dot_general(a, b, (((0,), (0,)), ((), ())), preferred_element_type=F32)


def _in_proj_ab_kernel(n_p, xp_ref, xs_ref, g_ref, w_ref, *o_refs):
    i = pl.program_id(0)
    x = jnp.where(i < n_p, xp_ref[...], xs_ref[...])
    xn = _rms(x, g_ref[...]).astype(BF16)
    for c, o_ref in enumerate(o_refs):
        o_ref[...] = _dot(xn, w_ref[:, c * D:(c + 1) * D])


def _in_proj_ab(xp, xs, g, w, tm):
    n_p, n_s = xp.shape[0] // tm, xs.shape[0] // tm
    n = xp.shape[0] + xs.shape[0]
    n_out = w.shape[1] // D
    return pl.pallas_call(
        functools.partial(_in_proj_ab_kernel, n_p),
        grid=(n_p + n_s,),
        in_specs=_pair(tm, D, n_p) + [_full((1, D)), _resident(w.shape)],
        out_specs=[_rows(tm, D)] * n_out,
        out_shape=[jax.ShapeDtypeStruct((n, D), F32)] * n_out,
        compiler_params=_params(("parallel",)),
        name="in_proj_ab",
    )(xp, xs, g, w)


def _out_proj_kernel(n_p, n_res, n_a, *refs):
    i = pl.program_id(0)
    res_refs = refs[:n_res]
    a_refs = refs[n_res:n_res + 2 * n_a]
    w_refs = refs[n_res + 2 * n_a:n_res + 3 * n_a]
    o_ref = refs[-1]
    if n_res == 2:
        acc = jnp.where(i < n_p, res_refs[0][...], res_refs[1][...])
    else:
        acc = res_refs[0][...]
    for k in range(n_a):
        a = jnp.where(i < n_p, a_refs[2 * k][...], a_refs[2 * k + 1][...])
        acc = acc + _dot(a, w_refs[k][...])
    o_ref[...] = acc


def _out_proj(res, a_pairs, ws, tm, n_p):
    res = res if isinstance(res, tuple) else (res,)
    n = sum(a.shape[0] for a in a_pairs[0])
    res_specs = _pair(tm, D, n_p) if len(res) == 2 else [_rows(tm, D)]
    a_specs = []
    for a in a_pairs:
        a_specs += _pair(tm, a[0].shape[1], n_p)
    return pl.pallas_call(
        functools.partial(_out_proj_kernel, n_p, len(res), len(a_pairs)),
        grid=(n // tm,),
        in_specs=res_specs + a_specs + [_resident(w.shape) for w in ws],
        out_specs=_rows(tm, D),
        out_shape=jax.ShapeDtypeStruct((n, D), F32),
        compiler_params=_params(("parallel",)),
        name="out_proj",
    )(*res, *[x for a in a_pairs for x in a], *ws)


def _ffn_dense_kernel(x_ref, g_ref, w1_ref, w3_ref, w2_ref, o_ref):
    x = x_ref[...]
    xn = _rms(x, g_ref[...]).astype(BF16)
    step = w1_ref.shape[1] // FF_SPLIT_DENSE
    acc = x
    for c in range(FF_SPLIT_DENSE):
        sl = slice(c * step, (c + 1) * step)
        a = (_silu(_dot(xn, w1_ref[:, sl])) * _dot(xn, w3_ref[:, sl])).astype(BF16)
        acc = acc + _dot(a, w2_ref[sl, :])
    o_ref[...] = acc


def _ffn_dense(x, g, w1, w3, w2, tm):
    n = x.shape[0]
    return pl.pallas_call(
        _ffn_dense_kernel,
        grid=(n // tm,),
        in_specs=[_rows(tm, D), _full((1, D)), _resident(w1.shape), _resident(w3.shape), _resident(w2.shape)],
        out_specs=_rows(tm, D),
        out_shape=jax.ShapeDtypeStruct((n, D), F32),
        compiler_params=_params(("parallel",)),
        name="ffn_dense",
    )(x, g, w1, w3, w2)


def _in_proj_att_kernel(x_ref, g_ref, w_ref, gq_ref, gk_ref, q_ref, kw_ref, vw_ref, k_ref, v_ref):
    xn = _rms(x_ref[...], g_ref[...]).astype(BF16)
    cw = D_KV
    r = lax.broadcasted_iota(jnp.int32, (cw, cw), 0) // HEAD_DIM
    c = lax.broadcasted_iota(jnp.int32, (cw, cw), 1) // HEAD_DIM
    group_ones = (r == c).astype(BF16)
    er = lax.broadcasted_iota(jnp.int32, (D_KV, D), 0)
    ec = lax.broadcasted_iota(jnp.int32, (D_KV, D), 1)
    widen = ((er // HEAD_DIM == ec // (GROUP * HEAD_DIM)) & (er % HEAD_DIM == ec % HEAD_DIM)).astype(BF16)

    def head_norm(y, gain):
        ms = _dot((y * y).astype(BF16), group_ones) * (1.0 / HEAD_DIM)
        return y * lax.rsqrt(ms + EPS) * gain

    for b in range(D // cw):
        y = _dot(xn, w_ref[:, b * cw:(b + 1) * cw])
        q_ref[:, b * cw:(b + 1) * cw] = (head_norm(y, gq_ref[...]) * (HEAD_DIM ** -0.5)).astype(BF16)
    k = head_norm(_dot(xn, w_ref[:, D:D + D_KV]), gk_ref[...])
    v = _dot(xn, w_ref[:, D + D_KV:D + 2 * D_KV])
    k_ref[...] = k
    v_ref[...] = v
    kw_ref[...] = _dot(k.astype(BF16), widen).astype(BF16)
    vw_ref[...] = _dot(v.astype(BF16), widen).astype(BF16)


def _in_proj_att(x, g, w, gq, gk, tm):
    n = x.shape[0]
    return pl.pallas_call(
        _in_proj_att_kernel,
        grid=(n // tm,),
        in_specs=[_rows(tm, D), _full((1, D)), _resident(w.shape), _full((1, D_KV)), _full((1, D_KV))],
        out_specs=[_rows(tm, D)] * 3 + [_rows(tm, D_KV)] * 2,
        out_shape=[jax.ShapeDtypeStruct((n, D), BF16)] * 3 + [jax.ShapeDtypeStruct((n, D_KV), F32)] * 2,
        compiler_params=_params(("parallel",)),
        name="in_proj_att",
    )(x, g, w, gq, gk)


def _causal_conv(x, xbuf, conv0_ref, wc_ref, bc_ref, first):
    tb = x.shape[0]

    @pl.when(first)
    def _():
        xbuf[5:8, :] = conv0_ref[0]

    xbuf[8:8 + tb, :] = x
    y = bc_ref[...] + xbuf[5:5 + tb, :] * wc_ref[0:1, :]
    for j in range(1, CONV_W):
        y = y + xbuf[5 + j:5 + j + tb, :] * wc_ref[j:j + 1, :]
    xbuf[5:8, :] = xbuf[5 + tb:8 + tb, :]
    return y


def _seq_specs(bsz, seq, tb, row0):
    nc = seq // tb
    off = row0 // tb
    return nc, (lambda bi, ci: (off + bi * nc + ci, 0)), (lambda bi, ci: (bi * nc + ci, 0))


def _scan_by_doubling(a, u, pos, axis, length):
    d = 1
    while d < length:
        keep = pos >= d
        u = jnp.where(keep, a * pltpu.roll(u, d, axis) + u, u)
        a = jnp.where(keep, a * pltpu.roll(a, d, axis), a)
        d *= 2
    return a, u


def _linear_scan(a, u, h0, row):
    a, u = _scan_by_doubling(a, u, row, 0, a.shape[0])
    return a * h0 + u


def _rglru_kernel(xa_ref, ga_ref, conv0_ref, h0_ref, wc_ref, bc_ref, wa_ref, ba_ref, wx_ref, bx_ref, lam_ref,
                  ya_ref, hl_ref, xbuf):
    first = pl.program_id(1) == 0
    tb = xa_ref.shape[0]

    @pl.when(first)
    def _():
        hl_ref[0] = h0_ref[0]

    y = _causal_conv(xa_ref[...], xbuf, conv0_ref, wc_ref, bc_ref, first)
    row = lax.broadcasted_iota(jnp.int32, (tb, BS_A), 0)
    for n in range(NB_A):
        sl = slice(n * BS_A, (n + 1) * BS_A)
        yn = y[:, sl]
        yb = yn.astype(BF16)
        r = _sigmoid_tanh(_dot(yb, wa_ref[n]) + ba_ref[:, sl])
        ig = _sigmoid_tanh(_dot(yb, wx_ref[n]) + bx_ref[:, sl])
        log_a = (-RG_C) * r * _softplus(-lam_ref[:, sl])
        a = jnp.exp(log_a)
        th = jnp.tanh(log_a)
        u = yn * ig * jnp.sqrt(-2.0 * th / (1.0 - th))
        h = _linear_scan(a, u, hl_ref[0, :, sl], row)
        hl_ref[0, :, sl] = h[tb - 1:tb, :]
        ya_ref[:, sl] = (h * _gelu_tanh(ga_ref[:, sl])).astype(BF16)


def _rglru(xa, ga, row0, seq, conv0, h0, wc, bc, wa, ba, wx, bx, lam):
    bsz = conv0.shape[0]
    tb = min(seq, SEQ_BLOCK)
    nc, in_map, out_map = _seq_specs(bsz, seq, tb, row0)
    state3 = lambda bi, ci: (bi, 0, 0)
    return pl.pallas_call(
        _rglru_kernel,
        grid=(bsz, nc),
        in_specs=[pl.BlockSpec((tb, D), in_map), pl.BlockSpec((tb, D), in_map),
                  pl.BlockSpec((1, CONV_W - 1, D), state3), pl.BlockSpec((1, 1, D), state3),
                  _full(wc.shape), _full(bc.shape), _full(wa.shape), _full(ba.shape), _full(wx.shape),
                  _full(bx.shape), _full(lam.shape)],
        out_specs=[pl.BlockSpec((tb, D), out_map), pl.BlockSpec((1, 1, D), state3)],
        out_shape=[jax.ShapeDtypeStruct((bsz * seq, D), BF16), jax.ShapeDtypeStruct((bsz, 1, D), F32)],
        scratch_shapes=[pltpu.VMEM((tb + 8, D), F32)],
        compiler_params=_params(("parallel", "arbitrary")),
        name="rglru",
    )(xa, ga, conv0, h0, wc, bc, wa, ba, wx, bx, lam)


def _mlstm_pre_kernel(chunk, xm_ref, conv0_ref, m0_ref, wc_ref, bc_ref, wq_ref, wk_ref, wv_ref, wif_ref, bif_ref,
                      xc_ref, q_ref, k_ref, v_ref, w_ref, col_ref, m_ref, xbuf):
    first = pl.program_id(1) == 0
    tb = xm_ref.shape[0]
    L = chunk

    @pl.when(first)
    def _():
        m_ref[...] = m0_ref[...]

    x = xm_ref[...]
    xc = _silu(_causal_conv(x, xbuf, conv0_ref, wc_ref, bc_ref, first))
    xc_ref[...] = xc
    gates = bif_ref[...]
    for h in range(NH_B):
        sl = slice(h * DH_B, (h + 1) * DH_B)
        xch = xc[:, sl].astype(BF16)
        q = _dot(xch, wq_ref[h]).astype(BF16)
        k = _dot(xch, wk_ref[h]).astype(BF16)
        v = _dot(x[:, sl].astype(BF16), wv_ref[h]).astype(BF16)
        q_ref[:, sl] = q
        k_ref[:, sl] = k
        v_ref[:, sl] = v
        gates = (gates + _dot_nt(wif_ref[:, sl], q) + _dot_nt(wif_ref[:, D + h * DH_B:D + (h + 1) * DH_B], k)
                 + _dot_nt(wif_ref[:, 2 * D + h * DH_B:2 * D + (h + 1) * DH_B], v))
    row = lax.broadcasted_iota(jnp.int32, gates.shape, 0)
    gates = jnp.where(row < NH_B, gates, -_softplus(-gates))

    tt = lax.broadcasted_iota(jnp.int32, (L, L), 0)
    ss = lax.broadcasted_iota(jnp.int32, (L, L), 1)
    eye = tt == ss
    causal = ss <= tt
    lane = lax.broadcasted_iota(jnp.int32, (L, LANES), 1)

    def to_col(row_vec):
        return jnp.sum(jnp.where(eye, row_vec, 0.0), axis=1, keepdims=True)

    m_prev = [m_ref[0, h:h + 1, 0:1] for h in range(NH_B)]
    for j in range(tb // L):
        table = jnp.zeros((L, LANES), F32)
        for h in range(NH_B):
            i_row = gates[h:h + 1, j * L:(j + 1) * L]
            f_row = gates[NH_B + h:NH_B + h + 1, j * L:(j + 1) * L]
            f_col = to_col(f_row)
            b_col = jnp.sum(jnp.where(causal, f_row, 0.0), axis=1, keepdims=True)
            b_row = jnp.sum(jnp.where(tt <= ss, f_col, 0.0), axis=0, keepdims=True)
            w_row = i_row - b_row
            w_col = to_col(i_row) - b_col
            big_m = jnp.maximum(m_prev[h], jnp.max(jnp.where(causal, w_row, NEG_INF), axis=1, keepdims=True))
            m_last = big_m[L - 1:L, :]
            w_ref[j, h:h + 1, :] = w_row
            for c, col in enumerate((big_m, jnp.exp(m_prev[h] - big_m), jnp.exp(-(b_col + big_m)),
                                     jnp.exp(w_col - m_last))):
                table = jnp.where(lane == c * NH_B + h, col, table)
            m_prev[h] = b_col[L - 1:L, :] + m_last
        col_ref[j * L:(j + 1) * L, :] = table
    for h in range(NH_B):
        m_ref[0, h:h + 1, :] = jnp.broadcast_to(m_prev[h], (1, LANES))


def _mlstm_pre(xm, row0, seq, conv0, m0, wc, bc, wq, wk, wv, wif_t, bif):
    bsz = conv0.shape[0]
    tb = min(seq, SEQ_BLOCK)
    chunk = tb
    nc, in_map, out_map = _seq_specs(bsz, seq, tb, row0)
    n = bsz * seq
    st3 = lambda bi, ci: (bi, 0, 0)
    return pl.pallas_call(
        functools.partial(_mlstm_pre_kernel, chunk),
        grid=(bsz, nc),
        in_specs=[pl.BlockSpec((tb, D), in_map), pl.BlockSpec((1, CONV_W - 1, D), st3),
                  pl.BlockSpec((1, NH_B, LANES), st3),
                  _full(wc.shape), _full(bc.shape), _full(wq.shape), _full(wk.shape), _full(wv.shape),
                  _full(wif_t.shape), _full(bif.shape)],
        out_specs=[pl.BlockSpec((tb, D), out_map)] * 4
                  + [pl.BlockSpec((tb // chunk, NH_B, chunk), lambda bi, ci: (bi * nc + ci, 0, 0)),
                     pl.BlockSpec((tb, LANES), out_map), pl.BlockSpec((1, NH_B, LANES), st3)],
        out_shape=[jax.ShapeDtypeStruct((n, D), F32)] + [jax.ShapeDtypeStruct((n, D), BF16)] * 3
                  + [jax.ShapeDtypeStruct((n // chunk, NH_B, chunk), F32), jax.ShapeDtypeStruct((n, LANES), F32),
                     jax.ShapeDtypeStruct(m0.shape, F32)],
        scratch_shapes=[pltpu.VMEM((tb + 8, D), F32)],
        compiler_params=_params(("parallel", "arbitrary")),
        name="mlstm_pre",
    )(xm, conv0, m0, wc, bc, wq, wk, wv, wif_t, bif)


def _mlstm_kernel(chunk, q_ref, k_ref, v_ref, w_ref, col_ref, xc_ref, z_ref, c0_ref, n0_ref, gh_ref, skip_ref,
                  yb_ref, c_ref, n_ref):
    tb = q_ref.shape[0]
    L = chunk
    scale = DH_B ** -0.5

    @pl.when(pl.program_id(1) == 0)
    def _():
        c_ref[...] = c0_ref[...]
        n_ref[...] = n0_ref[...]

    causal = lax.broadcasted_iota(jnp.int32, (L, L), 1) <= lax.broadcasted_iota(jnp.int32, (L, L), 0)
    heads = range(NH_B)
    hsl = [slice(h * DH_B, (h + 1) * DH_B) for h in heads]
    for j in range(tb // L):
        rows = slice(j * L, (j + 1) * L)
        cols = col_ref[rows, :]
        col = lambda c, h: cols[:, c * NH_B + h:c * NH_B + h + 1]
        q = [q_ref[rows, hsl[h]] for h in heads]
        k = [k_ref[rows, hsl[h]] for h in heads]
        v = [v_ref[rows, hsl[h]] for h in heads]
        s = [_dot_nt(q[h], k[h]) for h in heads]
        qc = [_dot(q[h], c_ref[0, h].astype(BF16)) for h in heads]
        qn = [_dot_nt(q[h], jnp.broadcast_to(n_ref[0, h:h + 1, :], (8, DH_B)).astype(BF16))[:, 0:1] for h in heads]
        p = [(s[h] * scale * jnp.where(causal, jnp.exp(w_ref[j, h:h + 1, :] - col(0, h)), 0.0)).astype(BF16)
             for h in heads]
        pv = [_dot(p[h], v[h]) for h in heads]
        psum = [_dot(p[h], jnp.ones((L, LANES), BF16))[:, 0:1] for h in heads]
        for h in heads:
            sc = col(1, h)
            keep = sc[L - 1:L, :]
            kd = k[h].astype(F32) * (col(3, h) * scale)
            c_ref[0, h] = keep * c_ref[0, h] + _dot_tn(kd.astype(BF16), v[h])
            n_ref[0, h:h + 1, :] = keep * n_ref[0, h:h + 1, :] + jnp.sum(kd, axis=0, keepdims=True)
            num = pv[h] + sc * qc[h]
            den = psum[h] + sc * qn[h]
            hs = num / jnp.maximum(jnp.abs(den), col(2, h))
            mu = jnp.mean(hs, axis=1, keepdims=True)
            dev = hs - mu
            var = jnp.mean(dev * dev, axis=1, keepdims=True)
            hn = dev * lax.rsqrt(var + EPS) * gh_ref[:, hsl[h]]
            out = (hn + skip_ref[:, hsl[h]] * xc_ref[rows, hsl[h]]) * _silu(z_ref[rows, hsl[h]])
            yb_ref[rows, hsl[h]] = out.astype(BF16)


def _mlstm(q, k, v, w, col, xc, z, row0, seq, c0, n0, gh, skip):
    bsz = c0.shape[0]
    tb = min(seq, SEQ_BLOCK)
    chunk = tb
    nc, z_map, own_map = _seq_specs(bsz, seq, tb, row0)
    st4 = lambda bi, ci: (bi, 0, 0, 0)
    st3 = lambda bi, ci: (bi, 0, 0)
    return pl.pallas_call(
        functools.partial(_mlstm_kernel, chunk),
        grid=(bsz, nc),
        in_specs=[pl.BlockSpec((tb, D), own_map)] * 3
                 + [pl.BlockSpec((tb // chunk, NH_B, chunk), lambda bi, ci: (bi * nc + ci, 0, 0)),
                    pl.BlockSpec((tb, LANES), own_map),
                    pl.BlockSpec((tb, D), own_map), pl.BlockSpec((tb, D), z_map),
                    pl.BlockSpec((1, NH_B, DH_B, DH_B), st4), pl.BlockSpec((1, NH_B, DH_B), st3),
                    _full(gh.shape), _full(skip.shape)],
        out_specs=[pl.BlockSpec((tb, D), own_map), pl.BlockSpec((1, NH_B, DH_B, DH_B), st4),
                   pl.BlockSpec((1, NH_B, DH_B), st3)],
        out_shape=[jax.ShapeDtypeStruct((bsz * seq, D), BF16), jax.ShapeDtypeStruct(c0.shape, F32),
                   jax.ShapeDtypeStruct(n0.shape, F32)],
        compiler_params=_params(("parallel", "arbitrary")),
        name="mlstm",
    )(q, k, v, w, col, xc, z, c0, n0, gh, skip)


def _attn_kernel(masked, q_ref, kh0_ref, kh1_ref, ko_ref, vh0_ref, vh1_ref, vo_ref, bucket_ref, relb_ref, sink_ref,
                 o_ref, bias_s, s_scr, p_scr):
    L = q_ref.shape[0]
    nk = WINDOW + L
    ci = pl.program_id(1)

    @pl.when((pl.program_id(0) == 0) & (ci == 0))
    def _():
        bucket = bucket_ref[...]
        for h in range(N_HEADS):
            acc = jnp.zeros((L, nk), F32)
            for b in range(NUM_BUCKETS):
                acc = jnp.where(bucket == b, relb_ref[b, h], acc)
            bias_s[h * L:(h + 1) * L, :] = acc

    kcat = jnp.concatenate([kh0_ref[...], kh1_ref[...], ko_ref[...]], axis=0)
    vcat = jnp.concatenate([vh0_ref[...], vh1_ref[...], vo_ref[...]], axis=0)
    gw = GROUP * HEAD_DIM
    slot = lax.broadcasted_iota(jnp.int32, (L, gw), 1) // HEAD_DIM
    if masked:
        kk = lax.broadcasted_iota(jnp.int32, (1, nk), 1)
        half = WINDOW // 2
        valid = (kk >= WINDOW) | ((kk >= half) & (ci >= 1)) | (ci >= 2)
    gl = GROUP * L
    for g in range(N_KV):
        gsl = slice(g * gw, (g + 1) * gw)
        qg = q_ref[:, gsl]
        zero = jnp.zeros_like(qg)
        qs = jnp.concatenate([jnp.where(slot == j, qg, zero) for j in range(GROUP)], axis=0)
        s_scr[g * gl:(g + 1) * gl, :] = _dot_nt(qs, kcat[:, gsl])
    for h in range(N_HEADS):
        rows = slice(h * L, (h + 1) * L)
        s = s_scr[rows, :] + bias_s[rows, :]
        if masked:
            s = jnp.where(valid, s, NEG_INF)
        sink = sink_ref[0, h]
        mx = jnp.maximum(jnp.max(s, axis=1, keepdims=True), sink)
        e = jnp.exp(s - mx)
        den = jnp.sum(e, axis=1, keepdims=True) + jnp.exp(sink - mx)
        p_scr[rows, :] = (e / den).astype(BF16)
    for g in range(N_KV):
        gsl = slice(g * gw, (g + 1) * gw)
        ow = _dot(p_scr[g * gl:(g + 1) * gl, :], vcat[:, gsl])
        og = jnp.where(slot == 0, ow[0:L, :], 0.0)
        for j in range(1, GROUP):
            og = jnp.where(slot == j, ow[j * L:(j + 1) * L, :], og)
        o_ref[:, gsl] = og.astype(BF16)


def _attention(q, k_own, v_own, k_hist, v_hist, hist_maps, row0, seq, bsz, masked, bucket, rel_bias, sinks):
    L = min(seq, CHUNK)
    nc, in_map, out_map = _seq_specs(bsz, seq, L, row0)
    half = WINDOW // 2
    h0_map, h1_map = hist_maps
    smem = functools.partial(pl.BlockSpec, memory_space=pltpu.SMEM)
    return pl.pallas_call(
        functools.partial(_attn_kernel, masked),
        grid=(bsz, nc),
        in_specs=[pl.BlockSpec((L, D), in_map),
                  pl.BlockSpec((half, D), h0_map), pl.BlockSpec((half, D), h1_map), pl.BlockSpec((L, D), in_map),
                  pl.BlockSpec((half, D), h0_map), pl.BlockSpec((half, D), h1_map), pl.BlockSpec((L, D), in_map),
                  _full(bucket.shape), smem(), smem()],
        out_specs=pl.BlockSpec((L, D), out_map),
        out_shape=jax.ShapeDtypeStruct((bsz * seq, D), BF16),
        scratch_shapes=[pltpu.VMEM((N_HEADS * L, WINDOW + L), F32)] * 2 + [pltpu.VMEM((N_HEADS * L, WINDOW + L), BF16)],
        compiler_params=_params(("arbitrary", "arbitrary")),
        name="swa",
    )(q, k_hist, k_hist, k_own, v_hist, v_hist, v_own, bucket, rel_bias, sinks)


def _router_kernel(x_ref, g_ref, wr_ref, info_ref, cnt_ref):
    tm = x_ref.shape[0]

    @pl.when(pl.program_id(0) == 0)
    def _():
        cnt_ref[...] = jnp.zeros_like(cnt_ref)

    xn = _rms(x_ref[...], g_ref[...])
    hi = xn.astype(BF16)
    lo = (xn - hi.astype(F32)).astype(BF16)
    w = wr_ref[...]
    whi = w.astype(BF16)
    wlo = (w - whi.astype(F32)).astype(BF16)
    logits = _dot(hi, whi) + _dot(hi, wlo) + _dot(lo, whi)
    lane = lax.broadcasted_iota(jnp.int32, (tm, LANES), 1)
    logits = jnp.where(lane < N_EXPERTS, logits, NEG_INF)
    m1 = jnp.max(logits, axis=1, keepdims=True)
    i1 = jnp.min(jnp.where(logits == m1, lane, LANES), axis=1, keepdims=True)
    rest = jnp.where(lane == i1, NEG_INF, logits)
    m2 = jnp.max(rest, axis=1, keepdims=True)
    i2 = jnp.min(jnp.where(rest == m2, lane, LANES), axis=1, keepdims=True)
    e2 = jnp.exp(m2 - m1)
    g1 = 1.0 / (1.0 + e2)
    g2 = e2 / (1.0 + e2)
    sel = (lane == i1) | (lane == i2)
    tri = (lax.broadcasted_iota(jnp.int32, (tm, tm), 0) > lax.broadcasted_iota(jnp.int32, (tm, tm), 1)).astype(BF16)
    rank = cnt_ref[...] + _dot(tri, sel.astype(BF16))
    r1 = jnp.sum(jnp.where(lane == i1, rank, 0.0), axis=1, keepdims=True)
    r2 = jnp.sum(jnp.where(lane == i2, rank, 0.0), axis=1, keepdims=True)
    cnt_ref[...] = cnt_ref[...] + jnp.sum(sel.astype(F32), axis=0, keepdims=True)
    info = jnp.where(lane == 0, i1.astype(F32), 0.0)
    info = jnp.where(lane == 1, i2.astype(F32), info)
    info = jnp.where(lane == 2, g1, info)
    info = jnp.where(lane == 3, g2, info)
    info = jnp.where(lane == 4, r1, info)
    info = jnp.where(lane == 5, r2, info)
    info_ref[...] = info


def _router(x, g, wr, tm):
    n = x.shape[0]
    return pl.pallas_call(
        _router_kernel,
        grid=(n // tm,),
        in_specs=[_rows(tm, D), _full((1, D)), _full(wr.shape)],
        out_specs=[_rows(tm, LANES), _full((1, LANES))],
        out_shape=[jax.ShapeDtypeStruct((n, LANES), F32), jax.ShapeDtypeStruct((1, LANES), F32)],
        compiler_params=_params(("arbitrary",)),
        name="moe_router",
    )(x, g, wr)


def _row_copy(src, dst, sem):
    return pltpu.make_async_copy(src, dst, sem)


def _moe_scatter_kernel(te, zpos_ref, dest_ref, x_ref, xs_hbm, zero_s, sem):
    tm = x_ref.shape[0]

    @pl.when(pl.program_id(0) == 0)
    def _():
        zero_s[...] = jnp.zeros_like(zero_s)
        for e in range(N_EXPERTS):
            _row_copy(zero_s, xs_hbm.at[pl.ds(pl.multiple_of(zpos_ref[e], te), te)], sem).start()
        for e in range(N_EXPERTS):
            _row_copy(zero_s, xs_hbm.at[pl.ds(pl.multiple_of(zpos_ref[e], te), te)], sem).wait()

        def clear_tail(t, carry):
            tail = _row_copy(zero_s, xs_hbm.at[pl.ds(pl.multiple_of(t * te, te), te)], sem)
            tail.start()
            tail.wait()
            return carry

        lax.fori_loop(zpos_ref[N_EXPERTS] // te, xs_hbm.shape[0] // te, clear_tail, 0)

    def start(r, carry):
        for s in range(2):
            _row_copy(x_ref.at[pl.ds(r, 1)], xs_hbm.at[pl.ds(dest_ref[s * tm + r], 1)], sem).start(priority=s)
        return carry

    def wait(r, carry):
        for s in range(2):
            _row_copy(x_ref.at[pl.ds(r, 1)], xs_hbm.at[pl.ds(dest_ref[s * tm + r], 1)], sem).wait()
        return carry

    lax.fori_loop(0, tm, start, 0)
    lax.fori_loop(0, tm, wait, 0)


def _moe_scatter(x, dest, zpos, n_rows, tm, te):
    n = x.shape[0]
    return pl.pallas_call(
        functools.partial(_moe_scatter_kernel, te),
        grid_spec=pltpu.PrefetchScalarGridSpec(
            num_scalar_prefetch=1,
            grid=(n // tm,),
            in_specs=[pl.BlockSpec((2 * tm,), lambda i, zp: (i,), memory_space=pltpu.SMEM),
                      pl.BlockSpec((tm, D), lambda i, zp: (i, 0))],
            out_specs=pl.BlockSpec(memory_space=pl.ANY),
            scratch_shapes=[pltpu.VMEM((te, D), F32), pltpu.SemaphoreType.DMA(())],
        ),
        out_shape=jax.ShapeDtypeStruct((n_rows, D), F32),
        compiler_params=_params(("arbitrary",)),
        name="moe_scatter",
    )(zpos, dest, x)


def _moe_ffn_kernel(n_ff, te_ref, nu_ref, xs_ref, g_ref, w1_ref, w3_ref, w2_ref, ys_ref, xn_s, acc_s):
    i, j = pl.program_id(0), pl.program_id(1)
    used = i < nu_ref[0]

    @pl.when(used & (j == 0))
    def _():
        xn_s[...] = _rms(xs_ref[...], g_ref[...]).astype(BF16)
        acc_s[...] = jnp.zeros_like(acc_s)

    @pl.when(used)
    def _():
        xn = xn_s[...]
        a = (_silu(_dot(xn, w1_ref[0])) * _dot(xn, w3_ref[0])).astype(BF16)
        acc_s[...] += _dot(a, w2_ref[0])

    @pl.when(j == n_ff - 1)
    def _():
        ys_ref[...] = jnp.where(used, acc_s[...], 0.0)


def _moe_ffn(xs, g, w1, w3, w2, tile_expert, n_used, te):
    n_tiles = tile_expert.shape[0]
    d_ff = w1.shape[2]
    n_ff = FF_SPLIT_MOE
    tf = d_ff // n_ff

    def row_map(i, j, te_r, nu_r):
        return (jnp.minimum(i, nu_r[0] - 1), 0)

    def ff(i, j, nu_r):
        return jnp.where(i < nu_r[0], j, n_ff - 1)

    return pl.pallas_call(
        functools.partial(_moe_ffn_kernel, n_ff),
        grid_spec=pltpu.PrefetchScalarGridSpec(
            num_scalar_prefetch=2,
            grid=(n_tiles, n_ff),
            in_specs=[pl.BlockSpec((te, D), row_map),
                      pl.BlockSpec((1, D), lambda i, j, te_r, nu_r: (0, 0)),
                      pl.BlockSpec((1, D, tf), lambda i, j, te_r, nu_r: (te_r[i], 0, ff(i, j, nu_r))),
                      pl.BlockSpec((1, D, tf), lambda i, j, te_r, nu_r: (te_r[i], 0, ff(i, j, nu_r))),
                      pl.BlockSpec((1, tf, D), lambda i, j, te_r, nu_r: (te_r[i], ff(i, j, nu_r), 0))],
            out_specs=pl.BlockSpec((te, D), lambda i, j, te_r, nu_r: (i, 0)),
            scratch_shapes=[pltpu.VMEM((te, D), BF16), pltpu.VMEM((te, D), F32)],
        ),
        out_shape=jax.ShapeDtypeStruct((n_tiles * te, D), F32),
        compiler_params=_params(("arbitrary", "arbitrary")),
        name="moe_ffn",
    )(tile_expert, n_used, xs, g, w1, w3, w2)


def _moe_combine_kernel(n_p, dest_ref, info_ref, x_ref, ys_hbm, op_ref, os_ref, buf, sem):
    tm = x_ref.shape[0]
    i = pl.program_id(0)

    def start(r, carry):
        for s in range(2):
            _row_copy(ys_hbm.at[pl.ds(dest_ref[s * tm + r], 1)], buf.at[s, pl.ds(r, 1)], sem).start(priority=s)
        return carry

    def wait(r, carry):
        for s in range(2):
            _row_copy(ys_hbm.at[pl.ds(dest_ref[s * tm + r], 1)], buf.at[s, pl.ds(r, 1)], sem).wait()
        return carry

    lax.fori_loop(0, tm, start, 0)
    lax.fori_loop(0, tm, wait, 0)
    out = x_ref[...] + info_ref[:, 2:3] * buf[0] + info_ref[:, 3:4] * buf[1]

    @pl.when(i < n_p)
    def _():
        op_ref[...] = out

    @pl.when(i >= n_p)
    def _():
        os_ref[...] = out


def _moe_combine(x, info, dest, ys, tm, n_p, n_s):
    out_p, out_s = _pair(tm, D, n_p)
    return pl.pallas_call(
        functools.partial(_moe_combine_kernel, n_p),
        grid=(n_p + n_s,),
        in_specs=[pl.BlockSpec((2 * tm,), lambda i: (i,), memory_space=pltpu.SMEM),
                  _rows(tm, LANES), _rows(tm, D), pl.BlockSpec(memory_space=pl.ANY)],
        out_specs=[out_p, out_s],
        out_shape=[jax.ShapeDtypeStruct((n_p * tm, D), F32), jax.ShapeDtypeStruct((n_s * tm, D), F32)],
        scratch_shapes=[pltpu.VMEM((2, tm, D), F32), pltpu.SemaphoreType.DMA(())],
        compiler_params=_params(("arbitrary",)),
        name="moe_combine",
    )(dest, info, x, ys)


def _moe(x, g, w_router, w1, w3, w2, tm, n_p, n_s):
    n = x.shape[0]
    te = min(EXPERT_TILE, tm)
    wr = jnp.zeros((D, LANES), F32).at[:, :N_EXPERTS].set(w_router)
    info, cnt = _router(x, g, wr, tm)
    e1, e2 = info[:, 0].astype(jnp.int32), info[:, 1].astype(jnp.int32)
    r1, r2 = info[:, 4].astype(jnp.int32), info[:, 5].astype(jnp.int32)
    counts = cnt[0, :N_EXPERTS].astype(jnp.int32)
    padded = (counts + te - 1) // te * te
    ends = jnp.cumsum(padded)
    starts = ends - padded
    dest = jnp.stack([(starts[e1] + r1).reshape(n // tm, tm), (starts[e2] + r2).reshape(n // tm, tm)], axis=1)
    dest = dest.reshape(-1)
    n_tiles = (2 * n + N_EXPERTS * (te - 1)) // te
    tile_start = jnp.arange(n_tiles, dtype=jnp.int32) * te
    tile_expert = jnp.minimum(jnp.sum(tile_start[:, None] >= ends[None, :], axis=1), N_EXPERTS - 1).astype(jnp.int32)
    n_used = (ends[-1:] // te).astype(jnp.int32)
    last_tile = jnp.maximum(ends - te, 0).astype(jnp.int32)
    xs = _moe_scatter(x, dest, jnp.concatenate([last_tile, ends[-1:].astype(jnp.int32)]), n_tiles * te, tm, te)
    ys = _moe_ffn(xs, g, w1, w3, w2, tile_expert, n_used, te)
    return _moe_combine(x, info, dest, ys, tm, n_p, n_s)


def _t5_bucket(rel):
    n = -rel
    half = NUM_BUCKETS // 2
    ret = jnp.where(n < 0, half, 0)
    n = jnp.abs(n)
    max_exact = half // 2
    nf = jnp.maximum(n, 1).astype(F32)
    large = max_exact + (jnp.log(nf / max_exact) / math.log(MAX_DISTANCE / max_exact)
                         * (half - max_exact)).astype(jnp.int32)
    large = jnp.minimum(large, half - 1)
    return ret + jnp.where(n < max_exact, n, large)


def _bucket_table(length):
    kpos = jnp.arange(WINDOW + length) - WINDOW
    return _t5_bucket(kpos[None, :] - jnp.arange(length)[:, None]).astype(jnp.int32)


def kernel(x_prompt, x_sample, state_conv_a, state_rglru_h, state_conv_b, state_mlstm_c, state_mlstm_n, state_mlstm_m, cache_swa_k, cache_swa_v, norm_mix, norm_ffn, w_in_ab, w_conv_a, b_conv_a, w_rg_a, b_rg_a, w_rg_x, b_rg_x, rg_lambda, w_conv_b, b_conv_b, w_q_b, w_k_b, w_v_b, w_if_b, b_if_b, g_hnorm_b, skip_b, w_out_ab, w1_dense, w3_dense, w2_dense, w_in_att, g_qnorm, g_knorm, sinks, w_out_att, rel_bias, w_router, w1_moe, w3_moe, w2_moe):
    bp, tp, _ = x_prompt.shape
    bs, ts, _ = x_sample.shape
    assert norm_mix.shape[0] == 2 and w_in_ab.shape[0] == 1 and w_in_att.shape[0] == 1
    assert tp % CHUNK == 0 and ts <= CHUNK and cache_swa_k.shape[2] == WINDOW
    rows_p, rows_s = bp * tp, bs * ts
    tm = math.gcd(math.gcd(rows_p, rows_s), TOKEN_TILE)
    n_p, n_s = rows_p // tm, rows_s // tm
    xp = x_prompt.reshape(rows_p, D)
    xs = x_sample.reshape(rows_s, D)
    bf = lambda w: w.astype(BF16)
    vec = lambda v: v.reshape(1, -1)

    xa, ga, xm, z = _in_proj_ab(xp, xs, vec(norm_mix[0]), bf(w_in_ab[0]), tm)
    rg_w = (w_conv_a[0], vec(b_conv_a[0]), bf(w_rg_a[0]), vec(b_rg_a[0]), bf(w_rg_x[0]), vec(b_rg_x[0]),
            vec(rg_lambda[0]))
    ya_p, hl_p = _rglru(xa, ga, 0, tp, jnp.zeros((bp, CONV_W - 1, D), F32), jnp.zeros((bp, 1, D), F32), *rg_w)
    ya_s, hl_s = _rglru(xa, ga, rows_p, ts, state_conv_a[0], state_rglru_h[0].reshape(bs, 1, D), *rg_w)

    pre_w = (w_conv_b[0], vec(b_conv_b[0]), bf(w_q_b[0]), bf(w_k_b[0]), bf(w_v_b[0]), bf(w_if_b[0].T),
             b_if_b[0].reshape(2 * NH_B, 1))
    rec_w = (g_hnorm_b[0].reshape(1, D), vec(skip_b[0]))
    lanes = lambda m: jnp.broadcast_to(m[:, :, None], m.shape + (LANES,))
    pre_p = _mlstm_pre(xm, 0, tp, jnp.zeros((bp, CONV_W - 1, D), F32), jnp.zeros((bp, NH_B, LANES), F32), *pre_w)
    pre_s = _mlstm_pre(xm, rows_p, ts, state_conv_b[0], lanes(state_mlstm_m[0]), *pre_w)
    m_p, m_s = pre_p[6], pre_s[6]
    yb_p, c_p, nn_p = _mlstm(*pre_p[1:6], pre_p[0], z, 0, tp, jnp.zeros((bp, NH_B, DH_B, DH_B), F32),
                             jnp.zeros((bp, NH_B, DH_B), F32), *rec_w)
    yb_s, c_s, nn_s = _mlstm(*pre_s[1:6], pre_s[0], z, rows_p, ts, state_mlstm_c[0], state_mlstm_n[0], *rec_w)

    w_out = bf(w_out_ab[0])
    y = _out_proj((xp, xs), [(ya_p, ya_s), (yb_p, yb_s)], [w_out[:D], w_out[D:]], tm, n_p)
    y = _ffn_dense(y, vec(norm_ffn[0]), bf(w1_dense[0]), bf(w3_dense[0]), bf(w2_dense[0]), tm)

    tile2 = lambda gain: jnp.tile(gain, D_KV // HEAD_DIM).reshape(1, D_KV)
    q, kw, vw, k, v = _in_proj_att(y, vec(norm_mix[1]), bf(w_in_att[0]), tile2(g_qnorm[0]), tile2(g_knorm[0]), tm)
    half = WINDOW // 2
    ncp = tp // CHUNK
    sinks2 = sinks[0].reshape(1, N_HEADS)
    hist_p = (lambda bi, ci: (bi * ncp + jnp.maximum(ci - 2, 0), 0), lambda bi, ci: (bi * ncp + jnp.maximum(ci - 1, 0), 0))
    o_p = _attention(q, kw, vw, kw, vw, hist_p, 0, tp, bp, True, _bucket_table(CHUNK), rel_bias, sinks2)

    def widen(cache):
        wide = jnp.broadcast_to(cache[:, :, :, None, :], (bs, WINDOW, N_KV, GROUP, HEAD_DIM))
        return wide.reshape(bs * WINDOW, D).astype(BF16)

    ck, cv = widen(cache_swa_k[0]), widen(cache_swa_v[0])
    hist_s = (lambda bi, ci: (2 * bi, 0), lambda bi, ci: (2 * bi + 1, 0))
    o_s = _attention(q, kw, vw, ck, cv, hist_s, rows_p, ts, bs, False, _bucket_table(ts), rel_bias, sinks2)
    y = _out_proj(y, [(o_p, o_s)], [bf(w_out_att[0])], tm, n_p)
    yp, ys = _moe(y, vec(norm_ffn[1]), w_router[0], bf(w1_moe[0]), bf(w3_moe[0]), bf(w2_moe[0]), tm, n_p, n_s)

    def tail(a, rows, b, t, keep):
        if rows == 0:
            return jnp.stack([a[(i + 1) * t - keep:(i + 1) * t] for i in range(b)])
        return a[rows:rows + b * t].reshape(b, t, -1)[:, t - keep:]

    kv4 = lambda a, b: a.reshape(b, -1, N_KV, HEAD_DIM)
    one = lambda a: a[None]
    k_s = jnp.concatenate([cache_swa_k[0][:, ts:], kv4(k[rows_p:], bs)], axis=1)
    v_s = jnp.concatenate([cache_swa_v[0][:, ts:], kv4(v[rows_p:], bs)], axis=1)
    return (yp.reshape(bp, tp, D), ys.reshape(bs, ts, D),
            one(tail(xa, 0, bp, tp, CONV_W - 1)), one(hl_p.reshape(bp, D)), one(tail(xm, 0, bp, tp, CONV_W - 1)),
            one(c_p), one(nn_p), one(m_p[:, :, 0]),
            one(kv4(tail(k, 0, bp, tp, WINDOW), bp)), one(kv4(tail(v, 0, bp, tp, WINDOW), bp)),
            one(tail(xa, rows_p, bs, ts, CONV_W - 1)), one(hl_s.reshape(bs, D)),
            one(tail(xm, rows_p, bs, ts, CONV_W - 1)), one(c_s), one(nn_s), one(m_s[:, :, 0]),
            one(k_s), one(v_s))
```

```python
import functools
import math

import jax
import jax.numpy as jnp
from jax import lax
from jax.experimental import pallas as pl
from jax.experimental.pallas import tpu as pltpu

F32 = jnp.float32
BF16 = jnp.bfloat16

D = 1024
CHUNK = 64
CONV_W = 4
NB_A = 8
BS_A = D // NB_A
RG_C = 8.0
NH_B = 4
DH_B = D // NH_B
N_HEADS = 16
HEAD_DIM = D // N_HEADS
N_KV = 4
GROUP = N_HEADS // N_KV
D_KV = N_KV * HEAD_DIM
WINDOW = 128
NUM_BUCKETS = 32
MAX_DISTANCE = 128
N_EXPERTS = 8
EPS = 1e-6
LANES = 128
SUB = D // LANES
NEG_INF = float("-inf")

TOKEN_TILE = 512
EXPERT_TILE = 512
FF_SPLIT_DENSE = 3
FF_SPLIT_MOE = 2
SEQ_BLOCK = 256
VMEM_LIMIT = 56 * 1024 * 1024


def _params(sem):
    return pltpu.CompilerParams(dimension_semantics=sem, vmem_limit_bytes=VMEM_LIMIT)


def _full(shape):
    return pl.BlockSpec(shape, lambda *_: (0,) * len(shape))


def _resident(shape):
    return pl.BlockSpec(shape, lambda *_: (0,) * len(shape), pipeline_mode=pl.Buffered(1))


def _rows(tm, c):
    return pl.BlockSpec((tm, c), lambda i: (i, 0))


def _pair(tm, c, n_p):
    return [pl.BlockSpec((tm, c), lambda i: (jnp.minimum(i, n_p - 1), 0)),
            pl.BlockSpec((tm, c), lambda i: (jnp.maximum(i - n_p, 0), 0))]


def _rms(x, g):
    ms = jnp.mean(x * x, axis=-1, keepdims=True)
    return x * lax.rsqrt(ms + EPS) * g


def _silu(x):
    return x * jax.nn.sigmoid(x)


def _sigmoid_tanh(x):
    return 0.5 * jnp.tanh(0.5 * x) + 0.5


def _softplus(x):
    return jnp.maximum(x, 0.0) + jnp.log1p(jnp.exp(-jnp.abs(x)))


def _gelu_tanh(x):
    c = math.sqrt(2.0 / math.pi)
    return x * (0.5 * (1.0 + jnp.tanh(c * (x + 0.044715 * (x * x * x)))))


def _dot(a, b):
    return jnp.dot(a, b, preferred_element_type=F32)


def _dot_nt(a, b):
    return lax.dot_general(a, b, (((1,), (1,)), ((), ())), preferred_element_type=F32)


def _dot_tn(a, b):
    return lax.dot_general(a, b, (((0,), (0,)), ((), ())), preferred_element_type=F32)


def _in_proj_ab_kernel(n_p, xp_ref, xs_ref, g_ref, w_ref, *o_refs):
    i = pl.program_id(0)
    x = jnp.where(i < n_p, xp_ref[...], xs_ref[...])
    xn = _rms(x, g_ref[...]).astype(BF16)
    for c, o_ref in enumerate(o_refs):
        o_ref[...] = _dot(xn, w_ref[:, c * D:(c + 1) * D])


def _in_proj_ab(xp, xs, g, w, tm):
    n_p, n_s = xp.shape[0] // tm, xs.shape[0] // tm
    n = xp.shape[0] + xs.shape[0]
    n_out = w.shape[1] // D
    return pl.pallas_call(
        functools.partial(_in_proj_ab_kernel, n_p),
        grid=(n_p + n_s,),
        in_specs=_pair(tm, D, n_p) + [_full((1, D)), _resident(w.shape)],
        out_specs=[_rows(tm, D)] * n_out,
        out_shape=[jax.ShapeDtypeStruct((n, D), F32)] * n_out,
        compiler_params=_params(("parallel",)),
        name="in_proj_ab",
    )(xp, xs, g, w)


def _tile_rows_load(ref):
    tm = ref.shape[0] // SUB
    return jnp.concatenate([ref[pl.ds(s, tm, stride=SUB), :] for s in range(SUB)], axis=1)


def _tile_rows_store(ref, val):
    tm = val.shape[0]
    for s in range(SUB):
        ref[pl.ds(s, tm, stride=SUB), :] = val[:, s * LANES:(s + 1) * LANES]


def _tile_spec(tm):
    return pl.BlockSpec((tm * SUB, LANES), lambda i: (i, 0))


def _out_proj_kernel(n_p, n_res, n_a, tiled_out, *refs):
    i = pl.program_id(0)
    res_refs = refs[:n_res]
    a_refs = refs[n_res:n_res + 2 * n_a]
    w_refs = refs[n_res + 2 * n_a:n_res + 3 * n_a]
    o_ref = refs[-1]
    if n_res == 2:
        acc = jnp.where(i < n_p, res_refs[0][...], res_refs[1][...])
    else:
        acc = res_refs[0][...]
    for k in range(n_a):
        a = jnp.where(i < n_p, a_refs[2 * k][...], a_refs[2 * k + 1][...])
        acc = acc + _dot(a, w_refs[k][...])
    if tiled_out:
        _tile_rows_store(o_ref, acc)
    else:
        o_ref[...] = acc


def _out_proj(res, a_pairs, ws, tm, n_p, tiled_out=False):
    res = res if isinstance(res, tuple) else (res,)
    n = sum(a.shape[0] for a in a_pairs[0])
    res_specs = _pair(tm, D, n_p) if len(res) == 2 else [_rows(tm, D)]
    a_specs = []
    for a in a_pairs:
        a_specs += _pair(tm, a[0].shape[1], n_p)
    return pl.pallas_call(
        functools.partial(_out_proj_kernel, n_p, len(res), len(a_pairs), tiled_out),
        grid=(n // tm,),
        in_specs=res_specs + a_specs + [_resident(w.shape) for w in ws],
        out_specs=_tile_spec(tm) if tiled_out else _rows(tm, D),
        out_shape=jax.ShapeDtypeStruct((n * SUB, LANES) if tiled_out else (n, D), F32),
        compiler_params=_params(("parallel",)),
        name="out_proj",
    )(*res, *[x for a in a_pairs for x in a], *ws)


def _ffn_dense_kernel(x_ref, g_ref, w1_ref, w3_ref, w2_ref, o_ref):
    x = x_ref[...]
    xn = _rms(x, g_ref[...]).astype(BF16)
    step = w1_ref.shape[1] // FF_SPLIT_DENSE
    acc = x
    for c in range(FF_SPLIT_DENSE):
        sl = slice(c * step, (c + 1) * step)
        a = (_silu(_dot(xn, w1_ref[:, sl])) * _dot(xn, w3_ref[:, sl])).astype(BF16)
        acc = acc + _dot(a, w2_ref[sl, :])
    o_ref[...] = acc


def _ffn_dense(x, g, w1, w3, w2, tm):
    n = x.shape[0]
    return pl.pallas_call(
        _ffn_dense_kernel,
        grid=(n // tm,),
        in_specs=[_rows(tm, D), _full((1, D)), _resident(w1.shape), _resident(w3.shape), _resident(w2.shape)],
        out_specs=_rows(tm, D),
        out_shape=jax.ShapeDtypeStruct((n, D), F32),
        compiler_params=_params(("parallel",)),
        name="ffn_dense",
    )(x, g, w1, w3, w2)


def _in_proj_att_kernel(x_ref, g_ref, w_ref, gq_ref, gk_ref, q_ref, kw_ref, vw_ref, k_ref, v_ref):
    xn = _rms(x_ref[...], g_ref[...]).astype(BF16)
    cw = D_KV
    r = lax.broadcasted_iota(jnp.int32, (cw, cw), 0) // HEAD_DIM
    c = lax.broadcasted_iota(jnp.int32, (cw, cw), 1) // HEAD_DIM
    group_ones = (r == c).astype(BF16)
    er = lax.broadcasted_iota(jnp.int32, (D_KV, D), 0)
    ec = lax.broadcasted_iota(jnp.int32, (D_KV, D), 1)
    widen = ((er // HEAD_DIM == ec // (GROUP * HEAD_DIM)) & (er % HEAD_DIM == ec % HEAD_DIM)).astype(BF16)

    def head_norm(y, gain):
        ms = _dot((y * y).astype(BF16), group_ones) * (1.0 / HEAD_DIM)
        return y * lax.rsqrt(ms + EPS) * gain

    for b in range(D // cw):
        y = _dot(xn, w_ref[:, b * cw:(b + 1) * cw])
        q_ref[:, b * cw:(b + 1) * cw] = (head_norm(y, gq_ref[...]) * (HEAD_DIM ** -0.5)).astype(BF16)
    k = head_norm(_dot(xn, w_ref[:, D:D + D_KV]), gk_ref[...])
    v = _dot(xn, w_ref[:, D + D_KV:D + 2 * D_KV])
    k_ref[...] = k
    v_ref[...] = v
    kw_ref[...] = _dot(k.astype(BF16), widen).astype(BF16)
    vw_ref[...] = _dot(v.astype(BF16), widen).astype(BF16)


def _in_proj_att(x, g, w, gq, gk, tm):
    n = x.shape[0]
    return pl.pallas_call(
        _in_proj_att_kernel,
        grid=(n // tm,),
        in_specs=[_rows(tm, D), _full((1, D)), _resident(w.shape), _full((1, D_KV)), _full((1, D_KV))],
        out_specs=[_rows(tm, D)] * 3 + [_rows(tm, D_KV)] * 2,
        out_shape=[jax.ShapeDtypeStruct((n, D), BF16)] * 3 + [jax.ShapeDtypeStruct((n, D_KV), F32)] * 2,
        compiler_params=_params(("parallel",)),
        name="in_proj_att",
    )(x, g, w, gq, gk)


def _causal_conv(x, xbuf, conv0_ref, wc_ref, bc_ref, first):
    tb = x.shape[0]

    @pl.when(first)
    def _():
        xbuf[5:8, :] = conv0_ref[0]

    xbuf[8:8 + tb, :] = x
    y = bc_ref[...] + xbuf[5:5 + tb, :] * wc_ref[0:1, :]
    for j in range(1, CONV_W):
        y = y + xbuf[5 + j:5 + j + tb, :] * wc_ref[j:j + 1, :]
    xbuf[5:8, :] = xbuf[5 + tb:8 + tb, :]
    return y


def _seq_specs(bsz, seq, tb, row0):
    nc = seq // tb
    off = row0 // tb
    return nc, (lambda bi, ci: (off + bi * nc + ci, 0)), (lambda bi, ci: (bi * nc + ci, 0))


def _scan_by_doubling(a, u, pos, axis, length):
    d = 1
    while d < length:
        keep = pos >= d
        u = jnp.where(keep, a * pltpu.roll(u, d, axis) + u, u)
        a = jnp.where(keep, a * pltpu.roll(a, d, axis), a)
        d *= 2
    return a, u


def _linear_scan(a, u, h0, row):
    a, u = _scan_by_doubling(a, u, row, 0, a.shape[0])
    return a * h0 + u


def _rglru_kernel(xa_ref, ga_ref, conv0_ref, h0_ref, wc_ref, bc_ref, wa_ref, ba_ref, wx_ref, bx_ref, lam_ref,
                  ya_ref, hl_ref, xbuf):
    first = pl.program_id(1) == 0
    tb = xa_ref.shape[0]

    @pl.when(first)
    def _():
        hl_ref[0] = h0_ref[0]

    y = _causal_conv(xa_ref[...], xbuf, conv0_ref, wc_ref, bc_ref, first)
    row = lax.broadcasted_iota(jnp.int32, (tb, BS_A), 0)
    for n in range(NB_A):
        sl = slice(n * BS_A, (n + 1) * BS_A)
        yn = y[:, sl]
        yb = yn.astype(BF16)
        r = _sigmoid_tanh(_dot(yb, wa_ref[n]) + ba_ref[:, sl])
        ig = _sigmoid_tanh(_dot(yb, wx_ref[n]) + bx_ref[:, sl])
        log_a = (-RG_C) * r * _softplus(-lam_ref[:, sl])
        a = jnp.exp(log_a)
        th = jnp.tanh(log_a)
        u = yn * ig * jnp.sqrt(-2.0 * th / (1.0 - th))
        h = _linear_scan(a, u, hl_ref[0, :, sl], row)
        hl_ref[0, :, sl] = h[tb - 1:tb, :]
        ya_ref[:, sl] = (h * _gelu_tanh(ga_ref[:, sl])).astype(BF16)


def _rglru(xa, ga, row0, seq, conv0, h0, wc, bc, wa, ba, wx, bx, lam):
    bsz = conv0.shape[0]
    tb = min(seq, SEQ_BLOCK)
    nc, in_map, out_map = _seq_specs(bsz, seq, tb, row0)
    state3 = lambda bi, ci: (bi, 0, 0)
    return pl.pallas_call(
        _rglru_kernel,
        grid=(bsz, nc),
        in_specs=[pl.BlockSpec((tb, D), in_map), pl.BlockSpec((tb, D), in_map),
                  pl.BlockSpec((1, CONV_W - 1, D), state3), pl.BlockSpec((1, 1, D), state3),
                  _full(wc.shape), _full(bc.shape), _full(wa.shape), _full(ba.shape), _full(wx.shape),
                  _full(bx.shape), _full(lam.shape)],
        out_specs=[pl.BlockSpec((tb, D), out_map), pl.BlockSpec((1, 1, D), state3)],
        out_shape=[jax.ShapeDtypeStruct((bsz * seq, D), BF16), jax.ShapeDtypeStruct((bsz, 1, D), F32)],
        scratch_shapes=[pltpu.VMEM((tb + 8, D), F32)],
        compiler_params=_params(("parallel", "arbitrary")),
        name="rglru",
    )(xa, ga, conv0, h0, wc, bc, wa, ba, wx, bx, lam)


def _mlstm_pre_kernel(chunk, xm_ref, conv0_ref, m0_ref, wc_ref, bc_ref, wq_ref, wk_ref, wv_ref, wif_ref, bif_ref,
                      xc_ref, q_ref, k_ref, v_ref, w_ref, col_ref, m_ref, xbuf):
    first = pl.program_id(1) == 0
    tb = xm_ref.shape[0]
    L = chunk

    @pl.when(first)
    def _():
        m_ref[...] = m0_ref[...]

    x = xm_ref[...]
    xc = _silu(_causal_conv(x, xbuf, conv0_ref, wc_ref, bc_ref, first))
    xc_ref[...] = xc
    gates = bif_ref[...]
    for h in range(NH_B):
        sl = slice(h * DH_B, (h + 1) * DH_B)
        xch = xc[:, sl].astype(BF16)
        q = _dot(xch, wq_ref[h]).astype(BF16)
        k = _dot(xch, wk_ref[h]).astype(BF16)
        v = _dot(x[:, sl].astype(BF16), wv_ref[h]).astype(BF16)
        q_ref[:, sl] = q
        k_ref[:, sl] = k
        v_ref[:, sl] = v
        gates = (gates + _dot_nt(wif_ref[:, sl], q) + _dot_nt(wif_ref[:, D + h * DH_B:D + (h + 1) * DH_B], k)
                 + _dot_nt(wif_ref[:, 2 * D + h * DH_B:2 * D + (h + 1) * DH_B], v))
    row = lax.broadcasted_iota(jnp.int32, gates.shape, 0)
    gates = jnp.where(row < NH_B, gates, -_softplus(-gates))

    tt = lax.broadcasted_iota(jnp.int32, (L, L), 0)
    ss = lax.broadcasted_iota(jnp.int32, (L, L), 1)
    eye = tt == ss
    causal = ss <= tt
    lane = lax.broadcasted_iota(jnp.int32, (L, LANES), 1)

    def to_col(row_vec):
        return jnp.sum(jnp.where(eye, row_vec, 0.0), axis=1, keepdims=True)

    m_prev = [m_ref[0, h:h + 1, 0:1] for h in range(NH_B)]
    for j in range(tb // L):
        table = jnp.zeros((L, LANES), F32)
        for h in range(NH_B):
            i_row = gates[h:h + 1, j * L:(j + 1) * L]
            f_row = gates[NH_B + h:NH_B + h + 1, j * L:(j + 1) * L]
            f_col = to_col(f_row)
            b_col = jnp.sum(jnp.where(causal, f_row, 0.0), axis=1, keepdims=True)
            b_row = jnp.sum(jnp.where(tt <= ss, f_col, 0.0), axis=0, keepdims=True)
            w_row = i_row - b_row
            w_col = to_col(i_row) - b_col
            big_m = jnp.maximum(m_prev[h], jnp.max(jnp.where(causal, w_row, NEG_INF), axis=1, keepdims=True))
            m_last = big_m[L - 1:L, :]
            w_ref[j, h:h + 1, :] = w_row
            for c, col in enumerate((big_m, jnp.exp(m_prev[h] - big_m), jnp.exp(-(b_col + big_m)),
                                     jnp.exp(w_col - m_last))):
                table = jnp.where(lane == c * NH_B + h, col, table)
            m_prev[h] = b_col[L - 1:L, :] + m_last
        col_ref[j * L:(j + 1) * L, :] = table
    for h in range(NH_B):
        m_ref[0, h:h + 1, :] = jnp.broadcast_to(m_prev[h], (1, LANES))


def _mlstm_pre(xm, row0, seq, conv0, m0, wc, bc, wq, wk, wv, wif_t, bif):
    bsz = conv0.shape[0]
    tb = min(seq, SEQ_BLOCK)
    chunk = tb
    nc, in_map, out_map = _seq_specs(bsz, seq, tb, row0)
    n = bsz * seq
    st3 = lambda bi, ci: (bi, 0, 0)
    return pl.pallas_call(
        functools.partial(_mlstm_pre_kernel, chunk),
        grid=(bsz, nc),
        in_specs=[pl.BlockSpec((tb, D), in_map), pl.BlockSpec((1, CONV_W - 1, D), st3),
                  pl.BlockSpec((1, NH_B, LANES), st3),
                  _full(wc.shape), _full(bc.shape), _full(wq.shape), _full(wk.shape), _full(wv.shape),
                  _full(wif_t.shape), _full(bif.shape)],
        out_specs=[pl.BlockSpec((tb, D), out_map)] * 4
                  + [pl.BlockSpec((tb // chunk, NH_B, chunk), lambda bi, ci: (bi * nc + ci, 0, 0)),
                     pl.BlockSpec((tb, LANES), out_map), pl.BlockSpec((1, NH_B, LANES), st3)],
        out_shape=[jax.ShapeDtypeStruct((n, D), F32)] + [jax.ShapeDtypeStruct((n, D), BF16)] * 3
                  + [jax.ShapeDtypeStruct((n // chunk, NH_B, chunk), F32), jax.ShapeDtypeStruct((n, LANES), F32),
                     jax.ShapeDtypeStruct(m0.shape, F32)],
        scratch_shapes=[pltpu.VMEM((tb + 8, D), F32)],
        compiler_params=_params(("parallel", "arbitrary")),
        name="mlstm_pre",
    )(xm, conv0, m0, wc, bc, wq, wk, wv, wif_t, bif)


def _mlstm_kernel(chunk, q_ref, k_ref, v_ref, w_ref, col_ref, xc_ref, z_ref, c0_ref, n0_ref, gh_ref, skip_ref,
                  yb_ref, c_ref, n_ref):
    tb = q_ref.shape[0]
    L = chunk
    scale = DH_B ** -0.5

    @pl.when(pl.program_id(1) == 0)
    def _():
        c_ref[...] = c0_ref[...]
        n_ref[...] = n0_ref[...]

    causal = lax.broadcasted_iota(jnp.int32, (L, L), 1) <= lax.broadcasted_iota(jnp.int32, (L, L), 0)
    heads = range(NH_B)
    hsl = [slice(h * DH_B, (h + 1) * DH_B) for h in heads]
    for j in range(tb // L):
        rows = slice(j * L, (j + 1) * L)
        cols = col_ref[rows, :]
        col = lambda c, h: cols[:, c * NH_B + h:c * NH_B + h + 1]
        q = [q_ref[rows, hsl[h]] for h in heads]
        k = [k_ref[rows, hsl[h]] for h in heads]
        v = [v_ref[rows, hsl[h]] for h in heads]
        s = [_dot_nt(q[h], k[h]) for h in heads]
        qc = [_dot(q[h], c_ref[0, h].astype(BF16)) for h in heads]
        qn = [_dot_nt(q[h], jnp.broadcast_to(n_ref[0, h:h + 1, :], (8, DH_B)).astype(BF16))[:, 0:1] for h in heads]
        p = [(s[h] * scale * jnp.where(causal, jnp.exp(w_ref[j, h:h + 1, :] - col(0, h)), 0.0)).astype(BF16)
             for h in heads]
        pv = [_dot(p[h], v[h]) for h in heads]
        psum = [_dot(p[h], jnp.ones((L, LANES), BF16))[:, 0:1] for h in heads]
        for h in heads:
            sc = col(1, h)
            keep = sc[L - 1:L, :]
            kd = k[h].astype(F32) * (col(3, h) * scale)
            c_ref[0, h] = keep * c_ref[0, h] + _dot_tn(kd.astype(BF16), v[h])
            n_ref[0, h:h + 1, :] = keep * n_ref[0, h:h + 1, :] + jnp.sum(kd, axis=0, keepdims=True)
            num = pv[h] + sc * qc[h]
            den = psum[h] + sc * qn[h]
            hs = num / jnp.maximum(jnp.abs(den), col(2, h))
            mu = jnp.mean(hs, axis=1, keepdims=True)
            dev = hs - mu
            var = jnp.mean(dev * dev, axis=1, keepdims=True)
            hn = dev * lax.rsqrt(var + EPS) * gh_ref[:, hsl[h]]
            out = (hn + skip_ref[:, hsl[h]] * xc_ref[rows, hsl[h]]) * _silu(z_ref[rows, hsl[h]])
            yb_ref[rows, hsl[h]] = out.astype(BF16)


def _mlstm(q, k, v, w, col, xc, z, row0, seq, c0, n0, gh, skip):
    bsz = c0.shape[0]
    tb = min(seq, SEQ_BLOCK)
    chunk = tb
    nc, z_map, own_map = _seq_specs(bsz, seq, tb, row0)
    st4 = lambda bi, ci: (bi, 0, 0, 0)
    st3 = lambda bi, ci: (bi, 0, 0)
    return pl.pallas_call(
        functools.partial(_mlstm_kernel, chunk),
        grid=(bsz, nc),
        in_specs=[pl.BlockSpec((tb, D), own_map)] * 3
                 + [pl.BlockSpec((tb // chunk, NH_B, chunk), lambda bi, ci: (bi * nc + ci, 0, 0)),
                    pl.BlockSpec((tb, LANES), own_map),
                    pl.BlockSpec((tb, D), own_map), pl.BlockSpec((tb, D), z_map),
                    pl.BlockSpec((1, NH_B, DH_B, DH_B), st4), pl.BlockSpec((1, NH_B, DH_B), st3),
                    _full(gh.shape), _full(skip.shape)],
        out_specs=[pl.BlockSpec((tb, D), own_map), pl.BlockSpec((1, NH_B, DH_B, DH_B), st4),
                   pl.BlockSpec((1, NH_B, DH_B), st3)],
        out_shape=[jax.ShapeDtypeStruct((bsz * seq, D), BF16), jax.ShapeDtypeStruct(c0.shape, F32),
                   jax.ShapeDtypeStruct(n0.shape, F32)],
        compiler_params=_params(("parallel", "arbitrary")),
        name="mlstm",
    )(q, k, v, w, col, xc, z, c0, n0, gh, skip)


def _attn_kernel(masked, q_ref, kh0_ref, kh1_ref, ko_ref, vh0_ref, vh1_ref, vo_ref, bucket_ref, relb_ref, sink_ref,
                 o_ref, bias_s, s_scr, p_scr):
    L = q_ref.shape[0]
    nk = WINDOW + L
    ci = pl.program_id(1)

    @pl.when((pl.program_id(0) == 0) & (ci == 0))
    def _():
        bucket = bucket_ref[...]
        for h in range(N_HEADS):
            acc = jnp.zeros((L, nk), F32)
            for b in range(NUM_BUCKETS):
                acc = jnp.where(bucket == b, relb_ref[b, h], acc)
            bias_s[h * L:(h + 1) * L, :] = acc

    kcat = jnp.concatenate([kh0_ref[...], kh1_ref[...], ko_ref[...]], axis=0)
    vcat = jnp.concatenate([vh0_ref[...], vh1_ref[...], vo_ref[...]], axis=0)
    gw = GROUP * HEAD_DIM
    slot = lax.broadcasted_iota(jnp.int32, (L, gw), 1) // HEAD_DIM
    if masked:
        kk = lax.broadcasted_iota(jnp.int32, (1, nk), 1)
        half = WINDOW // 2
        valid = (kk >= WINDOW) | ((kk >= half) & (ci >= 1)) | (ci >= 2)
    gl = GROUP * L
    for g in range(N_KV):
        gsl = slice(g * gw, (g + 1) * gw)
        qg = q_ref[:, gsl]
        zero = jnp.zeros_like(qg)
        qs = jnp.concatenate([jnp.where(slot == j, qg, zero) for j in range(GROUP)], axis=0)
        s_scr[g * gl:(g + 1) * gl, :] = _dot_nt(qs, kcat[:, gsl])
    for h in range(N_HEADS):
        rows = slice(h * L, (h + 1) * L)
        s = s_scr[rows, :] + bias_s[rows, :]
        if masked:
            s = jnp.where(valid, s, NEG_INF)
        sink = sink_ref[0, h]
        mx = jnp.maximum(jnp.max(s, axis=1, keepdims=True), sink)
        e = jnp.exp(s - mx)
        den = jnp.sum(e, axis=1, keepdims=True) + jnp.exp(sink - mx)
        p_scr[rows, :] = (e / den).astype(BF16)
    for g in range(N_KV):
        gsl = slice(g * gw, (g + 1) * gw)
        ow = _dot(p_scr[g * gl:(g + 1) * gl, :], vcat[:, gsl])
        og = jnp.where(slot == 0, ow[0:L, :], 0.0)
        for j in range(1, GROUP):
            og = jnp.where(slot == j, ow[j * L:(j + 1) * L, :], og)
        o_ref[:, gsl] = og.astype(BF16)


def _attention(q, k_own, v_own, k_hist, v_hist, hist_maps, row0, seq, bsz, masked, bucket, rel_bias, sinks):
    L = min(seq, CHUNK)
    nc, in_map, out_map = _seq_specs(bsz, seq, L, row0)
    half = WINDOW // 2
    h0_map, h1_map = hist_maps
    smem = functools.partial(pl.BlockSpec, memory_space=pltpu.SMEM)
    return pl.pallas_call(
        functools.partial(_attn_kernel, masked),
        grid=(bsz, nc),
        in_specs=[pl.BlockSpec((L, D), in_map),
                  pl.BlockSpec((half, D), h0_map), pl.BlockSpec((half, D), h1_map), pl.BlockSpec((L, D), in_map),
                  pl.BlockSpec((half, D), h0_map), pl.BlockSpec((half, D), h1_map), pl.BlockSpec((L, D), in_map),
                  _full(bucket.shape), smem(), smem()],
        out_specs=pl.BlockSpec((L, D), out_map),
        out_shape=jax.ShapeDtypeStruct((bsz * seq, D), BF16),
        scratch_shapes=[pltpu.VMEM((N_HEADS * L, WINDOW + L), F32)] * 2 + [pltpu.VMEM((N_HEADS * L, WINDOW + L), BF16)],
        compiler_params=_params(("arbitrary", "arbitrary")),
        name="swa",
    )(q, k_hist, k_hist, k_own, v_hist, v_hist, v_own, bucket, rel_bias, sinks)


def _router_kernel(x_ref, g_ref, wr_ref, info_ref, cnt_ref):
    tm = x_ref.shape[0] // SUB

    @pl.when(pl.program_id(0) == 0)
    def _():
        cnt_ref[...] = jnp.zeros_like(cnt_ref)

    xn = _rms(_tile_rows_load(x_ref), g_ref[...])
    hi = xn.astype(BF16)
    lo = (xn - hi.astype(F32)).astype(BF16)
    w = wr_ref[...]
    whi = w.astype(BF16)
    wlo = (w - whi.astype(F32)).astype(BF16)
    logits = _dot(hi, whi) + _dot(hi, wlo) + _dot(lo, whi)
    lane = lax.broadcasted_iota(jnp.int32, (tm, LANES), 1)
    logits = jnp.where(lane < N_EXPERTS, logits, NEG_INF)
    m1 = jnp.max(logits, axis=1, keepdims=True)
    i1 = jnp.min(jnp.where(logits == m1, lane, LANES), axis=1, keepdims=True)
    rest = jnp.where(lane == i1, NEG_INF, logits)
    m2 = jnp.max(rest, axis=1, keepdims=True)
    i2 = jnp.min(jnp.where(rest == m2, lane, LANES), axis=1, keepdims=True)
    e2 = jnp.exp(m2 - m1)
    g1 = 1.0 / (1.0 + e2)
    g2 = e2 / (1.0 + e2)
    sel = (lane == i1) | (lane == i2)
    tri = (lax.broadcasted_iota(jnp.int32, (tm, tm), 0) > lax.broadcasted_iota(jnp.int32, (tm, tm), 1)).astype(BF16)
    rank = cnt_ref[...] + _dot(tri, sel.astype(BF16))
    r1 = jnp.sum(jnp.where(lane == i1, rank, 0.0), axis=1, keepdims=True)
    r2 = jnp.sum(jnp.where(lane == i2, rank, 0.0), axis=1, keepdims=True)
    cnt_ref[...] = cnt_ref[...] + jnp.sum(sel.astype(F32), axis=0, keepdims=True)
    info = jnp.where(lane == 0, i1.astype(F32), 0.0)
    info = jnp.where(lane == 1, i2.astype(F32), info)
    info = jnp.where(lane == 2, g1, info)
    info = jnp.where(lane == 3, g2, info)
    info = jnp.where(lane == 4, r1, info)
    info = jnp.where(lane == 5, r2, info)
    info_ref[...] = info


def _router(x, g, wr, tm):
    n = x.shape[0] // SUB
    return pl.pallas_call(
        _router_kernel,
        grid=(n // tm,),
        in_specs=[_tile_spec(tm), _full((1, D)), _full(wr.shape)],
        out_specs=[_rows(tm, LANES), _full((1, LANES))],
        out_shape=[jax.ShapeDtypeStruct((n, LANES), F32), jax.ShapeDtypeStruct((1, LANES), F32)],
        compiler_params=_params(("arbitrary",)),
        name="moe_router",
    )(x, g, wr)


def _token_tile(ref, t):
    return ref.at[pl.ds(pl.multiple_of(t * SUB, SUB), SUB)]


def _moe_scatter_kernel(te, zpos_ref, dest_ref, x_ref, xs_hbm, zero_s, sem):
    tm = x_ref.shape[0] // SUB
    span = te * SUB

    def rows_of(tile_start):
        return xs_hbm.at[pl.ds(pl.multiple_of(tile_start * SUB, span), span)]

    @pl.when(pl.program_id(0) == 0)
    def _():
        zero_s[...] = jnp.zeros_like(zero_s)
        for e in range(N_EXPERTS):
            pltpu.make_async_copy(zero_s, rows_of(zpos_ref[e]), sem).start()
        for e in range(N_EXPERTS):
            pltpu.make_async_copy(zero_s, rows_of(zpos_ref[e]), sem).wait()

        def clear_tail(t, carry):
            tail = pltpu.make_async_copy(zero_s, rows_of(t * te), sem)
            tail.start()
            tail.wait()
            return carry

        lax.fori_loop(zpos_ref[N_EXPERTS] // te, xs_hbm.shape[0] // span, clear_tail, 0)

    def copy(r, s):
        return pltpu.make_async_copy(_token_tile(x_ref, r), _token_tile(xs_hbm, dest_ref[s * tm + r]), sem)

    def start(r, carry):
        for s in range(2):
            copy(r, s).start(priority=s)
        return carry

    def wait(r, carry):
        for s in range(2):
            copy(r, s).wait()
        return carry

    lax.fori_loop(0, tm, start, 0)
    lax.fori_loop(0, tm, wait, 0)


def _moe_scatter(x, dest, zpos, n_rows, tm, te):
    n = x.shape[0] // SUB
    return pl.pallas_call(
        functools.partial(_moe_scatter_kernel, te),
        grid_spec=pltpu.PrefetchScalarGridSpec(
            num_scalar_prefetch=1,
            grid=(n // tm,),
            in_specs=[pl.BlockSpec((2 * tm,), lambda i, zp: (i,), memory_space=pltpu.SMEM),
                      pl.BlockSpec((tm * SUB, LANES), lambda i, zp: (i, 0))],
            out_specs=pl.BlockSpec(memory_space=pl.ANY),
            scratch_shapes=[pltpu.VMEM((te * SUB, LANES), F32), pltpu.SemaphoreType.DMA(())],
        ),
        out_shape=jax.ShapeDtypeStruct((n_rows * SUB, LANES), F32),
        compiler_params=_params(("arbitrary",)),
        name="moe_scatter",
    )(zpos, dest, x)


def _moe_ffn_kernel(n_ff, te_ref, nu_ref, xs_ref, g_ref, w1_ref, w3_ref, w2_ref, ys_ref, xn_s, acc_s):
    i, j = pl.program_id(0), pl.program_id(1)
    used = i < nu_ref[0]

    @pl.when(used & (j == 0))
    def _():
        xn_s[...] = _rms(_tile_rows_load(xs_ref), g_ref[...]).astype(BF16)
        acc_s[...] = jnp.zeros_like(acc_s)

    @pl.when(used)
    def _():
        xn = xn_s[...]
        a = (_silu(_dot(xn, w1_ref[0])) * _dot(xn, w3_ref[0])).astype(BF16)
        acc_s[...] += _dot(a, w2_ref[0])

    @pl.when(j == n_ff - 1)
    def _():
        _tile_rows_store(ys_ref, jnp.where(used, acc_s[...], 0.0))


def _moe_ffn(xs, g, w1, w3, w2, tile_expert, n_used, te):
    n_tiles = tile_expert.shape[0]
    d_ff = w1.shape[2]
    n_ff = FF_SPLIT_MOE
    tf = d_ff // n_ff

    def row_map(i, j, te_r, nu_r):
        return (jnp.minimum(i, nu_r[0] - 1), 0)

    def ff(i, j, nu_r):
        return jnp.where(i < nu_r[0], j, n_ff - 1)

    return pl.pallas_call(
        functools.partial(_moe_ffn_kernel, n_ff),
        grid_spec=pltpu.PrefetchScalarGridSpec(
            num_scalar_prefetch=2,
            grid=(n_tiles, n_ff),
            in_specs=[pl.BlockSpec((te * SUB, LANES), row_map),
                      pl.BlockSpec((1, D), lambda i, j, te_r, nu_r: (0, 0)),
                      pl.BlockSpec((1, D, tf), lambda i, j, te_r, nu_r: (te_r[i], 0, ff(i, j, nu_r))),
                      pl.BlockSpec((1, D, tf), lambda i, j, te_r, nu_r: (te_r[i], 0, ff(i, j, nu_r))),
                      pl.BlockSpec((1, tf, D), lambda i, j, te_r, nu_r: (te_r[i], ff(i, j, nu_r), 0))],
            out_specs=pl.BlockSpec((te * SUB, LANES), lambda i, j, te_r, nu_r: (i, 0)),
            scratch_shapes=[pltpu.VMEM((te, D), BF16), pltpu.VMEM((te, D), F32)],
        ),
        out_shape=jax.ShapeDtypeStruct((n_tiles * te * SUB, LANES), F32),
        compiler_params=_params(("arbitrary", "arbitrary")),
        name="moe_ffn",
    )(tile_expert, n_used, xs, g, w1, w3, w2)


def _moe_combine_kernel(n_p, dest_ref, info_ref, x_ref, ys_hbm, op_ref, os_ref, buf, sem):
    tm = x_ref.shape[0] // SUB
    i = pl.program_id(0)

    def copy(r, s):
        return pltpu.make_async_copy(_token_tile(ys_hbm, dest_ref[s * tm + r]), _token_tile(buf.at[s], r), sem)

    def start(r, carry):
        for s in range(2):
            copy(r, s).start(priority=s)
        return carry

    def wait(r, carry):
        for s in range(2):
            copy(r, s).wait()
        return carry

    lax.fori_loop(0, tm, start, 0)
    lax.fori_loop(0, tm, wait, 0)
    out = (_tile_rows_load(x_ref) + info_ref[:, 2:3] * _tile_rows_load(buf.at[0])
           + info_ref[:, 3:4] * _tile_rows_load(buf.at[1]))

    @pl.when(i < n_p)
    def _():
        op_ref[...] = out

    @pl.when(i >= n_p)
    def _():
        os_ref[...] = out


def _moe_combine(x, info, dest, ys, tm, n_p, n_s):
    out_p, out_s = _pair(tm, D, n_p)
    return pl.pallas_call(
        functools.partial(_moe_combine_kernel, n_p),
        grid=(n_p + n_s,),
        in_specs=[pl.BlockSpec((2 * tm,), lambda i: (i,), memory_space=pltpu.SMEM),
                  _rows(tm, LANES), _tile_spec(tm), pl.BlockSpec(memory_space=pl.ANY)],
        out_specs=[out_p, out_s],
        out_shape=[jax.ShapeDtypeStruct((n_p * tm, D), F32), jax.ShapeDtypeStruct((n_s * tm, D), F32)],
        scratch_shapes=[pltpu.VMEM((2, tm * SUB, LANES), F32), pltpu.SemaphoreType.DMA(())],
        compiler_params=_params(("arbitrary",)),
        name="moe_combine",
    )(dest, info, x, ys)


def _moe(x, g, w_router, w1, w3, w2, tm, n_p, n_s):
    n = x.shape[0] // SUB
    te = min(EXPERT_TILE, tm)
    wr = jnp.zeros((D, LANES), F32).at[:, :N_EXPERTS].set(w_router)
    info, cnt = _router(x, g, wr, tm)
    e1, e2 = info[:, 0].astype(jnp.int32), info[:, 1].astype(jnp.int32)
    r1, r2 = info[:, 4].astype(jnp.int32), info[:, 5].astype(jnp.int32)
    counts = cnt[0, :N_EXPERTS].astype(jnp.int32)
    padded = (counts + te - 1) // te * te
    ends = jnp.cumsum(padded)
    starts = ends - padded
    dest = jnp.stack([(starts[e1] + r1).reshape(n // tm, tm), (starts[e2] + r2).reshape(n // tm, tm)], axis=1)
    dest = dest.reshape(-1)
    n_tiles = (2 * n + N_EXPERTS * (te - 1)) // te
    tile_start = jnp.arange(n_tiles, dtype=jnp.int32) * te
    tile_expert = jnp.minimum(jnp.sum(tile_start[:, None] >= ends[None, :], axis=1), N_EXPERTS - 1).astype(jnp.int32)
    n_used = (ends[-1:] // te).astype(jnp.int32)
    last_tile = jnp.maximum(ends - te, 0).astype(jnp.int32)
    xs = _moe_scatter(x, dest, jnp.concatenate([last_tile, ends[-1:].astype(jnp.int32)]), n_tiles * te, tm, te)
    ys = _moe_ffn(xs, g, w1, w3, w2, tile_expert, n_used, te)
    return _moe_combine(x, info, dest, ys, tm, n_p, n_s)


def _t5_bucket(rel):
    n = -rel
    half = NUM_BUCKETS // 2
    ret = jnp.where(n < 0, half, 0)
    n = jnp.abs(n)
    max_exact = half // 2
    nf = jnp.maximum(n, 1).astype(F32)
    large = max_exact + (jnp.log(nf / max_exact) / math.log(MAX_DISTANCE / max_exact)
                         * (half - max_exact)).astype(jnp.int32)
    large = jnp.minimum(large, half - 1)
    return ret + jnp.where(n < max_exact, n, large)


def _bucket_table(length):
    kpos = jnp.arange(WINDOW + length) - WINDOW
    return _t5_bucket(kpos[None, :] - jnp.arange(length)[:, None]).astype(jnp.int32)


def kernel(x_prompt, x_sample, state_conv_a, state_rglru_h, state_conv_b, state_mlstm_c, state_mlstm_n, state_mlstm_m, cache_swa_k, cache_swa_v, norm_mix, norm_ffn, w_in_ab, w_conv_a, b_conv_a, w_rg_a, b_rg_a, w_rg_x, b_rg_x, rg_lambda, w_conv_b, b_conv_b, w_q_b, w_k_b, w_v_b, w_if_b, b_if_b, g_hnorm_b, skip_b, w_out_ab, w1_dense, w3_dense, w2_dense, w_in_att, g_qnorm, g_knorm, sinks, w_out_att, rel_bias, w_router, w1_moe, w3_moe, w2_moe):
    bp, tp, _ = x_prompt.shape
    bs, ts, _ = x_sample.shape
    assert norm_mix.shape[0] == 2 and w_in_ab.shape[0] == 1 and w_in_att.shape[0] == 1
    assert tp % CHUNK == 0 and ts <= CHUNK and cache_swa_k.shape[2] == WINDOW
    rows_p, rows_s = bp * tp, bs * ts
    tm = math.gcd(math.gcd(rows_p, rows_s), TOKEN_TILE)
    n_p, n_s = rows_p // tm, rows_s // tm
    xp = x_prompt.reshape(rows_p, D)
    xs = x_sample.reshape(rows_s, D)
    bf = lambda w: w.astype(BF16)
    vec = lambda v: v.reshape(1, -1)

    xa, ga, xm, z = _in_proj_ab(xp, xs, vec(norm_mix[0]), bf(w_in_ab[0]), tm)
    rg_w = (w_conv_a[0], vec(b_conv_a[0]), bf(w_rg_a[0]), vec(b_rg_a[0]), bf(w_rg_x[0]), vec(b_rg_x[0]),
            vec(rg_lambda[0]))
    ya_p, hl_p = _rglru(xa, ga, 0, tp, jnp.zeros((bp, CONV_W - 1, D), F32), jnp.zeros((bp, 1, D), F32), *rg_w)
    ya_s, hl_s = _rglru(xa, ga, rows_p, ts, state_conv_a[0], state_rglru_h[0].reshape(bs, 1, D), *rg_w)

    pre_w = (w_conv_b[0], vec(b_conv_b[0]), bf(w_q_b[0]), bf(w_k_b[0]), bf(w_v_b[0]), bf(w_if_b[0].T),
             b_if_b[0].reshape(2 * NH_B, 1))
    rec_w = (g_hnorm_b[0].reshape(1, D), vec(skip_b[0]))
    lanes = lambda m: jnp.broadcast_to(m[:, :, None], m.shape + (LANES,))
    pre_p = _mlstm_pre(xm, 0, tp, jnp.zeros((bp, CONV_W - 1, D), F32), jnp.zeros((bp, NH_B, LANES), F32), *pre_w)
    pre_s = _mlstm_pre(xm, rows_p, ts, state_conv_b[0], lanes(state_mlstm_m[0]), *pre_w)
    m_p, m_s = pre_p[6], pre_s[6]
    yb_p, c_p, nn_p = _mlstm(*pre_p[1:6], pre_p[0], z, 0, tp, jnp.zeros((bp, NH_B, DH_B, DH_B), F32),
                             jnp.zeros((bp, NH_B, DH_B), F32), *rec_w)
    yb_s, c_s, nn_s = _mlstm(*pre_s[1:6], pre_s[0], z, rows_p, ts, state_mlstm_c[0], state_mlstm_n[0], *rec_w)

    w_out = bf(w_out_ab[0])
    y = _out_proj((xp, xs), [(ya_p, ya_s), (yb_p, yb_s)], [w_out[:D], w_out[D:]], tm, n_p)
    y = _ffn_dense(y, vec(norm_ffn[0]), bf(w1_dense[0]), bf(w3_dense[0]), bf(w2_dense[0]), tm)

    tile2 = lambda gain: jnp.tile(gain, D_KV // HEAD_DIM).reshape(1, D_KV)
    q, kw, vw, k, v = _in_proj_att(y, vec(norm_mix[1]), bf(w_in_att[0]), tile2(g_qnorm[0]), tile2(g_knorm[0]), tm)
    half = WINDOW // 2
    ncp = tp // CHUNK
    sinks2 = sinks[0].reshape(1, N_HEADS)
    hist_p = (lambda bi, ci: (bi * ncp + jnp.maximum(ci - 2, 0), 0), lambda bi, ci: (bi * ncp + jnp.maximum(ci - 1, 0), 0))
    o_p = _attention(q, kw, vw, kw, vw, hist_p, 0, tp, bp, True, _bucket_table(CHUNK), rel_bias, sinks2)

    def widen(cache):
        wide = jnp.broadcast_to(cache[:, :, :, None, :], (bs, WINDOW, N_KV, GROUP, HEAD_DIM))
        return wide.reshape(bs * WINDOW, D).astype(BF16)

    ck, cv = widen(cache_swa_k[0]), widen(cache_swa_v[0])
    hist_s = (lambda bi, ci: (2 * bi, 0), lambda bi, ci: (2 * bi + 1, 0))
    o_s = _attention(q, kw, vw, ck, cv, hist_s, rows_p, ts, bs, False, _bucket_table(ts), rel_bias, sinks2)
    y = _out_proj(y, [(o_p, o_s)], [bf(w_out_att[0])], tm, n_p, tiled_out=True)
    yp, ys = _moe(y, vec(norm_ffn[1]), w_router[0], bf(w1_moe[0]), bf(w3_moe[0]), bf(w2_moe[0]), tm, n_p, n_s)

    def tail(a, rows, b, t, keep):
        if rows == 0:
            return jnp.stack([a[(i + 1) * t - keep:(i + 1) * t] for i in range(b)])
        return a[rows:rows + b * t].reshape(b, t, -1)[:, t - keep:]

    kv4 = lambda a, b: a.reshape(b, -1, N_KV, HEAD_DIM)
    one = lambda a: a[None]
    k_s = jnp.concatenate([cache_swa_k[0][:, ts:], kv4(k[rows_p:], bs)], axis=1)
    v_s = jnp.concatenate([cache_swa_v[0][:, ts:], kv4(v[rows_p:], bs)], axis=1)
    return (yp.reshape(bp, tp, D), ys.reshape(bs, ts, D),
            one(tail(xa, 0, bp, tp, CONV_W - 1)), one(hl_p.reshape(bp, D)), one(tail(xm, 0, bp, tp, CONV_W - 1)),
            one(c_p), one(nn_p), one(m_p[:, :, 0]),
            one(kv4(tail(k, 0, bp, tp, WINDOW), bp)), one(kv4(tail(v, 0, bp, tp, WINDOW), bp)),
            one(tail(xa, rows_p, bs, ts, CONV_W - 1)), one(hl_s.reshape(bs, D)),
            one(tail(xm, rows_p, bs, ts, CONV_W - 1)), one(c_s), one(nn_s), one(m_s[:, :, 0]),
            one(k_s), one(v_s))
```

```python
import functools
import math

import jax
import jax.numpy as jnp
from jax import lax
from jax.experimental import pallas as pl
from jax.experimental.pallas import tpu as pltpu

F32 = jnp.float32
BF16 = jnp.bfloat16

D = 1024
CHUNK = 64
CONV_W = 4
NB_A = 8
BS_A = D // NB_A
RG_C = 8.0
NH_B = 4
DH_B = D // NH_B
N_HEADS = 16
HEAD_DIM = D // N_HEADS
N_KV = 4
GROUP = N_HEADS // N_KV
D_KV = N_KV * HEAD_DIM
WINDOW = 128
NUM_BUCKETS = 32
MAX_DISTANCE = 128
N_EXPERTS = 8
EPS = 1e-6
LANES = 128
SUB = D // LANES
NEG_INF = float("-inf")
MIN_NORMAL = 1.1754944e-38

TOKEN_TILE = 512
EXPERT_TILE = 512
FF_SPLIT_DENSE = 3
FF_SPLIT_MOE = 4
SEQ_BLOCK = 256
ATTN_BLOCK = 128
VMEM_LIMIT = 56 * 1024 * 1024


def _params(sem):
    return pltpu.CompilerParams(dimension_semantics=sem, vmem_limit_bytes=VMEM_LIMIT)


def _full(shape):
    return pl.BlockSpec(shape, lambda *_: (0,) * len(shape))


def _resident(shape):
    return pl.BlockSpec(shape, lambda *_: (0,) * len(shape), pipeline_mode=pl.Buffered(1))


def _rows(tm, c):
    return pl.BlockSpec((tm, c), lambda i: (i, 0))


def _pair(tm, c, n_p):
    return [pl.BlockSpec((tm, c), lambda i: (jnp.minimum(i, n_p - 1), 0)),
            pl.BlockSpec((tm, c), lambda i: (jnp.maximum(i - n_p, 0), 0))]


def _rms(x, g):
    ms = jnp.mean(x * x, axis=-1, keepdims=True)
    return x * lax.rsqrt(ms + EPS) * g


def _silu(x):
    return x * jax.nn.sigmoid(x)


def _sigmoid_tanh(x):
    return 0.5 * jnp.tanh(0.5 * x) + 0.5


def _softplus(x):
    return jnp.maximum(x, 0.0) + jnp.log1p(jnp.exp(-jnp.abs(x)))


def _gelu_tanh(x):
    c = math.sqrt(2.0 / math.pi)
    return x * (0.5 * (1.0 + jnp.tanh(c * (x + 0.044715 * (x * x * x)))))


def _dot(a, b):
    return jnp.dot(a, b, preferred_element_type=F32)


def _dot_nt(a, b):
    return lax.dot_general(a, b, (((1,), (1,)), ((), ())), preferred_element_type=F32)


def _dot_tn(a, b):
    return lax.dot_general(a, b, (((0,), (0,)), ((), ())), preferred_element_type=F32)


def _in_proj_ab_kernel(n_p, xp_ref, xs_ref, g_ref, w_ref, *o_refs):
    i = pl.program_id(0)
    x = jnp.where(i < n_p, xp_ref[...], xs_ref[...])
    xn = _rms(x, g_ref[...]).astype(BF16)
    for c, o_ref in enumerate(o_refs):
        o_ref[...] = _dot(xn, w_ref[:, c * D:(c + 1) * D])


def _in_proj_ab(xp, xs, g, w, tm):
    n_p, n_s = xp.shape[0] // tm, xs.shape[0] // tm
    n = xp.shape[0] + xs.shape[0]
    n_out = w.shape[1] // D
    return pl.pallas_call(
        functools.partial(_in_proj_ab_kernel, n_p),
        grid=(n_p + n_s,),
        in_specs=_pair(tm, D, n_p) + [_full((1, D)), _resident(w.shape)],
        out_specs=[_rows(tm, D)] * n_out,
        out_shape=[jax.ShapeDtypeStruct((n, D), F32)] * n_out,
        compiler_params=_params(("parallel",)),
        name="in_proj_ab",
    )(xp, xs, g, w)


def _tile_rows_load(ref):
    tm = ref.shape[0] // SUB
    return jnp.concatenate([ref[pl.ds(s, tm, stride=SUB), :] for s in range(SUB)], axis=1)


def _tile_rows_store(ref, val):
    tm = val.shape[0]
    for s in range(SUB):
        ref[pl.ds(s, tm, stride=SUB), :] = val[:, s * LANES:(s + 1) * LANES]


def _tile_spec(tm):
    return pl.BlockSpec((tm * SUB, LANES), lambda i: (i, 0))


def _out_proj_kernel(n_p, n_res, n_a, tiled_out, *refs):
    i = pl.program_id(0)
    res_refs = refs[:n_res]
    a_refs = refs[n_res:n_res + 2 * n_a]
    w_refs = refs[n_res + 2 * n_a:n_res + 3 * n_a]
    o_ref = refs[-1]
    if n_res == 2:
        acc = jnp.where(i < n_p, res_refs[0][...], res_refs[1][...])
    else:
        acc = res_refs[0][...]
    for k in range(n_a):
        a = jnp.where(i < n_p, a_refs[2 * k][...], a_refs[2 * k + 1][...])
        acc = acc + _dot(a, w_refs[k][...])
    if tiled_out:
        _tile_rows_store(o_ref, acc)
    else:
        o_ref[...] = acc


def _out_proj(res, a_pairs, ws, tm, n_p, tiled_out=False):
    res = res if isinstance(res, tuple) else (res,)
    n = sum(a.shape[0] for a in a_pairs[0])
    res_specs = _pair(tm, D, n_p) if len(res) == 2 else [_rows(tm, D)]
    a_specs = []
    for a in a_pairs:
        a_specs += _pair(tm, a[0].shape[1], n_p)
    return pl.pallas_call(
        functools.partial(_out_proj_kernel, n_p, len(res), len(a_pairs), tiled_out),
        grid=(n // tm,),
        in_specs=res_specs + a_specs + [_resident(w.shape) for w in ws],
        out_specs=_tile_spec(tm) if tiled_out else _rows(tm, D),
        out_shape=jax.ShapeDtypeStruct((n * SUB, LANES) if tiled_out else (n, D), F32),
        compiler_params=_params(("parallel",)),
        name="out_proj",
    )(*res, *[x for a in a_pairs for x in a], *ws)


def _ffn_dense_kernel(x_ref, g_ref, w1_ref, w3_ref, w2_ref, o_ref):
    x = x_ref[...]
    xn = _rms(x, g_ref[...]).astype(BF16)
    step = w1_ref.shape[1] // FF_SPLIT_DENSE
    acc = x
    for c in range(FF_SPLIT_DENSE):
        sl = slice(c * step, (c + 1) * step)
        a = (_silu(_dot(xn, w1_ref[:, sl])) * _dot(xn, w3_ref[:, sl])).astype(BF16)
        acc = acc + _dot(a, w2_ref[sl, :])
    o_ref[...] = acc


def _ffn_dense(x, g, w1, w3, w2, tm):
    n = x.shape[0]
    return pl.pallas_call(
        _ffn_dense_kernel,
        grid=(n // tm,),
        in_specs=[_rows(tm, D), _full((1, D)), _resident(w1.shape), _resident(w3.shape), _resident(w2.shape)],
        out_specs=_rows(tm, D),
        out_shape=jax.ShapeDtypeStruct((n, D), F32),
        compiler_params=_params(("parallel",)),
        name="ffn_dense",
    )(x, g, w1, w3, w2)


def _in_proj_att_kernel(x_ref, g_ref, w_ref, gq_ref, gk_ref, q_ref, kw_ref, vw_ref, k_ref, v_ref):
    xn = _rms(x_ref[...], g_ref[...]).astype(BF16)
    cw = D_KV
    r = lax.broadcasted_iota(jnp.int32, (cw, cw), 0) // HEAD_DIM
    c = lax.broadcasted_iota(jnp.int32, (cw, cw), 1) // HEAD_DIM
    group_ones = (r == c).astype(BF16)
    er = lax.broadcasted_iota(jnp.int32, (D_KV, D), 0)
    ec = lax.broadcasted_iota(jnp.int32, (D_KV, D), 1)
    widen = ((er // HEAD_DIM == ec // (GROUP * HEAD_DIM)) & (er % HEAD_DIM == ec % HEAD_DIM)).astype(BF16)

    def head_norm(y, gain):
        ms = _dot((y * y).astype(BF16), group_ones) * (1.0 / HEAD_DIM)
        return y * lax.rsqrt(ms + EPS) * gain

    for b in range(D // cw):
        y = _dot(xn, w_ref[:, b * cw:(b + 1) * cw])
        q_ref[:, b * cw:(b + 1) * cw] = (head_norm(y, gq_ref[...]) * (HEAD_DIM ** -0.5)).astype(BF16)
    k = head_norm(_dot(xn, w_ref[:, D:D + D_KV]), gk_ref[...])
    v = _dot(xn, w_ref[:, D + D_KV:D + 2 * D_KV])
    k_ref[...] = k
    v_ref[...] = v
    kw_ref[...] = _dot(k.astype(BF16), widen).astype(BF16)
    vw_ref[...] = _dot(v.astype(BF16), widen).astype(BF16)


def _in_proj_att(x, g, w, gq, gk, tm):
    n = x.shape[0]
    return pl.pallas_call(
        _in_proj_att_kernel,
        grid=(n // tm,),
        in_specs=[_rows(tm, D), _full((1, D)), _resident(w.shape), _full((1, D_KV)), _full((1, D_KV))],
        out_specs=[_rows(tm, D)] * 3 + [_rows(tm, D_KV)] * 2,
        out_shape=[jax.ShapeDtypeStruct((n, D), BF16)] * 3 + [jax.ShapeDtypeStruct((n, D_KV), F32)] * 2,
        compiler_params=_params(("parallel",)),
        name="in_proj_att",
    )(x, g, w, gq, gk)


def _causal_conv(x, xbuf, conv0_ref, wc_ref, bc_ref, first):
    tb = x.shape[0]

    @pl.when(first)
    def _():
        xbuf[5:8, :] = conv0_ref[0]

    xbuf[8:8 + tb, :] = x
    y = bc_ref[...] + xbuf[5:5 + tb, :] * wc_ref[0:1, :]
    for j in range(1, CONV_W):
        y = y + xbuf[5 + j:5 + j + tb, :] * wc_ref[j:j + 1, :]
    xbuf[5:8, :] = xbuf[5 + tb:8 + tb, :]
    return y


def _seq_specs(bsz, seq, tb, row0):
    nc = seq // tb
    off = row0 // tb
    return nc, (lambda bi, ci: (off + bi * nc + ci, 0)), (lambda bi, ci: (bi * nc + ci, 0))


def _scan_by_doubling(a, u, pos, axis, length):
    d = 1
    while d < length:
        keep = pos >= d
        u = jnp.where(keep, a * pltpu.roll(u, d, axis) + u, u)
        a = jnp.where(keep, a * pltpu.roll(a, d, axis), a)
        d *= 2
    return a, u


def _linear_scan(a, u, h0, row):
    a, u = _scan_by_doubling(a, u, row, 0, a.shape[0])
    return a * h0 + u


def _rglru_kernel(xa_ref, ga_ref, conv0_ref, h0_ref, wc_ref, bc_ref, wa_ref, ba_ref, wx_ref, bx_ref, lam_ref,
                  ya_ref, hl_ref, xbuf):
    first = pl.program_id(1) == 0
    tb = xa_ref.shape[0]

    @pl.when(first)
    def _():
        hl_ref[0] = h0_ref[0]

    y = _causal_conv(xa_ref[...], xbuf, conv0_ref, wc_ref, bc_ref, first)
    row = lax.broadcasted_iota(jnp.int32, (tb, BS_A), 0)
    for n in range(NB_A):
        sl = slice(n * BS_A, (n + 1) * BS_A)
        yn = y[:, sl]
        yb = yn.astype(BF16)
        r = _sigmoid_tanh(_dot(yb, wa_ref[n]) + ba_ref[:, sl])
        ig = _sigmoid_tanh(_dot(yb, wx_ref[n]) + bx_ref[:, sl])
        log_a = (-RG_C) * r * _softplus(-lam_ref[:, sl])
        a = jnp.exp(log_a)
        th = jnp.tanh(log_a)
        gain2 = -2.0 * th / (1.0 - th)
        u = yn * ig * (gain2 * lax.rsqrt(jnp.maximum(gain2, MIN_NORMAL)))
        h = _linear_scan(a, u, hl_ref[0, :, sl], row)
        hl_ref[0, :, sl] = h[tb - 1:tb, :]
        ya_ref[:, sl] = (h * _gelu_tanh(ga_ref[:, sl])).astype(BF16)


def _rglru(xa, ga, row0, seq, conv0, h0, wc, bc, wa, ba, wx, bx, lam):
    bsz = conv0.shape[0]
    tb = min(seq, SEQ_BLOCK)
    nc, in_map, out_map = _seq_specs(bsz, seq, tb, row0)
    state3 = lambda bi, ci: (bi, 0, 0)
    return pl.pallas_call(
        _rglru_kernel,
        grid=(bsz, nc),
        in_specs=[pl.BlockSpec((tb, D), in_map), pl.BlockSpec((tb, D), in_map),
                  pl.BlockSpec((1, CONV_W - 1, D), state3), pl.BlockSpec((1, 1, D), state3),
                  _full(wc.shape), _full(bc.shape), _full(wa.shape), _full(ba.shape), _full(wx.shape),
                  _full(bx.shape), _full(lam.shape)],
        out_specs=[pl.BlockSpec((tb, D), out_map), pl.BlockSpec((1, 1, D), state3)],
        out_shape=[jax.ShapeDtypeStruct((bsz * seq, D), BF16), jax.ShapeDtypeStruct((bsz, 1, D), F32)],
        scratch_shapes=[pltpu.VMEM((tb + 8, D), F32)],
        compiler_params=_params(("parallel", "arbitrary")),
        name="rglru",
    )(xa, ga, conv0, h0, wc, bc, wa, ba, wx, bx, lam)


def _mlstm_pre_kernel(chunk, xm_ref, conv0_ref, m0_ref, wc_ref, bc_ref, wq_ref, wk_ref, wv_ref, wif_ref, bif_ref,
                      xc_ref, q_ref, k_ref, v_ref, w_ref, col_ref, m_ref, xbuf):
    first = pl.program_id(1) == 0
    tb = xm_ref.shape[0]
    L = chunk

    @pl.when(first)
    def _():
        m_ref[...] = m0_ref[...]

    x = xm_ref[...]
    xc = _silu(_causal_conv(x, xbuf, conv0_ref, wc_ref, bc_ref, first))
    xc_ref[...] = xc
    gates = bif_ref[...]
    for h in range(NH_B):
        sl = slice(h * DH_B, (h + 1) * DH_B)
        xch = xc[:, sl].astype(BF16)
        q = _dot(xch, wq_ref[h]).astype(BF16)
        k = _dot(xch, wk_ref[h]).astype(BF16)
        v = _dot(x[:, sl].astype(BF16), wv_ref[h]).astype(BF16)
        q_ref[:, sl] = q
        k_ref[:, sl] = k
        v_ref[:, sl] = v
        gates = (gates + _dot_nt(wif_ref[:, sl], q) + _dot_nt(wif_ref[:, D + h * DH_B:D + (h + 1) * DH_B], k)
                 + _dot_nt(wif_ref[:, 2 * D + h * DH_B:2 * D + (h + 1) * DH_B], v))
    row = lax.broadcasted_iota(jnp.int32, gates.shape, 0)
    gates = jnp.where(row < NH_B, gates, -_softplus(-gates))

    tt = lax.broadcasted_iota(jnp.int32, (L, L), 0)
    ss = lax.broadcasted_iota(jnp.int32, (L, L), 1)
    eye = tt == ss
    causal = ss <= tt
    lane = lax.broadcasted_iota(jnp.int32, (L, LANES), 1)

    def to_col(row_vec):
        return jnp.sum(jnp.where(eye, row_vec, 0.0), axis=1, keepdims=True)

    m_prev = [m_ref[0, h:h + 1, 0:1] for h in range(NH_B)]
    for j in range(tb // L):
        table = jnp.zeros((L, LANES), F32)
        for h in range(NH_B):
            i_row = gates[h:h + 1, j * L:(j + 1) * L]
            f_row = gates[NH_B + h:NH_B + h + 1, j * L:(j + 1) * L]
            f_col = to_col(f_row)
            b_col = jnp.sum(jnp.where(causal, f_row, 0.0), axis=1, keepdims=True)
            b_row = jnp.sum(jnp.where(tt <= ss, f_col, 0.0), axis=0, keepdims=True)
            w_row = i_row - b_row
            w_col = to_col(i_row) - b_col
            big_m = jnp.maximum(m_prev[h], jnp.max(jnp.where(causal, w_row, NEG_INF), axis=1, keepdims=True))
            m_last = big_m[L - 1:L, :]
            w_ref[j, h:h + 1, :] = w_row
            for c, col in enumerate((big_m, jnp.exp(m_prev[h] - big_m), jnp.exp(-(b_col + big_m)),
                                     jnp.exp(w_col - m_last))):
                table = jnp.where(lane == c * NH_B + h, col, table)
            m_prev[h] = b_col[L - 1:L, :] + m_last
        col_ref[j * L:(j + 1) * L, :] = table
    for h in range(NH_B):
        m_ref[0, h:h + 1, :] = jnp.broadcast_to(m_prev[h], (1, LANES))


def _mlstm_pre(xm, row0, seq, conv0, m0, wc, bc, wq, wk, wv, wif_t, bif):
    bsz = conv0.shape[0]
    tb = min(seq, SEQ_BLOCK)
    chunk = tb
    nc, in_map, out_map = _seq_specs(bsz, seq, tb, row0)
    n = bsz * seq
    st3 = lambda bi, ci: (bi, 0, 0)
    return pl.pallas_call(
        functools.partial(_mlstm_pre_kernel, chunk),
        grid=(bsz, nc),
        in_specs=[pl.BlockSpec((tb, D), in_map), pl.BlockSpec((1, CONV_W - 1, D), st3),
                  pl.BlockSpec((1, NH_B, LANES), st3),
                  _full(wc.shape), _full(bc.shape), _full(wq.shape), _full(wk.shape), _full(wv.shape),
                  _full(wif_t.shape), _full(bif.shape)],
        out_specs=[pl.BlockSpec((tb, D), out_map)] * 4
                  + [pl.BlockSpec((tb // chunk, NH_B, chunk), lambda bi, ci: (bi * nc + ci, 0, 0)),
                     pl.BlockSpec((tb, LANES), out_map), pl.BlockSpec((1, NH_B, LANES), st3)],
        out_shape=[jax.ShapeDtypeStruct((n, D), F32)] + [jax.ShapeDtypeStruct((n, D), BF16)] * 3
                  + [jax.ShapeDtypeStruct((n // chunk, NH_B, chunk), F32), jax.ShapeDtypeStruct((n, LANES), F32),
                     jax.ShapeDtypeStruct(m0.shape, F32)],
        scratch_shapes=[pltpu.VMEM((tb + 8, D), F32)],
        compiler_params=_params(("parallel", "arbitrary")),
        name="mlstm_pre",
    )(xm, conv0, m0, wc, bc, wq, wk, wv, wif_t, bif)


def _mlstm_kernel(chunk, q_ref, k_ref, v_ref, w_ref, col_ref, xc_ref, z_ref, c0_ref, n0_ref, gh_ref, skip_ref,
                  yb_ref, c_ref, n_ref):
    tb = q_ref.shape[0]
    L = chunk
    scale = DH_B ** -0.5

    @pl.when(pl.program_id(1) == 0)
    def _():
        c_ref[...] = c0_ref[...]
        n_ref[...] = n0_ref[...]

    causal = lax.broadcasted_iota(jnp.int32, (L, L), 1) <= lax.broadcasted_iota(jnp.int32, (L, L), 0)
    heads = range(NH_B)
    hsl = [slice(h * DH_B, (h + 1) * DH_B) for h in heads]
    for j in range(tb // L):
        rows = slice(j * L, (j + 1) * L)
        cols = col_ref[rows, :]
        col = lambda c, h: cols[:, c * NH_B + h:c * NH_B + h + 1]
        q = [q_ref[rows, hsl[h]] for h in heads]
        k = [k_ref[rows, hsl[h]] for h in heads]
        v = [v_ref[rows, hsl[h]] for h in heads]
        s = [_dot_nt(q[h], k[h]) for h in heads]
        qc = [_dot(q[h], c_ref[0, h].astype(BF16)) for h in heads]
        qn = [_dot_nt(q[h], jnp.broadcast_to(n_ref[0, h:h + 1, :], (8, DH_B)).astype(BF16))[:, 0:1] for h in heads]
        p = [(s[h] * scale * jnp.where(causal, jnp.exp(w_ref[j, h:h + 1, :] - col(0, h)), 0.0)).astype(BF16)
             for h in heads]
        pv = [_dot(p[h], v[h]) for h in heads]
        psum = [_dot(p[h], jnp.ones((L, LANES), BF16))[:, 0:1] for h in heads]
        for h in heads:
            sc = col(1, h)
            keep = sc[L - 1:L, :]
            kd = k[h].astype(F32) * (col(3, h) * scale)
            c_ref[0, h] = keep * c_ref[0, h] + _dot_tn(kd.astype(BF16), v[h])
            n_ref[0, h:h + 1, :] = keep * n_ref[0, h:h + 1, :] + jnp.sum(kd, axis=0, keepdims=True)
            num = pv[h] + sc * qc[h]
            den = psum[h] + sc * qn[h]
            hs = num / jnp.maximum(jnp.abs(den), col(2, h))
            mu = jnp.mean(hs, axis=1, keepdims=True)
            dev = hs - mu
            var = jnp.mean(dev * dev, axis=1, keepdims=True)
            hn = dev * lax.rsqrt(var + EPS) * gh_ref[:, hsl[h]]
            out = (hn + skip_ref[:, hsl[h]] * xc_ref[rows, hsl[h]]) * _silu(z_ref[rows, hsl[h]])
            yb_ref[rows, hsl[h]] = out.astype(BF16)


def _mlstm(q, k, v, w, col, xc, z, row0, seq, c0, n0, gh, skip):
    bsz = c0.shape[0]
    tb = min(seq, SEQ_BLOCK)
    chunk = tb
    nc, z_map, own_map = _seq_specs(bsz, seq, tb, row0)
    st4 = lambda bi, ci: (bi, 0, 0, 0)
    st3 = lambda bi, ci: (bi, 0, 0)
    return pl.pallas_call(
        functools.partial(_mlstm_kernel, chunk),
        grid=(bsz, nc),
        in_specs=[pl.BlockSpec((tb, D), own_map)] * 3
                 + [pl.BlockSpec((tb // chunk, NH_B, chunk), lambda bi, ci: (bi * nc + ci, 0, 0)),
                    pl.BlockSpec((tb, LANES), own_map),
                    pl.BlockSpec((tb, D), own_map), pl.BlockSpec((tb, D), z_map),
                    pl.BlockSpec((1, NH_B, DH_B, DH_B), st4), pl.BlockSpec((1, NH_B, DH_B), st3),
                    _full(gh.shape), _full(skip.shape)],
        out_specs=[pl.BlockSpec((tb, D), own_map), pl.BlockSpec((1, NH_B, DH_B, DH_B), st4),
                   pl.BlockSpec((1, NH_B, DH_B), st3)],
        out_shape=[jax.ShapeDtypeStruct((bsz * seq, D), BF16), jax.ShapeDtypeStruct(c0.shape, F32),
                   jax.ShapeDtypeStruct(n0.shape, F32)],
        compiler_params=_params(("parallel", "arbitrary")),
        name="mlstm",
    )(q, k, v, w, col, xc, z, c0, n0, gh, skip)


def _attn_kernel(masked, L, q_ref, kh_ref, ko_ref, vh_ref, vo_ref, bucket_ref, relb_ref, sink_ref,
                 o_ref, bias_s, s_scr, p_scr):
    nk = WINDOW + L
    ci = pl.program_id(1)

    @pl.when((pl.program_id(0) == 0) & (ci == 0))
    def _():
        bucket = bucket_ref[...]
        for h in range(N_HEADS):
            acc = jnp.zeros((L, nk), F32)
            for b in range(NUM_BUCKETS):
                acc = jnp.where(bucket == b, relb_ref[b, h], acc)
            bias_s[h * L:(h + 1) * L, :] = acc

    kall = jnp.concatenate([kh_ref[...], ko_ref[...]], axis=0)
    vall = jnp.concatenate([vh_ref[...], vo_ref[...]], axis=0)
    gw = GROUP * HEAD_DIM
    slot = lax.broadcasted_iota(jnp.int32, (L, gw), 1) // HEAD_DIM
    kk = lax.broadcasted_iota(jnp.int32, (1, nk), 1)
    gl = GROUP * L
    n_sub = q_ref.shape[0] // L
    hl = N_HEADS * L
    for u in range(n_sub):
        kcat = kall[u * L:u * L + nk, :]
        for g in range(N_KV):
            gsl = slice(g * gw, (g + 1) * gw)
            qg = q_ref[u * L:(u + 1) * L, gsl]
            zero = jnp.zeros_like(qg)
            qs = jnp.concatenate([jnp.where(slot == j, qg, zero) for j in range(GROUP)], axis=0)
            s_scr[u * hl + g * gl:u * hl + (g + 1) * gl, :] = _dot_nt(qs, kcat[:, gsl])
    for u in range(n_sub):
        if masked:
            valid = kk >= WINDOW - u * L - ci * (n_sub * L)
        for h in range(N_HEADS):
            rows = slice(u * hl + h * L, u * hl + (h + 1) * L)
            s = s_scr[rows, :] + bias_s[h * L:(h + 1) * L, :]
            if masked:
                s = jnp.where(valid, s, NEG_INF)
            sink = sink_ref[0, h]
            mx = jnp.maximum(jnp.max(s, axis=1, keepdims=True), sink)
            e = jnp.exp(s - mx)
            den = jnp.sum(e, axis=1, keepdims=True) + jnp.exp(sink - mx)
            p_scr[rows, :] = (e / den).astype(BF16)
    for u in range(n_sub):
        vcat = vall[u * L:u * L + nk, :]
        for g in range(N_KV):
            gsl = slice(g * gw, (g + 1) * gw)
            ow = _dot(p_scr[u * hl + g * gl:u * hl + (g + 1) * gl, :], vcat[:, gsl])
            og = jnp.where(slot == 0, ow[0:L, :], 0.0)
            for j in range(1, GROUP):
                og = jnp.where(slot == j, ow[j * L:(j + 1) * L, :], og)
            o_ref[u * L:(u + 1) * L, gsl] = og.astype(BF16)


def _attention(q, k_own, v_own, k_hist, v_hist, hist_map, row0, seq, bsz, masked, bucket, rel_bias, sinks):
    L = min(seq, CHUNK)
    lb = min(seq, ATTN_BLOCK)
    nc, in_map, out_map = _seq_specs(bsz, seq, lb, row0)
    n_sub = lb // L
    nk = WINDOW + L
    smem = functools.partial(pl.BlockSpec, memory_space=pltpu.SMEM)
    return pl.pallas_call(
        functools.partial(_attn_kernel, masked, L),
        grid=(bsz, nc),
        in_specs=[pl.BlockSpec((lb, D), in_map),
                  pl.BlockSpec((WINDOW, D), hist_map), pl.BlockSpec((lb, D), in_map),
                  pl.BlockSpec((WINDOW, D), hist_map), pl.BlockSpec((lb, D), in_map),
                  _full(bucket.shape), smem(), smem()],
        out_specs=pl.BlockSpec((lb, D), out_map),
        out_shape=jax.ShapeDtypeStruct((bsz * seq, D), BF16),
        scratch_shapes=[pltpu.VMEM((N_HEADS * L, nk), F32), pltpu.VMEM((n_sub * N_HEADS * L, nk), F32),
                        pltpu.VMEM((n_sub * N_HEADS * L, nk), BF16)],
        compiler_params=_params(("arbitrary", "arbitrary")),
        name="swa",
    )(q, k_hist, k_own, v_hist, v_own, bucket, rel_bias, sinks)


def _router_kernel(x_ref, g_ref, wr_ref, info_ref, cnt_ref):
    tm = x_ref.shape[0] // SUB

    @pl.when(pl.program_id(0) == 0)
    def _():
        cnt_ref[...] = jnp.zeros_like(cnt_ref)

    xn = _rms(_tile_rows_load(x_ref), g_ref[...])
    hi = xn.astype(BF16)
    lo = (xn - hi.astype(F32)).astype(BF16)
    w = wr_ref[...]
    whi = w.astype(BF16)
    wlo = (w - whi.astype(F32)).astype(BF16)
    logits = _dot(hi, whi) + _dot(hi, wlo) + _dot(lo, whi)
    lane = lax.broadcasted_iota(jnp.int32, (tm, LANES), 1)
    logits = jnp.where(lane < N_EXPERTS, logits, NEG_INF)
    m1 = jnp.max(logits, axis=1, keepdims=True)
    i1 = jnp.min(jnp.where(logits == m1, lane, LANES), axis=1, keepdims=True)
    rest = jnp.where(lane == i1, NEG_INF, logits)
    m2 = jnp.max(rest, axis=1, keepdims=True)
    i2 = jnp.min(jnp.where(rest == m2, lane, LANES), axis=1, keepdims=True)
    e2 = jnp.exp(m2 - m1)
    g1 = 1.0 / (1.0 + e2)
    g2 = e2 / (1.0 + e2)
    sel = (lane == i1) | (lane == i2)
    tri = (lax.broadcasted_iota(jnp.int32, (tm, tm), 0) > lax.broadcasted_iota(jnp.int32, (tm, tm), 1)).astype(BF16)
    rank = cnt_ref[...] + _dot(tri, sel.astype(BF16))
    r1 = jnp.sum(jnp.where(lane == i1, rank, 0.0), axis=1, keepdims=True)
    r2 = jnp.sum(jnp.where(lane == i2, rank, 0.0), axis=1, keepdims=True)
    cnt_ref[...] = cnt_ref[...] + jnp.sum(sel.astype(F32), axis=0, keepdims=True)
    info = jnp.where(lane == 0, i1.astype(F32), 0.0)
    info = jnp.where(lane == 1, i2.astype(F32), info)
    info = jnp.where(lane == 2, g1, info)
    info = jnp.where(lane == 3, g2, info)
    info = jnp.where(lane == 4, r1, info)
    info = jnp.where(lane == 5, r2, info)
    info_ref[...] = info


def _router(x, g, wr, tm):
    n = x.shape[0] // SUB
    return pl.pallas_call(
        _router_kernel,
        grid=(n // tm,),
        in_specs=[_tile_spec(tm), _full((1, D)), _full(wr.shape)],
        out_specs=[_rows(tm, LANES), _full((1, LANES))],
        out_shape=[jax.ShapeDtypeStruct((n, LANES), F32), jax.ShapeDtypeStruct((1, LANES), F32)],
        compiler_params=_params(("arbitrary",)),
        name="moe_router",
    )(x, g, wr)


def _token_tile(ref, t):
    return ref.at[pl.ds(pl.multiple_of(t * SUB, SUB), SUB)]


def _moe_scatter_kernel(te, zpos_ref, dest_ref, x_ref, xs_hbm, zero_s, sem):
    tm = x_ref.shape[0] // SUB
    span = te * SUB

    def rows_of(tile_start):
        return xs_hbm.at[pl.ds(pl.multiple_of(tile_start * SUB, span), span)]

    @pl.when(pl.program_id(0) == 0)
    def _():
        zero_s[...] = jnp.zeros_like(zero_s)
        for e in range(N_EXPERTS):
            pltpu.make_async_copy(zero_s, rows_of(zpos_ref[e]), sem).start()
        for e in range(N_EXPERTS):
            pltpu.make_async_copy(zero_s, rows_of(zpos_ref[e]), sem).wait()

        def clear_tail(t, carry):
            tail = pltpu.make_async_copy(zero_s, rows_of(t * te), sem)
            tail.start()
            tail.wait()
            return carry

        lax.fori_loop(zpos_ref[N_EXPERTS] // te, xs_hbm.shape[0] // span, clear_tail, 0)

    def copy(r, s):
        return pltpu.make_async_copy(_token_tile(x_ref, r), _token_tile(xs_hbm, dest_ref[s * tm + r]), sem)

    def start(r, carry):
        for s in range(2):
            copy(r, s).start(priority=s)
        return carry

    def wait(r, carry):
        for s in range(2):
            copy(r, s).wait()
        return carry

    lax.fori_loop(0, tm, start, 0)
    lax.fori_loop(0, tm, wait, 0)


def _moe_scatter(x, dest, zpos, n_rows, tm, te):
    n = x.shape[0] // SUB
    return pl.pallas_call(
        functools.partial(_moe_scatter_kernel, te),
        grid_spec=pltpu.PrefetchScalarGridSpec(
            num_scalar_prefetch=1,
            grid=(n // tm,),
            in_specs=[pl.BlockSpec((2 * tm,), lambda i, zp: (i,), memory_space=pltpu.SMEM),
                      pl.BlockSpec((tm * SUB, LANES), lambda i, zp: (i, 0))],
            out_specs=pl.BlockSpec(memory_space=pl.ANY),
            scratch_shapes=[pltpu.VMEM((te * SUB, LANES), F32), pltpu.SemaphoreType.DMA(())],
        ),
        out_shape=jax.ShapeDtypeStruct((n_rows * SUB, LANES), F32),
        compiler_params=_params(("arbitrary",)),
        name="moe_scatter",
    )(zpos, dest, x)


def _moe_ffn_kernel(n_ff, te, pe_ref, na_ref, xs_ref, g_ref, w1_ref, w3_ref, w2_ref, ys_ref, xn_s, acc_s):
    i, j = pl.program_id(0), pl.program_id(1)
    span = te * SUB
    for s in range(2):
        active = s < na_ref[i]
        rows = slice(s * te, (s + 1) * te)

        @pl.when(active & (j == 0))
        def _():
            xn_s[rows, :] = _rms(_tile_rows_load(xs_ref.at[pl.ds(s * span, span)]), g_ref[...]).astype(BF16)
            acc_s[rows, :] = jnp.zeros((te, D), F32)

        @pl.when(active)
        def _():
            xn = xn_s[rows, :]
            a = (_silu(_dot(xn, w1_ref[0])) * _dot(xn, w3_ref[0])).astype(BF16)
            acc_s[rows, :] += _dot(a, w2_ref[0])

        @pl.when(j == n_ff - 1)
        def _():
            _tile_rows_store(ys_ref.at[pl.ds(s * span, span)], jnp.where(active, acc_s[rows, :], 0.0))


def _moe_ffn(xs, g, w1, w3, w2, pair_expert, pair_active, te):
    n_pairs = pair_expert.shape[0]
    d_ff = w1.shape[2]
    n_ff = FF_SPLIT_MOE
    tf = d_ff // n_ff

    def ff(i, j, na_r):
        return jnp.where(na_r[i] > 0, j, n_ff - 1)

    rows = pl.BlockSpec((2 * te * SUB, LANES), lambda i, j, pe_r, na_r: (i, 0))
    return pl.pallas_call(
        functools.partial(_moe_ffn_kernel, n_ff, te),
        grid_spec=pltpu.PrefetchScalarGridSpec(
            num_scalar_prefetch=2,
            grid=(n_pairs, n_ff),
            in_specs=[rows,
                      pl.BlockSpec((1, D), lambda i, j, pe_r, na_r: (0, 0)),
                      pl.BlockSpec((1, D, tf), lambda i, j, pe_r, na_r: (pe_r[i], 0, ff(i, j, na_r))),
                      pl.BlockSpec((1, D, tf), lambda i, j, pe_r, na_r: (pe_r[i], 0, ff(i, j, na_r))),
                      pl.BlockSpec((1, tf, D), lambda i, j, pe_r, na_r: (pe_r[i], ff(i, j, na_r), 0))],
            out_specs=rows,
            scratch_shapes=[pltpu.VMEM((2 * te, D), BF16), pltpu.VMEM((2 * te, D), F32)],
        ),
        out_shape=jax.ShapeDtypeStruct((n_pairs * 2 * te * SUB, LANES), F32),
        compiler_params=_params(("arbitrary", "arbitrary")),
        name="moe_ffn",
    )(pair_expert, pair_active, xs, g, w1, w3, w2)


def _moe_combine_kernel(n_p, dest_ref, info_ref, x_ref, ys_hbm, op_ref, os_ref, buf, sem):
    tm = x_ref.shape[0] // SUB
    i = pl.program_id(0)

    def copy(r, s):
        return pltpu.make_async_copy(_token_tile(ys_hbm, dest_ref[s * tm + r]), _token_tile(buf.at[s], r), sem)

    def start(r, carry):
        for s in range(2):
            copy(r, s).start(priority=s)
        return carry

    def wait(r, carry):
        for s in range(2):
            copy(r, s).wait()
        return carry

    lax.fori_loop(0, tm, start, 0)
    lax.fori_loop(0, tm, wait, 0)
    out = (_tile_rows_load(x_ref) + info_ref[:, 2:3] * _tile_rows_load(buf.at[0])
           + info_ref[:, 3:4] * _tile_rows_load(buf.at[1]))

    @pl.when(i < n_p)
    def _():
        op_ref[...] = out

    @pl.when(i >= n_p)
    def _():
        os_ref[...] = out


def _moe_combine(x, info, dest, ys, tm, n_p, n_s):
    out_p, out_s = _pair(tm, D, n_p)
    return pl.pallas_call(
        functools.partial(_moe_combine_kernel, n_p),
        grid=(n_p + n_s,),
        in_specs=[pl.BlockSpec((2 * tm,), lambda i: (i,), memory_space=pltpu.SMEM),
                  _rows(tm, LANES), _tile_spec(tm), pl.BlockSpec(memory_space=pl.ANY)],
        out_specs=[out_p, out_s],
        out_shape=[jax.ShapeDtypeStruct((n_p * tm, D), F32), jax.ShapeDtypeStruct((n_s * tm, D), F32)],
        scratch_shapes=[pltpu.VMEM((2, tm * SUB, LANES), F32), pltpu.SemaphoreType.DMA(())],
        compiler_params=_params(("arbitrary",)),
        name="moe_combine",
    )(dest, info, x, ys)


def _moe(x, g, w_router, w1, w3, w2, tm, n_p, n_s):
    n = x.shape[0] // SUB
    te = min(EXPERT_TILE, tm)
    wr = jnp.zeros((D, LANES), F32).at[:, :N_EXPERTS].set(w_router)
    info, cnt = _router(x, g, wr, tm)
    e1, e2 = info[:, 0].astype(jnp.int32), info[:, 1].astype(jnp.int32)
    r1, r2 = info[:, 4].astype(jnp.int32), info[:, 5].astype(jnp.int32)
    counts = cnt[0, :N_EXPERTS].astype(jnp.int32)
    region = 2 * te
    padded = (counts + region - 1) // region * region
    ends = jnp.cumsum(padded)
    starts = ends - padded
    dest = jnp.stack([(starts[e1] + r1).reshape(n // tm, tm), (starts[e2] + r2).reshape(n // tm, tm)], axis=1)
    dest = dest.reshape(-1)
    n_pairs = (2 * n + N_EXPERTS * (region - 1)) // region
    pair_start = jnp.arange(n_pairs, dtype=jnp.int32) * region
    pair_expert = jnp.minimum(jnp.sum(pair_start[:, None] >= ends[None, :], axis=1), N_EXPERTS - 1).astype(jnp.int32)
    used_end = starts + (counts + te - 1) // te * te
    pair_active = jnp.clip((used_end[pair_expert] - pair_start) // te, 0, 2).astype(jnp.int32)
    last_region = jnp.maximum(ends - region, 0).astype(jnp.int32)
    xs = _moe_scatter(x, dest, jnp.concatenate([last_region, ends[-1:].astype(jnp.int32)]), n_pairs * region, tm,
                      region)
    ys = _moe_ffn(xs, g, w1, w3, w2, pair_expert, pair_active, te)
    return _moe_combine(x, info, dest, ys, tm, n_p, n_s)


def _t5_bucket(rel):
    n = -rel
    half = NUM_BUCKETS // 2
    ret = jnp.where(n < 0, half, 0)
    n = jnp.abs(n)
    max_exact = half // 2
    nf = jnp.maximum(n, 1).astype(F32)
    large = max_exact + (jnp.log(nf / max_exact) / math.log(MAX_DISTANCE / max_exact)
                         * (half - max_exact)).astype(jnp.int32)
    large = jnp.minimum(large, half - 1)
    return ret + jnp.where(n < max_exact, n, large)


def _bucket_table(length):
    kpos = jnp.arange(WINDOW + length) - WINDOW
    return _t5_bucket(kpos[None, :] - jnp.arange(length)[:, None]).astype(jnp.int32)


def kernel(x_prompt, x_sample, state_conv_a, state_rglru_h, state_conv_b, state_mlstm_c, state_mlstm_n, state_mlstm_m, cache_swa_k, cache_swa_v, norm_mix, norm_ffn, w_in_ab, w_conv_a, b_conv_a, w_rg_a, b_rg_a, w_rg_x, b_rg_x, rg_lambda, w_conv_b, b_conv_b, w_q_b, w_k_b, w_v_b, w_if_b, b_if_b, g_hnorm_b, skip_b, w_out_ab, w1_dense, w3_dense, w2_dense, w_in_att, g_qnorm, g_knorm, sinks, w_out_att, rel_bias, w_router, w1_moe, w3_moe, w2_moe):
    bp, tp, _ = x_prompt.shape
    bs, ts, _ = x_sample.shape
    assert norm_mix.shape[0] == 2 and w_in_ab.shape[0] == 1 and w_in_att.shape[0] == 1
    assert tp % CHUNK == 0 and ts <= CHUNK and cache_swa_k.shape[2] == WINDOW
    rows_p, rows_s = bp * tp, bs * ts
    tm = math.gcd(math.gcd(rows_p, rows_s), TOKEN_TILE)
    n_p, n_s = rows_p // tm, rows_s // tm
    xp = x_prompt.reshape(rows_p, D)
    xs = x_sample.reshape(rows_s, D)
    bf = lambda w: w.astype(BF16)
    vec = lambda v: v.reshape(1, -1)

    xa, ga, xm, z = _in_proj_ab(xp, xs, vec(norm_mix[0]), bf(w_in_ab[0]), tm)
    rg_w = (w_conv_a[0], vec(b_conv_a[0]), bf(w_rg_a[0]), vec(b_rg_a[0]), bf(w_rg_x[0]), vec(b_rg_x[0]),
            vec(rg_lambda[0]))
    ya_p, hl_p = _rglru(xa, ga, 0, tp, jnp.zeros((bp, CONV_W - 1, D), F32), jnp.zeros((bp, 1, D), F32), *rg_w)
    ya_s, hl_s = _rglru(xa, ga, rows_p, ts, state_conv_a[0], state_rglru_h[0].reshape(bs, 1, D), *rg_w)

    pre_w = (w_conv_b[0], vec(b_conv_b[0]), bf(w_q_b[0]), bf(w_k_b[0]), bf(w_v_b[0]), bf(w_if_b[0].T),
             b_if_b[0].reshape(2 * NH_B, 1))
    rec_w = (g_hnorm_b[0].reshape(1, D), vec(skip_b[0]))
    lanes = lambda m: jnp.broadcast_to(m[:, :, None], m.shape + (LANES,))
    pre_p = _mlstm_pre(xm, 0, tp, jnp.zeros((bp, CONV_W - 1, D), F32), jnp.zeros((bp, NH_B, LANES), F32), *pre_w)
    pre_s = _mlstm_pre(xm, rows_p, ts, state_conv_b[0], lanes(state_mlstm_m[0]), *pre_w)
    m_p, m_s = pre_p[6], pre_s[6]
    yb_p, c_p, nn_p = _mlstm(*pre_p[1:6], pre_p[0], z, 0, tp, jnp.zeros((bp, NH_B, DH_B, DH_B), F32),
                             jnp.zeros((bp, NH_B, DH_B), F32), *rec_w)
    yb_s, c_s, nn_s = _mlstm(*pre_s[1:6], pre_s[0], z, rows_p, ts, state_mlstm_c[0], state_mlstm_n[0], *rec_w)

    w_out = bf(w_out_ab[0])
    y = _out_proj((xp, xs), [(ya_p, ya_s), (yb_p, yb_s)], [w_out[:D], w_out[D:]], tm, n_p)
    y = _ffn_dense(y, vec(norm_ffn[0]), bf(w1_dense[0]), bf(w3_dense[0]), bf(w2_dense[0]), tm)

    tile2 = lambda gain: jnp.tile(gain, D_KV // HEAD_DIM).reshape(1, D_KV)
    q, kw, vw, k, v = _in_proj_att(y, vec(norm_mix[1]), bf(w_in_att[0]), tile2(g_qnorm[0]), tile2(g_knorm[0]), tm)
    sinks2 = sinks[0].reshape(1, N_HEADS)
    wpb = min(tp, ATTN_BLOCK) // WINDOW
    assert wpb >= 1 and tp % (wpb * WINDOW) == 0
    hist_p = lambda bi, ci: (bi * (tp // WINDOW) + jnp.maximum(ci * wpb - 1, 0), 0)
    o_p = _attention(q, kw, vw, kw, vw, hist_p, 0, tp, bp, True, _bucket_table(CHUNK), rel_bias, sinks2)

    def widen(cache):
        wide = jnp.broadcast_to(cache[:, :, :, None, :], (bs, WINDOW, N_KV, GROUP, HEAD_DIM))
        return wide.reshape(bs * WINDOW, D).astype(BF16)

    ck, cv = widen(cache_swa_k[0]), widen(cache_swa_v[0])
    hist_s = lambda bi, ci: (bi, 0)
    o_s = _attention(q, kw, vw, ck, cv, hist_s, rows_p, ts, bs, False, _bucket_table(ts), rel_bias, sinks2)
    y = _out_proj(y, [(o_p, o_s)], [bf(w_out_att[0])], tm, n_p, tiled_out=True)
    yp, ys = _moe(y, vec(norm_ffn[1]), w_router[0], bf(w1_moe[0]), bf(w3_moe[0]), bf(w2_moe[0]), tm, n_p, n_s)

    def tail(a, rows, b, t, keep):
        if rows == 0:
            return jnp.stack([a[(i + 1) * t - keep:(i + 1) * t] for i in range(b)])
        return a[rows:rows + b * t].reshape(b, t, -1)[:, t - keep:]

    kv4 = lambda a, b: a.reshape(b, -1, N_KV, HEAD_DIM)
    one = lambda a: a[None]
    k_s = jnp.concatenate([cache_swa_k[0][:, ts:], kv4(k[rows_p:], bs)], axis=1)
    v_s = jnp.concatenate([cache_swa_v[0][:, ts:], kv4(v[rows_p:], bs)], axis=1)
    return (yp.reshape(bp, tp, D), ys.reshape(bs, ts, D),
            one(tail(xa, 0, bp, tp, CONV_W - 1)), one(hl_p.reshape(bp, D)), one(tail(xm, 0, bp, tp, CONV_W - 1)),
            one(c_p), one(nn_p), one(m_p[:, :, 0]),
            one(kv4(tail(k, 0, bp, tp, WINDOW), bp)), one(kv4(tail(v, 0, bp, tp, WINDOW), bp)),
            one(tail(xa, rows_p, bs, ts, CONV_W - 1)), one(hl_s.reshape(bs, D)),
            one(tail(xm, rows_p, bs, ts, CONV_W - 1)), one(c_s), one(nn_s), one(m_s[:, :, 0]),
            one(k_s), one(v_s))
```

```python
import functools
import math

import jax
import jax.numpy as jnp
from jax import lax
from jax.experimental import pallas as pl
from jax.experimental.pallas import tpu as pltpu

F32 = jnp.float32
BF16 = jnp.bfloat16

D = 1024
CHUNK = 64
CONV_W = 4
NB_A = 8
BS_A = D // NB_A
RG_C = 8.0
NH_B = 4
DH_B = D // NH_B
N_HEADS = 16
HEAD_DIM = D // N_HEADS
N_KV = 4
GROUP = N_HEADS // N_KV
D_KV = N_KV * HEAD_DIM
WINDOW = 128
NUM_BUCKETS = 32
MAX_DISTANCE = 128
N_EXPERTS = 8
EPS = 1e-6
LANES = 128
SUB = D // LANES
NEG_INF = float("-inf")
MIN_NORMAL = 1.1754944e-38

TOKEN_TILE = 512
EXPERT_TILE = 512
FF_SPLIT_DENSE = 3
FF_SPLIT_MOE = 2
SEQ_BLOCK = 256
ATTN_BLOCK = 256
VMEM_LIMIT = 56 * 1024 * 1024


def _params(sem):
    return pltpu.CompilerParams(dimension_semantics=sem, vmem_limit_bytes=VMEM_LIMIT)


def _full(shape):
    return pl.BlockSpec(shape, lambda *_: (0,) * len(shape))


def _resident(shape):
    return pl.BlockSpec(shape, lambda *_: (0,) * len(shape), pipeline_mode=pl.Buffered(1))


def _rows(tm, c):
    return pl.BlockSpec((tm, c), lambda i: (i, 0))


def _pair(tm, c, n_p):
    return [pl.BlockSpec((tm, c), lambda i: (jnp.minimum(i, n_p - 1), 0)),
            pl.BlockSpec((tm, c), lambda i: (jnp.maximum(i - n_p, 0), 0))]


def _rms(x, g):
    ms = jnp.mean(x * x, axis=-1, keepdims=True)
    return x * lax.rsqrt(ms + EPS) * g


def _silu(x):
    return x * jax.nn.sigmoid(x)


def _sigmoid_tanh(x):
    return 0.5 * jnp.tanh(0.5 * x) + 0.5


def _softplus(x):
    return jnp.maximum(x, 0.0) + jnp.log1p(jnp.exp(-jnp.abs(x)))


def _gelu_tanh(x):
    c = math.sqrt(2.0 / math.pi)
    return x * (0.5 * (1.0 + jnp.tanh(c * (x + 0.044715 * (x * x * x)))))


def _dot(a, b):
    return jnp.dot(a, b, preferred_element_type=F32)


def _dot_nt(a, b):
    return lax.dot_general(a, b, (((1,), (1,)), ((), ())), preferred_element_type=F32)


def _dot_tn(a, b):
    return lax.dot_general(a, b, (((0,), (0,)), ((), ())), preferred_element_type=F32)


def _in_proj_ab_kernel(n_p, xp_ref, xs_ref, g_ref, w_ref, *o_refs):
    i = pl.program_id(0)
    x = jnp.where(i < n_p, xp_ref[...], xs_ref[...])
    xn = _rms(x, g_ref[...]).astype(BF16)
    for c, o_ref in enumerate(o_refs):
        o_ref[...] = _dot(xn, w_ref[:, c * D:(c + 1) * D])


def _in_proj_ab(xp, xs, g, w, tm):
    n_p, n_s = xp.shape[0] // tm, xs.shape[0] // tm
    n = xp.shape[0] + xs.shape[0]
    n_out = w.shape[1] // D
    return pl.pallas_call(
        functools.partial(_in_proj_ab_kernel, n_p),
        grid=(n_p + n_s,),
        in_specs=_pair(tm, D, n_p) + [_full((1, D)), _resident(w.shape)],
        out_specs=[_rows(tm, D)] * n_out,
        out_shape=[jax.ShapeDtypeStruct((n, D), F32)] * n_out,
        compiler_params=_params(("parallel",)),
        name="in_proj_ab",
    )(xp, xs, g, w)


def _tile_rows_load(ref):
    tm = ref.shape[0] // SUB
    return jnp.concatenate([ref[pl.ds(s, tm, stride=SUB), :] for s in range(SUB)], axis=1)


def _tile_rows_store(ref, val):
    tm = val.shape[0]
    for s in range(SUB):
        ref[pl.ds(s, tm, stride=SUB), :] = val[:, s * LANES:(s + 1) * LANES]


def _tile_spec(tm):
    return pl.BlockSpec((tm * SUB, LANES), lambda i: (i, 0))


def _out_proj_kernel(n_p, n_res, n_a, tiled_out, *refs):
    i = pl.program_id(0)
    res_refs = refs[:n_res]
    a_refs = refs[n_res:n_res + 2 * n_a]
    w_refs = refs[n_res + 2 * n_a:n_res + 3 * n_a]
    o_ref = refs[-1]
    if n_res == 2:
        acc = jnp.where(i < n_p, res_refs[0][...], res_refs[1][...])
    else:
        acc = res_refs[0][...]
    for k in range(n_a):
        a = jnp.where(i < n_p, a_refs[2 * k][...], a_refs[2 * k + 1][...])
        acc = acc + _dot(a, w_refs[k][...])
    if tiled_out:
        _tile_rows_store(o_ref, acc)
    else:
        o_ref[...] = acc


def _out_proj(res, a_pairs, ws, tm, n_p, tiled_out=False):
    res = res if isinstance(res, tuple) else (res,)
    n = sum(a.shape[0] for a in a_pairs[0])
    res_specs = _pair(tm, D, n_p) if len(res) == 2 else [_rows(tm, D)]
    a_specs = []
    for a in a_pairs:
        a_specs += _pair(tm, a[0].shape[1], n_p)
    return pl.pallas_call(
        functools.partial(_out_proj_kernel, n_p, len(res), len(a_pairs), tiled_out),
        grid=(n // tm,),
        in_specs=res_specs + a_specs + [_resident(w.shape) for w in ws],
        out_specs=_tile_spec(tm) if tiled_out else _rows(tm, D),
        out_shape=jax.ShapeDtypeStruct((n * SUB, LANES) if tiled_out else (n, D), F32),
        compiler_params=_params(("parallel",)),
        name="out_proj",
    )(*res, *[x for a in a_pairs for x in a], *ws)


def _ffn_dense_kernel(x_ref, g_ref, w1_ref, w3_ref, w2_ref, o_ref):
    x = x_ref[...]
    xn = _rms(x, g_ref[...]).astype(BF16)
    step = w1_ref.shape[1] // FF_SPLIT_DENSE
    acc = x
    for c in range(FF_SPLIT_DENSE):
        sl = slice(c * step, (c + 1) * step)
        a = (_silu(_dot(xn, w1_ref[:, sl])) * _dot(xn, w3_ref[:, sl])).astype(BF16)
        acc = acc + _dot(a, w2_ref[sl, :])
    o_ref[...] = acc


def _ffn_dense(x, g, w1, w3, w2, tm):
    n = x.shape[0]
    return pl.pallas_call(
        _ffn_dense_kernel,
        grid=(n // tm,),
        in_specs=[_rows(tm, D), _full((1, D)), _resident(w1.shape), _resident(w3.shape), _resident(w2.shape)],
        out_specs=_rows(tm, D),
        out_shape=jax.ShapeDtypeStruct((n, D), F32),
        compiler_params=_params(("parallel",)),
        name="ffn_dense",
    )(x, g, w1, w3, w2)


def _in_proj_att_kernel(x_ref, g_ref, w_ref, gq_ref, gk_ref, q_ref, kw_ref, vw_ref, k_ref, v_ref):
    xn = _rms(x_ref[...], g_ref[...]).astype(BF16)
    cw = D_KV
    r = lax.broadcasted_iota(jnp.int32, (cw, cw), 0) // HEAD_DIM
    c = lax.broadcasted_iota(jnp.int32, (cw, cw), 1) // HEAD_DIM
    group_ones = (r == c).astype(BF16)
    er = lax.broadcasted_iota(jnp.int32, (D_KV, D), 0)
    ec = lax.broadcasted_iota(jnp.int32, (D_KV, D), 1)
    widen = ((er // HEAD_DIM == ec // (GROUP * HEAD_DIM)) & (er % HEAD_DIM == ec % HEAD_DIM)).astype(BF16)

    def head_norm(y, gain):
        ms = _dot((y * y).astype(BF16), group_ones) * (1.0 / HEAD_DIM)
        return y * lax.rsqrt(ms + EPS) * gain

    for b in range(D // cw):
        y = _dot(xn, w_ref[:, b * cw:(b + 1) * cw])
        q_ref[:, b * cw:(b + 1) * cw] = (head_norm(y, gq_ref[...]) * (HEAD_DIM ** -0.5)).astype(BF16)
    k = head_norm(_dot(xn, w_ref[:, D:D + D_KV]), gk_ref[...])
    v = _dot(xn, w_ref[:, D + D_KV:D + 2 * D_KV])
    k_ref[...] = k
    v_ref[...] = v
    kw_ref[...] = _dot(k.astype(BF16), widen).astype(BF16)
    vw_ref[...] = _dot(v.astype(BF16), widen).astype(BF16)


def _in_proj_att(x, g, w, gq, gk, tm):
    n = x.shape[0]
    return pl.pallas_call(
        _in_proj_att_kernel,
        grid=(n // tm,),
        in_specs=[_rows(tm, D), _full((1, D)), _resident(w.shape), _full((1, D_KV)), _full((1, D_KV))],
        out_specs=[_rows(tm, D)] * 3 + [_rows(tm, D_KV)] * 2,
        out_shape=[jax.ShapeDtypeStruct((n, D), BF16)] * 3 + [jax.ShapeDtypeStruct((n, D_KV), F32)] * 2,
        compiler_params=_params(("parallel",)),
        name="in_proj_att",
    )(x, g, w, gq, gk)


def _causal_conv(x, xbuf, conv0_ref, wc_ref, bc_ref, first):
    tb = x.shape[0]

    @pl.when(first)
    def _():
        xbuf[5:8, :] = conv0_ref[0]

    xbuf[8:8 + tb, :] = x
    y = bc_ref[...] + xbuf[5:5 + tb, :] * wc_ref[0:1, :]
    for j in range(1, CONV_W):
        y = y + xbuf[5 + j:5 + j + tb, :] * wc_ref[j:j + 1, :]
    xbuf[5:8, :] = xbuf[5 + tb:8 + tb, :]
    return y


def _seq_specs(bsz, seq, tb, row0):
    nc = seq // tb
    off = row0 // tb
    return nc, (lambda bi, ci: (off + bi * nc + ci, 0)), (lambda bi, ci: (bi * nc + ci, 0))


def _scan_by_doubling(a, u, pos, axis, length):
    d = 1
    while d < length:
        keep = pos >= d
        u = jnp.where(keep, a * pltpu.roll(u, d, axis) + u, u)
        a = jnp.where(keep, a * pltpu.roll(a, d, axis), a)
        d *= 2
    return a, u


def _linear_scan(a, u, h0, row):
    a, u = _scan_by_doubling(a, u, row, 0, a.shape[0])
    return a * h0 + u


def _rglru_kernel(xa_ref, ga_ref, conv0_ref, h0_ref, wc_ref, bc_ref, wa_ref, ba_ref, wx_ref, bx_ref, lam_ref,
                  ya_ref, hl_ref, xbuf):
    first = pl.program_id(1) == 0
    tb = xa_ref.shape[0]

    @pl.when(first)
    def _():
        hl_ref[0] = h0_ref[0]

    y = _causal_conv(xa_ref[...], xbuf, conv0_ref, wc_ref, bc_ref, first)
    row = lax.broadcasted_iota(jnp.int32, (tb, BS_A), 0)
    for n in range(NB_A):
        sl = slice(n * BS_A, (n + 1) * BS_A)
        yn = y[:, sl]
        yb = yn.astype(BF16)
        r = _sigmoid_tanh(_dot(yb, wa_ref[n]) + ba_ref[:, sl])
        ig = _sigmoid_tanh(_dot(yb, wx_ref[n]) + bx_ref[:, sl])
        log_a = (-RG_C) * r * _softplus(-lam_ref[:, sl])
        a = jnp.exp(log_a)
        th = jnp.tanh(log_a)
        gain2 = -2.0 * th / (1.0 - th)
        u = yn * ig * (gain2 * lax.rsqrt(jnp.maximum(gain2, MIN_NORMAL)))
        h = _linear_scan(a, u, hl_ref[0, :, sl], row)
        hl_ref[0, :, sl] = h[tb - 1:tb, :]
        ya_ref[:, sl] = (h * _gelu_tanh(ga_ref[:, sl])).astype(BF16)


def _rglru(xa, ga, row0, seq, conv0, h0, wc, bc, wa, ba, wx, bx, lam):
    bsz = conv0.shape[0]
    tb = min(seq, SEQ_BLOCK)
    nc, in_map, out_map = _seq_specs(bsz, seq, tb, row0)
    state3 = lambda bi, ci: (bi, 0, 0)
    return pl.pallas_call(
        _rglru_kernel,
        grid=(bsz, nc),
        in_specs=[pl.BlockSpec((tb, D), in_map), pl.BlockSpec((tb, D), in_map),
                  pl.BlockSpec((1, CONV_W - 1, D), state3), pl.BlockSpec((1, 1, D), state3),
                  _full(wc.shape), _full(bc.shape), _full(wa.shape), _full(ba.shape), _full(wx.shape),
                  _full(bx.shape), _full(lam.shape)],
        out_specs=[pl.BlockSpec((tb, D), out_map), pl.BlockSpec((1, 1, D), state3)],
        out_shape=[jax.ShapeDtypeStruct((bsz * seq, D), BF16), jax.ShapeDtypeStruct((bsz, 1, D), F32)],
        scratch_shapes=[pltpu.VMEM((tb + 8, D), F32)],
        compiler_params=_params(("parallel", "arbitrary")),
        name="rglru",
    )(xa, ga, conv0, h0, wc, bc, wa, ba, wx, bx, lam)


def _mlstm_pre_kernel(chunk, xm_ref, conv0_ref, m0_ref, wc_ref, bc_ref, wq_ref, wk_ref, wv_ref, wif_ref, bif_ref,
                      wifc_ref, bifc_ref, xc_ref, q_ref, k_ref, v_ref, w_ref, col_ref, m_ref, xbuf):
    first = pl.program_id(1) == 0
    tb = xm_ref.shape[0]
    L = chunk

    @pl.when(first)
    def _():
        m_ref[...] = m0_ref[...]

    x = xm_ref[...]
    xc = _silu(_causal_conv(x, xbuf, conv0_ref, wc_ref, bc_ref, first))
    xc_ref[...] = xc
    g_row = bif_ref[...]
    g_col = bifc_ref[...]
    for h in range(NH_B):
        sl = slice(h * DH_B, (h + 1) * DH_B)
        xch = xc[:, sl].astype(BF16)
        q = _dot(xch, wq_ref[h]).astype(BF16)
        k = _dot(xch, wk_ref[h]).astype(BF16)
        v = _dot(x[:, sl].astype(BF16), wv_ref[h]).astype(BF16)
        q_ref[:, sl] = q
        k_ref[:, sl] = k
        v_ref[:, sl] = v
        for part, val in enumerate((q, k, v)):
            psl = slice(part * D + h * DH_B, part * D + (h + 1) * DH_B)
            g_row = g_row + _dot_nt(wif_ref[:, psl], val)
            g_col = g_col + _dot(val, wifc_ref[psl, :])

    def exact_dot(a, b01):
        p1 = a.astype(BF16)
        r1 = a - p1.astype(F32)
        p2 = r1.astype(BF16)
        p3 = (r1 - p2.astype(F32)).astype(BF16)
        return _dot(p1, b01) + _dot(p2, b01) + _dot(p3, b01)

    def exact_dot_left(b01, a):
        p1 = a.astype(BF16)
        r1 = a - p1.astype(F32)
        p2 = r1.astype(BF16)
        p3 = (r1 - p2.astype(F32)).astype(BF16)
        return _dot(b01, p1) + _dot(b01, p2) + _dot(b01, p3)

    tt = lax.broadcasted_iota(jnp.int32, (L, L), 0)
    ss = lax.broadcasted_iota(jnp.int32, (L, L), 1)
    upper = (tt <= ss).astype(BF16)
    lower = (ss <= tt).astype(BF16)
    i_row = g_row[0:NH_B, :]
    b_row = exact_dot(-_softplus(-g_row), upper)[NH_B:2 * NH_B, :]
    w_ref[0] = i_row - b_row
    i_col = g_col[:, 0:LANES]
    b_col = exact_dot_left(lower, -_softplus(-g_col[:, LANES:2 * LANES]))
    w_col = i_col - b_col
    trow = lax.broadcasted_iota(jnp.int32, (L, LANES), 0)
    run_max = w_col
    d = 1
    while d < L:
        run_max = jnp.where(trow >= d, jnp.maximum(run_max, pltpu.roll(run_max, d, 0)), run_max)
        d *= 2
    m_prev = m_ref[0]
    big_m = jnp.maximum(m_prev, run_max)
    m_last = big_m[L - 1:L, :]
    m_ref[0] = b_col[L - 1:L, :] + m_last
    head_lane = lax.broadcasted_iota(jnp.int32, (L, LANES), 1) < NH_B
    table = jnp.where(head_lane, big_m, 0.0)
    for c, col in enumerate((jnp.exp(m_prev - big_m), jnp.exp(-(b_col + big_m)), jnp.exp(w_col - m_last))):
        table = table + pltpu.roll(jnp.where(head_lane, col, 0.0), (c + 1) * NH_B, 1)
    col_ref[...] = table


def _mlstm_pre(xm, row0, seq, conv0, m0, wc, bc, wq, wk, wv, wif_t, bif, wif_c, bif_c):
    bsz = conv0.shape[0]
    tb = min(seq, SEQ_BLOCK)
    chunk = tb
    nc, in_map, out_map = _seq_specs(bsz, seq, tb, row0)
    n = bsz * seq
    st3 = lambda bi, ci: (bi, 0, 0)
    return pl.pallas_call(
        functools.partial(_mlstm_pre_kernel, chunk),
        grid=(bsz, nc),
        in_specs=[pl.BlockSpec((tb, D), in_map), pl.BlockSpec((1, CONV_W - 1, D), st3),
                  pl.BlockSpec((1, 1, LANES), st3),
                  _full(wc.shape), _full(bc.shape), _full(wq.shape), _full(wk.shape), _full(wv.shape),
                  _full(wif_t.shape), _full(bif.shape), _full(wif_c.shape), _full(bif_c.shape)],
        out_specs=[pl.BlockSpec((tb, D), out_map)] * 4
                  + [pl.BlockSpec((tb // chunk, NH_B, chunk), lambda bi, ci: (bi * nc + ci, 0, 0)),
                     pl.BlockSpec((tb, LANES), out_map), pl.BlockSpec((1, 1, LANES), st3)],
        out_shape=[jax.ShapeDtypeStruct((n, D), F32)] + [jax.ShapeDtypeStruct((n, D), BF16)] * 3
                  + [jax.ShapeDtypeStruct((n // chunk, NH_B, chunk), F32), jax.ShapeDtypeStruct((n, LANES), F32),
                     jax.ShapeDtypeStruct(m0.shape, F32)],
        scratch_shapes=[pltpu.VMEM((tb + 8, D), F32)],
        compiler_params=_params(("parallel", "arbitrary")),
        name="mlstm_pre",
    )(xm, conv0, m0, wc, bc, wq, wk, wv, wif_t, bif, wif_c, bif_c)


def _mlstm_kernel(chunk, q_ref, k_ref, v_ref, w_ref, col_ref, xc_ref, z_ref, c0_ref, n0_ref, gh_ref, skip_ref,
                  yb_ref, c_ref, n_ref):
    tb = q_ref.shape[0]
    L = chunk
    scale = DH_B ** -0.5

    @pl.when(pl.program_id(1) == 0)
    def _():
        c_ref[...] = c0_ref[...]
        n_ref[...] = n0_ref[...]

    causal = lax.broadcasted_iota(jnp.int32, (L, L), 1) <= lax.broadcasted_iota(jnp.int32, (L, L), 0)
    heads = range(NH_B)
    hsl = [slice(h * DH_B, (h + 1) * DH_B) for h in heads]
    for j in range(tb // L):
        rows = slice(j * L, (j + 1) * L)
        cols = col_ref[rows, :]
        col = lambda c, h: cols[:, c * NH_B + h:c * NH_B + h + 1]
        q = [q_ref[rows, hsl[h]] for h in heads]
        k = [k_ref[rows, hsl[h]] for h in heads]
        v = [v_ref[rows, hsl[h]] for h in heads]
        s = [_dot_nt(q[h], k[h]) for h in heads]
        qc = [_dot(q[h], c_ref[0, h].astype(BF16)) for h in heads]
        qn = [_dot_nt(q[h], jnp.broadcast_to(n_ref[0, h:h + 1, :], (8, DH_B)).astype(BF16))[:, 0:1] for h in heads]
        p = [(s[h] * scale * jnp.where(causal, jnp.exp(w_ref[j, h:h + 1, :] - col(0, h)), 0.0)).astype(BF16)
             for h in heads]
        pv = [_dot(p[h], v[h]) for h in heads]
        psum = [_dot(p[h], jnp.ones((L, LANES), BF16))[:, 0:1] for h in heads]
        for h in heads:
            sc = col(1, h)
            keep = sc[L - 1:L, :]
            kd = k[h].astype(F32) * (col(3, h) * scale)
            c_ref[0, h] = keep * c_ref[0, h] + _dot_tn(kd.astype(BF16), v[h])
            n_ref[0, h:h + 1, :] = keep * n_ref[0, h:h + 1, :] + jnp.sum(kd, axis=0, keepdims=True)
            num = pv[h] + sc * qc[h]
            den = psum[h] + sc * qn[h]
            hs = num / jnp.maximum(jnp.abs(den), col(2, h))
            mu = jnp.mean(hs, axis=1, keepdims=True)
            dev = hs - mu
            var = jnp.mean(dev * dev, axis=1, keepdims=True)
            hn = dev * lax.rsqrt(var + EPS) * gh_ref[:, hsl[h]]
            out = (hn + skip_ref[:, hsl[h]] * xc_ref[rows, hsl[h]]) * _silu(z_ref[rows, hsl[h]])
            yb_ref[rows, hsl[h]] = out.astype(BF16)


def _mlstm(q, k, v, w, col, xc, z, row0, seq, c0, n0, gh, skip):
    bsz = c0.shape[0]
    tb = min(seq, SEQ_BLOCK)
    chunk = tb
    nc, z_map, own_map = _seq_specs(bsz, seq, tb, row0)
    st4 = lambda bi, ci: (bi, 0, 0, 0)
    st3 = lambda bi, ci: (bi, 0, 0)
    return pl.pallas_call(
        functools.partial(_mlstm_kernel, chunk),
        grid=(bsz, nc),
        in_specs=[pl.BlockSpec((tb, D), own_map)] * 3
                 + [pl.BlockSpec((tb // chunk, NH_B, chunk), lambda bi, ci: (bi * nc + ci, 0, 0)),
                    pl.BlockSpec((tb, LANES), own_map),
                    pl.BlockSpec((tb, D), own_map), pl.BlockSpec((tb, D), z_map),
                    pl.BlockSpec((1, NH_B, DH_B, DH_B), st4), pl.BlockSpec((1, NH_B, DH_B), st3),
                    _full(gh.shape), _full(skip.shape)],
        out_specs=[pl.BlockSpec((tb, D), own_map), pl.BlockSpec((1, NH_B, DH_B, DH_B), st4),
                   pl.BlockSpec((1, NH_B, DH_B), st3)],
        out_shape=[jax.ShapeDtypeStruct((bsz * seq, D), BF16), jax.ShapeDtypeStruct(c0.shape, F32),
                   jax.ShapeDtypeStruct(n0.shape, F32)],
        compiler_params=_params(("parallel", "arbitrary")),
        name="mlstm",
    )(q, k, v, w, col, xc, z, c0, n0, gh, skip)


def _attn_kernel(masked, L, q_ref, kh_ref, ko_ref, vh_ref, vo_ref, bucket_ref, relb_ref, sink_ref,
                 o_ref, bias_s, s_scr, p_scr):
    nk = WINDOW + L
    ci = pl.program_id(1)

    @pl.when((pl.program_id(0) == 0) & (ci == 0))
    def _():
        bucket = bucket_ref[...]
        for h in range(N_HEADS):
            acc = jnp.zeros((L, nk), F32)
            for b in range(NUM_BUCKETS):
                acc = jnp.where(bucket == b, relb_ref[b, h], acc)
            bias_s[h * L:(h + 1) * L, :] = acc

    kall = jnp.concatenate([kh_ref[...], ko_ref[...]], axis=0)
    vall = jnp.concatenate([vh_ref[...], vo_ref[...]], axis=0)
    gw = GROUP * HEAD_DIM
    slot = lax.broadcasted_iota(jnp.int32, (L, gw), 1) // HEAD_DIM
    kk = lax.broadcasted_iota(jnp.int32, (1, nk), 1)
    gl = GROUP * L
    n_sub = q_ref.shape[0] // L
    hl = N_HEADS * L
    for u in range(n_sub):
        kcat = kall[u * L:u * L + nk, :]
        for g in range(N_KV):
            gsl = slice(g * gw, (g + 1) * gw)
            qg = q_ref[u * L:(u + 1) * L, gsl]
            zero = jnp.zeros_like(qg)
            qs = jnp.concatenate([jnp.where(slot == j, qg, zero) for j in range(GROUP)], axis=0)
            s_scr[u * hl + g * gl:u * hl + (g + 1) * gl, :] = _dot_nt(qs, kcat[:, gsl])
    for u in range(n_sub):
        if masked:
            valid = kk >= WINDOW - u * L - ci * (n_sub * L)
        for h in range(N_HEADS):
            rows = slice(u * hl + h * L, u * hl + (h + 1) * L)
            s = s_scr[rows, :] + bias_s[h * L:(h + 1) * L, :]
            if masked:
                s = jnp.where(valid, s, NEG_INF)
            sink = sink_ref[0, h]
            mx = jnp.maximum(jnp.max(s, axis=1, keepdims=True), sink)
            e = jnp.exp(s - mx)
            den = jnp.sum(e, axis=1, keepdims=True) + jnp.exp(sink - mx)
            p_scr[rows, :] = (e / den).astype(BF16)
    for u in range(n_sub):
        vcat = vall[u * L:u * L + nk, :]
        for g in range(N_KV):
            gsl = slice(g * gw, (g + 1) * gw)
            ow = _dot(p_scr[u * hl + g * gl:u * hl + (g + 1) * gl, :], vcat[:, gsl])
            og = jnp.where(slot == 0, ow[0:L, :], 0.0)
            for j in range(1, GROUP):
                og = jnp.where(slot == j, ow[j * L:(j + 1) * L, :], og)
            o_ref[u * L:(u + 1) * L, gsl] = og.astype(BF16)


def _attention(q, k_own, v_own, k_hist, v_hist, hist_map, row0, seq, bsz, masked, bucket, rel_bias, sinks):
    L = min(seq, CHUNK)
    lb = min(seq, ATTN_BLOCK)
    nc, in_map, out_map = _seq_specs(bsz, seq, lb, row0)
    n_sub = lb // L
    nk = WINDOW + L
    smem = functools.partial(pl.BlockSpec, memory_space=pltpu.SMEM)
    return pl.pallas_call(
        functools.partial(_attn_kernel, masked, L),
        grid=(bsz, nc),
        in_specs=[pl.BlockSpec((lb, D), in_map),
                  pl.BlockSpec((WINDOW, D), hist_map), pl.BlockSpec((lb, D), in_map),
                  pl.BlockSpec((WINDOW, D), hist_map), pl.BlockSpec((lb, D), in_map),
                  _full(bucket.shape), smem(), smem()],
        out_specs=pl.BlockSpec((lb, D), out_map),
        out_shape=jax.ShapeDtypeStruct((bsz * seq, D), BF16),
        scratch_shapes=[pltpu.VMEM((N_HEADS * L, nk), F32), pltpu.VMEM((n_sub * N_HEADS * L, nk), F32),
                        pltpu.VMEM((n_sub * N_HEADS * L, nk), BF16)],
        compiler_params=_params(("arbitrary", "arbitrary")),
        name="swa",
    )(q, k_hist, k_own, v_hist, v_own, bucket, rel_bias, sinks)


def _router_kernel(x_ref, g_ref, wr_ref, info_ref, cnt_ref):
    tm = x_ref.shape[0] // SUB

    @pl.when(pl.program_id(0) == 0)
    def _():
        cnt_ref[...] = jnp.zeros_like(cnt_ref)

    xn = _rms(_tile_rows_load(x_ref), g_ref[...])
    hi = xn.astype(BF16)
    lo = (xn - hi.astype(F32)).astype(BF16)
    w = wr_ref[...]
    whi = w.astype(BF16)
    wlo = (w - whi.astype(F32)).astype(BF16)
    logits = _dot(hi, whi) + _dot(hi, wlo) + _dot(lo, whi)
    lane = lax.broadcasted_iota(jnp.int32, (tm, LANES), 1)
    logits = jnp.where(lane < N_EXPERTS, logits, NEG_INF)
    m1 = jnp.max(logits, axis=1, keepdims=True)
    i1 = jnp.min(jnp.where(logits == m1, lane, LANES), axis=1, keepdims=True)
    rest = jnp.where(lane == i1, NEG_INF, logits)
    m2 = jnp.max(rest, axis=1, keepdims=True)
    i2 = jnp.min(jnp.where(rest == m2, lane, LANES), axis=1, keepdims=True)
    e2 = jnp.exp(m2 - m1)
    g1 = 1.0 / (1.0 + e2)
    g2 = e2 / (1.0 + e2)
    sel = (lane == i1) | (lane == i2)
    tri = (lax.broadcasted_iota(jnp.int32, (tm, tm), 0) > lax.broadcasted_iota(jnp.int32, (tm, tm), 1)).astype(BF16)
    rank = cnt_ref[...] + _dot(tri, sel.astype(BF16))
    r1 = jnp.sum(jnp.where(lane == i1, rank, 0.0), axis=1, keepdims=True)
    r2 = jnp.sum(jnp.where(lane == i2, rank, 0.0), axis=1, keepdims=True)
    cnt_ref[...] = cnt_ref[...] + jnp.sum(sel.astype(F32), axis=0, keepdims=True)
    info = jnp.where(lane == 0, i1.astype(F32), 0.0)
    info = jnp.where(lane == 1, i2.astype(F32), info)
    info = jnp.where(lane == 2, g1, info)
    info = jnp.where(lane == 3, g2, info)
    info = jnp.where(lane == 4, r1, info)
    info = jnp.where(lane == 5, r2, info)
    info_ref[...] = info


def _router(x, g, wr, tm):
    n = x.shape[0] // SUB
    return pl.pallas_call(
        _router_kernel,
        grid=(n // tm,),
        in_specs=[_tile_spec(tm), _full((1, D)), _full(wr.shape)],
        out_specs=[_rows(tm, LANES), _full((1, LANES))],
        out_shape=[jax.ShapeDtypeStruct((n, LANES), F32), jax.ShapeDtypeStruct((1, LANES), F32)],
        compiler_params=_params(("arbitrary",)),
        name="moe_router",
    )(x, g, wr)


def _token_tile(ref, t):
    return ref.at[pl.ds(pl.multiple_of(t * SUB, SUB), SUB)]


def _moe_scatter_kernel(te, zpos_ref, dest_ref, x_ref, xs_hbm, zero_s, sem):
    tm = x_ref.shape[0] // SUB
    span = te * SUB

    def rows_of(tile_start):
        return xs_hbm.at[pl.ds(pl.multiple_of(tile_start * SUB, span), span)]

    @pl.when(pl.program_id(0) == 0)
    def _():
        zero_s[...] = jnp.zeros_like(zero_s)
        for e in range(N_EXPERTS):
            pltpu.make_async_copy(zero_s, rows_of(zpos_ref[e]), sem).start()
        for e in range(N_EXPERTS):
            pltpu.make_async_copy(zero_s, rows_of(zpos_ref[e]), sem).wait()

        def clear_tail(t, carry):
            tail = pltpu.make_async_copy(zero_s, rows_of(t * te), sem)
            tail.start()
            tail.wait()
            return carry

        lax.fori_loop(zpos_ref[N_EXPERTS] // te, xs_hbm.shape[0] // span, clear_tail, 0)

    def copy(r, s):
        return pltpu.make_async_copy(_token_tile(x_ref, r), _token_tile(xs_hbm, dest_ref[s * tm + r]), sem)

    def start(r, carry):
        for s in range(2):
            copy(r, s).start(priority=s)
        return carry

    def wait(r, carry):
        for s in range(2):
            copy(r, s).wait()
        return carry

    lax.fori_loop(0, tm, start, 0)
    lax.fori_loop(0, tm, wait, 0)


def _moe_scatter(x, dest, zpos, n_rows, tm, te):
    n = x.shape[0] // SUB
    return pl.pallas_call(
        functools.partial(_moe_scatter_kernel, te),
        grid_spec=pltpu.PrefetchScalarGridSpec(
            num_scalar_prefetch=1,
            grid=(n // tm,),
            in_specs=[pl.BlockSpec((2 * tm,), lambda i, zp: (i,), memory_space=pltpu.SMEM),
                      pl.BlockSpec((tm * SUB, LANES), lambda i, zp: (i, 0))],
            out_specs=pl.BlockSpec(memory_space=pl.ANY),
            scratch_shapes=[pltpu.VMEM((te * SUB, LANES), F32), pltpu.SemaphoreType.DMA(())],
        ),
        out_shape=jax.ShapeDtypeStruct((n_rows * SUB, LANES), F32),
        compiler_params=_params(("arbitrary",)),
        name="moe_scatter",
    )(zpos, dest, x)


def _moe_ffn_kernel(n_ff, te, pe_ref, na_ref, xs_ref, g_ref, w1_ref, w3_ref, w2_ref, ys_ref, xn_s, acc_s):
    i, j = pl.program_id(0), pl.program_id(1)
    span = te * SUB
    for s in range(2):
        active = s < na_ref[i]
        rows = slice(s * te, (s + 1) * te)

        @pl.when(active & (j == 0))
        def _():
            xn_s[rows, :] = _rms(_tile_rows_load(xs_ref.at[pl.ds(s * span, span)]), g_ref[...]).astype(BF16)
            acc_s[rows, :] = jnp.zeros((te, D), F32)

        @pl.when(active)
        def _():
            xn = xn_s[rows, :]
            a = (_silu(_dot(xn, w1_ref[0])) * _dot(xn, w3_ref[0])).astype(BF16)
            acc_s[rows, :] += _dot(a, w2_ref[0])

        @pl.when(j == n_ff - 1)
        def _():
            _tile_rows_store(ys_ref.at[pl.ds(s * span, span)], jnp.where(active, acc_s[rows, :], 0.0))


def _moe_ffn(xs, g, w1, w3, w2, pair_expert, pair_active, te):
    n_pairs = pair_expert.shape[0]
    d_ff = w1.shape[2]
    n_ff = FF_SPLIT_MOE
    tf = d_ff // n_ff

    def ff(i, j, na_r):
        return jnp.where(na_r[i] > 0, j, n_ff - 1)

    rows = pl.BlockSpec((2 * te * SUB, LANES), lambda i, j, pe_r, na_r: (i, 0))
    return pl.pallas_call(
        functools.partial(_moe_ffn_kernel, n_ff, te),
        grid_spec=pltpu.PrefetchScalarGridSpec(
            num_scalar_prefetch=2,
            grid=(n_pairs, n_ff),
            in_specs=[rows,
                      pl.BlockSpec((1, D), lambda i, j, pe_r, na_r: (0, 0)),
                      pl.BlockSpec((1, D, tf), lambda i, j, pe_r, na_r: (pe_r[i], 0, ff(i, j, na_r))),
                      pl.BlockSpec((1, D, tf), lambda i, j, pe_r, na_r: (pe_r[i], 0, ff(i, j, na_r))),
                      pl.BlockSpec((1, tf, D), lambda i, j, pe_r, na_r: (pe_r[i], ff(i, j, na_r), 0))],
            out_specs=rows,
            scratch_shapes=[pltpu.VMEM((2 * te, D), BF16), pltpu.VMEM((2 * te, D), F32)],
        ),
        out_shape=jax.ShapeDtypeStruct((n_pairs * 2 * te * SUB, LANES), F32),
        compiler_params=_params(("arbitrary", "arbitrary")),
        name="moe_ffn",
    )(pair_expert, pair_active, xs, g, w1, w3, w2)


def _moe_combine_kernel(n_p, dest_ref, info_ref, x_ref, ys_hbm, op_ref, os_ref, buf, sem):
    tm = x_ref.shape[0] // SUB
    i = pl.program_id(0)

    def copy(r, s):
        return pltpu.make_async_copy(_token_tile(ys_hbm, dest_ref[s * tm + r]), _token_tile(buf.at[s], r), sem)

    def start(r, carry):
        for s in range(2):
            copy(r, s).start(priority=s)
        return carry

    def wait(r, carry):
        for s in range(2):
            copy(r, s).wait()
        return carry

    lax.fori_loop(0, tm, start, 0)
    lax.fori_loop(0, tm, wait, 0)
    out = (_tile_rows_load(x_ref) + info_ref[:, 2:3] * _tile_rows_load(buf.at[0])
           + info_ref[:, 3:4] * _tile_rows_load(buf.at[1]))

    @pl.when(i < n_p)
    def _():
        op_ref[...] = out

    @pl.when(i >= n_p)
    def _():
        os_ref[...] = out


def _moe_combine(x, info, dest, ys, tm, n_p, n_s):
    out_p, out_s = _pair(tm, D, n_p)
    return pl.pallas_call(
        functools.partial(_moe_combine_kernel, n_p),
        grid=(n_p + n_s,),
        in_specs=[pl.BlockSpec((2 * tm,), lambda i: (i,), memory_space=pltpu.SMEM),
                  _rows(tm, LANES), _tile_spec(tm), pl.BlockSpec(memory_space=pl.ANY)],
        out_specs=[out_p, out_s],
        out_shape=[jax.ShapeDtypeStruct((n_p * tm, D), F32), jax.ShapeDtypeStruct((n_s * tm, D), F32)],
        scratch_shapes=[pltpu.VMEM((2, tm * SUB, LANES), F32), pltpu.SemaphoreType.DMA(())],
        compiler_params=_params(("arbitrary",)),
        name="moe_combine",
    )(dest, info, x, ys)


def _moe(x, g, w_router, w1, w3, w2, tm, n_p, n_s):
    n = x.shape[0] // SUB
    te = min(EXPERT_TILE, tm)
    wr = jnp.zeros((D, LANES), F32).at[:, :N_EXPERTS].set(w_router)
    info, cnt = _router(x, g, wr, tm)
    e1, e2 = info[:, 0].astype(jnp.int32), info[:, 1].astype(jnp.int32)
    r1, r2 = info[:, 4].astype(jnp.int32), info[:, 5].astype(jnp.int32)
    counts = cnt[0, :N_EXPERTS].astype(jnp.int32)
    region = 2 * te
    padded = (counts + region - 1) // region * region
    ends = jnp.cumsum(padded)
    starts = ends - padded
    dest = jnp.stack([(starts[e1] + r1).reshape(n // tm, tm), (starts[e2] + r2).reshape(n // tm, tm)], axis=1)
    dest = dest.reshape(-1)
    n_pairs = (2 * n + N_EXPERTS * (region - 1)) // region
    pair_start = jnp.arange(n_pairs, dtype=jnp.int32) * region
    pair_expert = jnp.minimum(jnp.sum(pair_start[:, None] >= ends[None, :], axis=1), N_EXPERTS - 1).astype(jnp.int32)
    used_end = starts + (counts + te - 1) // te * te
    pair_active = jnp.clip((used_end[pair_expert] - pair_start) // te, 0, 2).astype(jnp.int32)
    last_region = jnp.maximum(ends - region, 0).astype(jnp.int32)
    xs = _moe_scatter(x, dest, jnp.concatenate([last_region, ends[-1:].astype(jnp.int32)]), n_pairs * region, tm,
                      region)
    ys = _moe_ffn(xs, g, w1, w3, w2, pair_expert, pair_active, te)
    return _moe_combine(x, info, dest, ys, tm, n_p, n_s)


def _t5_bucket(rel):
    n = -rel
    half = NUM_BUCKETS // 2
    ret = jnp.where(n < 0, half, 0)
    n = jnp.abs(n)
    max_exact = half // 2
    nf = jnp.maximum(n, 1).astype(F32)
    large = max_exact + (jnp.log(nf / max_exact) / math.log(MAX_DISTANCE / max_exact)
                         * (half - max_exact)).astype(jnp.int32)
    large = jnp.minimum(large, half - 1)
    return ret + jnp.where(n < max_exact, n, large)


def _bucket_table(length):
    kpos = jnp.arange(WINDOW + length) - WINDOW
    return _t5_bucket(kpos[None, :] - jnp.arange(length)[:, None]).astype(jnp.int32)


def kernel(x_prompt, x_sample, state_conv_a, state_rglru_h, state_conv_b, state_mlstm_c, state_mlstm_n, state_mlstm_m, cache_swa_k, cache_swa_v, norm_mix, norm_ffn, w_in_ab, w_conv_a, b_conv_a, w_rg_a, b_rg_a, w_rg_x, b_rg_x, rg_lambda, w_conv_b, b_conv_b, w_q_b, w_k_b, w_v_b, w_if_b, b_if_b, g_hnorm_b, skip_b, w_out_ab, w1_dense, w3_dense, w2_dense, w_in_att, g_qnorm, g_knorm, sinks, w_out_att, rel_bias, w_router, w1_moe, w3_moe, w2_moe):
    bp, tp, _ = x_prompt.shape
    bs, ts, _ = x_sample.shape
    assert norm_mix.shape[0] == 2 and w_in_ab.shape[0] == 1 and w_in_att.shape[0] == 1
    assert tp % CHUNK == 0 and ts <= CHUNK and cache_swa_k.shape[2] == WINDOW
    rows_p, rows_s = bp * tp, bs * ts
    tm = math.gcd(math.gcd(rows_p, rows_s), TOKEN_TILE)
    n_p, n_s = rows_p // tm, rows_s // tm
    xp = x_prompt.reshape(rows_p, D)
    xs = x_sample.reshape(rows_s, D)
    bf = lambda w: w.astype(BF16)
    vec = lambda v: v.reshape(1, -1)

    xa, ga, xm, z = _in_proj_ab(xp, xs, vec(norm_mix[0]), bf(w_in_ab[0]), tm)
    rg_w = (w_conv_a[0], vec(b_conv_a[0]), bf(w_rg_a[0]), vec(b_rg_a[0]), bf(w_rg_x[0]), vec(b_rg_x[0]),
            vec(rg_lambda[0]))
    ya_p, hl_p = _rglru(xa, ga, 0, tp, jnp.zeros((bp, CONV_W - 1, D), F32), jnp.zeros((bp, 1, D), F32), *rg_w)
    ya_s, hl_s = _rglru(xa, ga, rows_p, ts, state_conv_a[0], state_rglru_h[0].reshape(bs, 1, D), *rg_w)

    def gate_cols(w):
        pad = ((0, 0), (0, LANES - NH_B))
        return jnp.concatenate([jnp.pad(w[:, :NH_B], pad), jnp.pad(w[:, NH_B:], pad)], axis=1)

    pre_w = (w_conv_b[0], vec(b_conv_b[0]), bf(w_q_b[0]), bf(w_k_b[0]), bf(w_v_b[0]), bf(w_if_b[0].T),
             b_if_b[0].reshape(2 * NH_B, 1), bf(gate_cols(w_if_b[0])), gate_cols(b_if_b[0][None]))
    rec_w = (g_hnorm_b[0].reshape(1, D), vec(skip_b[0]))
    lanes = lambda m: jnp.pad(m, ((0, 0), (0, LANES - NH_B)))[:, None, :]
    pre_p = _mlstm_pre(xm, 0, tp, jnp.zeros((bp, CONV_W - 1, D), F32), jnp.zeros((bp, 1, LANES), F32), *pre_w)
    pre_s = _mlstm_pre(xm, rows_p, ts, state_conv_b[0], lanes(state_mlstm_m[0]), *pre_w)
    m_p, m_s = pre_p[6], pre_s[6]
    yb_p, c_p, nn_p = _mlstm(*pre_p[1:6], pre_p[0], z, 0, tp, jnp.zeros((bp, NH_B, DH_B, DH_B), F32),
                             jnp.zeros((bp, NH_B, DH_B), F32), *rec_w)
    yb_s, c_s, nn_s = _mlstm(*pre_s[1:6], pre_s[0], z, rows_p, ts, state_mlstm_c[0], state_mlstm_n[0], *rec_w)

    w_out = bf(w_out_ab[0])
    y = _out_proj((xp, xs), [(ya_p, ya_s), (yb_p, yb_s)], [w_out[:D], w_out[D:]], tm, n_p)
    y = _ffn_dense(y, vec(norm_ffn[0]), bf(w1_dense[0]), bf(w3_dense[0]), bf(w2_dense[0]), tm)

    tile2 = lambda gain: jnp.tile(gain, D_KV // HEAD_DIM).reshape(1, D_KV)
    q, kw, vw, k, v = _in_proj_att(y, vec(norm_mix[1]), bf(w_in_att[0]), tile2(g_qnorm[0]), tile2(g_knorm[0]), tm)
    sinks2 = sinks[0].reshape(1, N_HEADS)
    wpb = min(tp, ATTN_BLOCK) // WINDOW
    assert wpb >= 1 and tp % (wpb * WINDOW) == 0
    hist_p = lambda bi, ci: (bi * (tp // WINDOW) + jnp.maximum(ci * wpb - 1, 0), 0)
    o_p = _attention(q, kw, vw, kw, vw, hist_p, 0, tp, bp, True, _bucket_table(CHUNK), rel_bias, sinks2)

    def widen(cache):
        wide = jnp.broadcast_to(cache[:, :, :, None, :], (bs, WINDOW, N_KV, GROUP, HEAD_DIM))
        return wide.reshape(bs * WINDOW, D).astype(BF16)

    ck, cv = widen(cache_swa_k[0]), widen(cache_swa_v[0])
    hist_s = lambda bi, ci: (bi, 0)
    o_s = _attention(q, kw, vw, ck, cv, hist_s, rows_p, ts, bs, False, _bucket_table(ts), rel_bias, sinks2)
    y = _out_proj(y, [(o_p, o_s)], [bf(w_out_att[0])], tm, n_p, tiled_out=True)
    yp, ys = _moe(y, vec(norm_ffn[1]), w_router[0], bf(w1_moe[0]), bf(w3_moe[0]), bf(w2_moe[0]), tm, n_p, n_s)

    def tail(a, rows, b, t, keep):
        if rows == 0:
            return jnp.stack([a[(i + 1) * t - keep:(i + 1) * t] for i in range(b)])
        return a[rows:rows + b * t].reshape(b, t, -1)[:, t - keep:]

    kv4 = lambda a, b: a.reshape(b, -1, N_KV, HEAD_DIM)
    one = lambda a: a[None]
    k_s = jnp.concatenate([cache_swa_k[0][:, ts:], kv4(k[rows_p:], bs)], axis=1)
    v_s = jnp.concatenate([cache_swa_v[0][:, ts:], kv4(v[rows_p:], bs)], axis=1)
    return (yp.reshape(bp, tp, D), ys.reshape(bs, ts, D),
            one(tail(xa, 0, bp, tp, CONV_W - 1)), one(hl_p.reshape(bp, D)), one(tail(xm, 0, bp, tp, CONV_W - 1)),
            one(c_p), one(nn_p), one(m_p[:, 0, :NH_B]),
            one(kv4(tail(k, 0, bp, tp, WINDOW), bp)), one(kv4(tail(v, 0, bp, tp, WINDOW), bp)),
            one(tail(xa, rows_p, bs, ts, CONV_W - 1)), one(hl_s.reshape(bs, D)),
            one(tail(xm, rows_p, bs, ts, CONV_W - 1)), one(c_s), one(nn_s), one(m_s[:, 0, :NH_B]),
            one(k_s), one(v_s))
```

```python
import functools
import math

import jax
import jax.numpy as jnp
from jax import lax
from jax.experimental import pallas as pl
from jax.experimental.pallas import tpu as pltpu

F32 = jnp.float32
BF16 = jnp.bfloat16

D = 1024
CHUNK = 64
CONV_W = 4
NB_A = 8
BS_A = D // NB_A
RG_C = 8.0
NH_B = 4
DH_B = D // NH_B
N_HEADS = 16
HEAD_DIM = D // N_HEADS
N_KV = 4
GROUP = N_HEADS // N_KV
D_KV = N_KV * HEAD_DIM
WINDOW = 128
NUM_BUCKETS = 32
MAX_DISTANCE = 128
N_EXPERTS = 8
EPS = 1e-6
LANES = 128
SUB = D // LANES
NEG_INF = float("-inf")
MIN_NORMAL = 1.1754944e-38

TOKEN_TILE = 512
EXPERT_TILE = 512
FF_SPLIT_DENSE = 3
FF_SPLIT_MOE = 2
SEQ_BLOCK = 256
RGLRU_BLOCK = 512
SCAN_ROWS = 64
ATTN_BLOCK = 256
VMEM_LIMIT = 56 * 1024 * 1024


def _params(sem):
    return pltpu.CompilerParams(dimension_semantics=sem, vmem_limit_bytes=VMEM_LIMIT)


def _full(shape):
    return pl.BlockSpec(shape, lambda *_: (0,) * len(shape))


def _resident(shape):
    return pl.BlockSpec(shape, lambda *_: (0,) * len(shape), pipeline_mode=pl.Buffered(1))


def _rows(tm, c):
    return pl.BlockSpec((tm, c), lambda i: (i, 0))


def _pair(tm, c, n_p):
    return [pl.BlockSpec((tm, c), lambda i: (jnp.minimum(i, n_p - 1), 0)),
            pl.BlockSpec((tm, c), lambda i: (jnp.maximum(i - n_p, 0), 0))]


def _rms(x, g):
    ms = jnp.mean(x * x, axis=-1, keepdims=True)
    return x * lax.rsqrt(ms + EPS) * g


def _silu(x):
    return x * jax.nn.sigmoid(x)


def _sigmoid_tanh(x):
    return 0.5 * jnp.tanh(0.5 * x) + 0.5


def _softplus(x):
    return jnp.maximum(x, 0.0) + jnp.log1p(jnp.exp(-jnp.abs(x)))


def _gelu_tanh(x):
    c = math.sqrt(2.0 / math.pi)
    return x * (0.5 * (1.0 + jnp.tanh(c * (x + 0.044715 * (x * x * x)))))


def _dot(a, b):
    return jnp.dot(a, b, preferred_element_type=F32)


def _dot_nt(a, b):
    return lax.dot_general(a, b, (((1,), (1,)), ((), ())), preferred_element_type=F32)


def _dot_tn(a, b):
    return lax.dot_general(a, b, (((0,), (0,)), ((), ())), preferred_element_type=F32)


def _in_proj_ab_kernel(n_p, xp_ref, xs_ref, g_ref, w_ref, *o_refs):
    i = pl.program_id(0)
    x = jnp.where(i < n_p, xp_ref[...], xs_ref[...])
    xn = _rms(x, g_ref[...]).astype(BF16)
    for c, o_ref in enumerate(o_refs):
        o_ref[...] = _dot(xn, w_ref[:, c * D:(c + 1) * D])


def _in_proj_ab(xp, xs, g, w, tm):
    n_p, n_s = xp.shape[0] // tm, xs.shape[0] // tm
    n = xp.shape[0] + xs.shape[0]
    n_out = w.shape[1] // D
    return pl.pallas_call(
        functools.partial(_in_proj_ab_kernel, n_p),
        grid=(n_p + n_s,),
        in_specs=_pair(tm, D, n_p) + [_full((1, D)), _resident(w.shape)],
        out_specs=[_rows(tm, D)] * n_out,
        out_shape=[jax.ShapeDtypeStruct((n, D), F32)] * n_out,
        compiler_params=_params(("parallel",)),
        name="in_proj_ab",
    )(xp, xs, g, w)


def _tile_rows_load(ref):
    tm = ref.shape[0] // SUB
    return jnp.concatenate([ref[pl.ds(s, tm, stride=SUB), :] for s in range(SUB)], axis=1)


def _tile_rows_store(ref, val):
    tm = val.shape[0]
    for s in range(SUB):
        ref[pl.ds(s, tm, stride=SUB), :] = val[:, s * LANES:(s + 1) * LANES]


def _tile_spec(tm):
    return pl.BlockSpec((tm * SUB, LANES), lambda i: (i, 0))


def _out_proj_kernel(n_p, n_res, n_a, tiled_out, *refs):
    i = pl.program_id(0)
    res_refs = refs[:n_res]
    a_refs = refs[n_res:n_res + 2 * n_a]
    w_refs = refs[n_res + 2 * n_a:n_res + 3 * n_a]
    o_ref = refs[-1]
    if n_res == 2:
        acc = jnp.where(i < n_p, res_refs[0][...], res_refs[1][...])
    else:
        acc = res_refs[0][...]
    for k in range(n_a):
        a = jnp.where(i < n_p, a_refs[2 * k][...], a_refs[2 * k + 1][...])
        acc = acc + _dot(a, w_refs[k][...])
    if tiled_out:
        _tile_rows_store(o_ref, acc)
    else:
        o_ref[...] = acc


def _out_proj(res, a_pairs, ws, tm, n_p, tiled_out=False):
    res = res if isinstance(res, tuple) else (res,)
    n = sum(a.shape[0] for a in a_pairs[0])
    res_specs = _pair(tm, D, n_p) if len(res) == 2 else [_rows(tm, D)]
    a_specs = []
    for a in a_pairs:
        a_specs += _pair(tm, a[0].shape[1], n_p)
    return pl.pallas_call(
        functools.partial(_out_proj_kernel, n_p, len(res), len(a_pairs), tiled_out),
        grid=(n // tm,),
        in_specs=res_specs + a_specs + [_resident(w.shape) for w in ws],
        out_specs=_tile_spec(tm) if tiled_out else _rows(tm, D),
        out_shape=jax.ShapeDtypeStruct((n * SUB, LANES) if tiled_out else (n, D), F32),
        compiler_params=_params(("parallel",)),
        name="out_proj",
    )(*res, *[x for a in a_pairs for x in a], *ws)


def _ffn_dense_kernel(x_ref, g_ref, w1_ref, w3_ref, w2_ref, o_ref):
    x = x_ref[...]
    xn = _rms(x, g_ref[...]).astype(BF16)
    step = w1_ref.shape[1] // FF_SPLIT_DENSE
    acc = x
    for c in range(FF_SPLIT_DENSE):
        sl = slice(c * step, (c + 1) * step)
        a = (_silu(_dot(xn, w1_ref[:, sl])) * _dot(xn, w3_ref[:, sl])).astype(BF16)
        acc = acc + _dot(a, w2_ref[sl, :])
    o_ref[...] = acc


def _ffn_dense(x, g, w1, w3, w2, tm):
    n = x.shape[0]
    return pl.pallas_call(
        _ffn_dense_kernel,
        grid=(n // tm,),
        in_specs=[_rows(tm, D), _full((1, D)), _resident(w1.shape), _resident(w3.shape), _resident(w2.shape)],
        out_specs=_rows(tm, D),
        out_shape=jax.ShapeDtypeStruct((n, D), F32),
        compiler_params=_params(("parallel",)),
        name="ffn_dense",
    )(x, g, w1, w3, w2)


def _in_proj_att_kernel(x_ref, g_ref, w_ref, gq_ref, gk_ref, q_ref, kw_ref, vw_ref, k_ref, v_ref):
    xn = _rms(x_ref[...], g_ref[...]).astype(BF16)
    cw = D_KV
    r = lax.broadcasted_iota(jnp.int32, (cw, cw), 0) // HEAD_DIM
    c = lax.broadcasted_iota(jnp.int32, (cw, cw), 1) // HEAD_DIM
    group_ones = (r == c).astype(BF16)
    er = lax.broadcasted_iota(jnp.int32, (D_KV, D), 0)
    ec = lax.broadcasted_iota(jnp.int32, (D_KV, D), 1)
    widen = ((er // HEAD_DIM == ec // (GROUP * HEAD_DIM)) & (er % HEAD_DIM == ec % HEAD_DIM)).astype(BF16)

    def head_norm(y, gain):
        ms = _dot((y * y).astype(BF16), group_ones) * (1.0 / HEAD_DIM)
        return y * lax.rsqrt(ms + EPS) * gain

    for b in range(D // cw):
        y = _dot(xn, w_ref[:, b * cw:(b + 1) * cw])
        q_ref[:, b * cw:(b + 1) * cw] = (head_norm(y, gq_ref[...]) * (HEAD_DIM ** -0.5)).astype(BF16)
    k = head_norm(_dot(xn, w_ref[:, D:D + D_KV]), gk_ref[...])
    v = _dot(xn, w_ref[:, D + D_KV:D + 2 * D_KV])
    k_ref[...] = k
    v_ref[...] = v
    kw_ref[...] = _dot(k.astype(BF16), widen).astype(BF16)
    vw_ref[...] = _dot(v.astype(BF16), widen).astype(BF16)


def _in_proj_att(x, g, w, gq, gk, tm):
    n = x.shape[0]
    return pl.pallas_call(
        _in_proj_att_kernel,
        grid=(n // tm,),
        in_specs=[_rows(tm, D), _full((1, D)), _resident(w.shape), _full((1, D_KV)), _full((1, D_KV))],
        out_specs=[_rows(tm, D)] * 3 + [_rows(tm, D_KV)] * 2,
        out_shape=[jax.ShapeDtypeStruct((n, D), BF16)] * 3 + [jax.ShapeDtypeStruct((n, D_KV), F32)] * 2,
        compiler_params=_params(("parallel",)),
        name="in_proj_att",
    )(x, g, w, gq, gk)


def _causal_conv(x, xbuf, conv0_ref, wc_ref, bc_ref, first):
    tb = x.shape[0]

    @pl.when(first)
    def _():
        xbuf[5:8, :] = conv0_ref[0]

    xbuf[8:8 + tb, :] = x
    y = bc_ref[...] + xbuf[5:5 + tb, :] * wc_ref[0:1, :]
    for j in range(1, CONV_W):
        y = y + xbuf[5 + j:5 + j + tb, :] * wc_ref[j:j + 1, :]
    xbuf[5:8, :] = xbuf[5 + tb:8 + tb, :]
    return y


def _seq_specs(bsz, seq, tb, row0):
    nc = seq // tb
    off = row0 // tb
    return nc, (lambda bi, ci: (off + bi * nc + ci, 0)), (lambda bi, ci: (bi * nc + ci, 0))


def _scan_by_doubling(a, u, pos, axis, length):
    d = 1
    while d < length:
        keep = pos >= d
        u = jnp.where(keep, a * pltpu.roll(u, d, axis) + u, u)
        a = jnp.where(keep, a * pltpu.roll(a, d, axis), a)
        d *= 2
    return a, u


def _linear_scan(a, u, h0, row):
    tb = a.shape[0]
    sub = min(tb, SCAN_ROWS)
    out = []
    for k in range(tb // sub):
        rows = slice(k * sub, (k + 1) * sub)
        a_k, u_k = _scan_by_doubling(a[rows, :], u[rows, :], row[0:sub, :], 0, sub)
        h_k = a_k * h0 + u_k
        h0 = h_k[sub - 1:sub, :]
        out.append(h_k)
    return jnp.concatenate(out, axis=0)


def _rglru_kernel(xa_ref, ga_ref, conv0_ref, h0_ref, wc_ref, bc_ref, wa_ref, ba_ref, wx_ref, bx_ref, lam_ref,
                  ya_ref, hl_ref, xbuf):
    first = pl.program_id(1) == 0
    tb = xa_ref.shape[0]

    @pl.when(first)
    def _():
        hl_ref[0] = h0_ref[0]

    y = _causal_conv(xa_ref[...], xbuf, conv0_ref, wc_ref, bc_ref, first)
    row = lax.broadcasted_iota(jnp.int32, (tb, BS_A), 0)
    for n in range(NB_A):
        sl = slice(n * BS_A, (n + 1) * BS_A)
        yn = y[:, sl]
        yb = yn.astype(BF16)
        r = _sigmoid_tanh(_dot(yb, wa_ref[n]) + ba_ref[:, sl])
        ig = _sigmoid_tanh(_dot(yb, wx_ref[n]) + bx_ref[:, sl])
        log_a = (-RG_C) * r * _softplus(-lam_ref[:, sl])
        a = jnp.exp(log_a)
        th = jnp.tanh(log_a)
        gain2 = -2.0 * th / (1.0 - th)
        u = yn * ig * (gain2 * lax.rsqrt(jnp.maximum(gain2, MIN_NORMAL)))
        h = _linear_scan(a, u, hl_ref[0, :, sl], row)
        hl_ref[0, :, sl] = h[tb - 1:tb, :]
        ya_ref[:, sl] = (h * _gelu_tanh(ga_ref[:, sl])).astype(BF16)


def _rglru(xa, ga, row0, seq, conv0, h0, wc, bc, wa, ba, wx, bx, lam):
    bsz = conv0.shape[0]
    tb = math.gcd(seq, RGLRU_BLOCK)
    nc, in_map, out_map = _seq_specs(bsz, seq, tb, row0)
    state3 = lambda bi, ci: (bi, 0, 0)
    return pl.pallas_call(
        _rglru_kernel,
        grid=(bsz, nc),
        in_specs=[pl.BlockSpec((tb, D), in_map), pl.BlockSpec((tb, D), in_map),
                  pl.BlockSpec((1, CONV_W - 1, D), state3), pl.BlockSpec((1, 1, D), state3),
                  _full(wc.shape), _full(bc.shape), _full(wa.shape), _full(ba.shape), _full(wx.shape),
                  _full(bx.shape), _full(lam.shape)],
        out_specs=[pl.BlockSpec((tb, D), out_map), pl.BlockSpec((1, 1, D), state3)],
        out_shape=[jax.ShapeDtypeStruct((bsz * seq, D), BF16), jax.ShapeDtypeStruct((bsz, 1, D), F32)],
        scratch_shapes=[pltpu.VMEM((tb + 8, D), F32)],
        compiler_params=_params(("parallel", "arbitrary")),
        name="rglru",
    )(xa, ga, conv0, h0, wc, bc, wa, ba, wx, bx, lam)


def _mlstm_pre_kernel(chunk, xm_ref, conv0_ref, m0_ref, wc_ref, bc_ref, wq_ref, wk_ref, wv_ref, wif_ref, bif_ref,
                      wifc_ref, bifc_ref, xc_ref, q_ref, k_ref, v_ref, w_ref, col_ref, m_ref, xbuf):
    first = pl.program_id(1) == 0
    tb = xm_ref.shape[0]
    L = chunk

    @pl.when(first)
    def _():
        m_ref[...] = m0_ref[...]

    x = xm_ref[...]
    xc = _silu(_causal_conv(x, xbuf, conv0_ref, wc_ref, bc_ref, first))
    xc_ref[...] = xc
    g_row = bif_ref[...]
    g_col = bifc_ref[...]
    for h in range(NH_B):
        sl = slice(h * DH_B, (h + 1) * DH_B)
        xch = xc[:, sl].astype(BF16)
        q = _dot(xch, wq_ref[h]).astype(BF16)
        k = _dot(xch, wk_ref[h]).astype(BF16)
        v = _dot(x[:, sl].astype(BF16), wv_ref[h]).astype(BF16)
        q_ref[:, sl] = q
        k_ref[:, sl] = k
        v_ref[:, sl] = v
        for part, val in enumerate((q, k, v)):
            psl = slice(part * D + h * DH_B, part * D + (h + 1) * DH_B)
            g_row = g_row + _dot_nt(wif_ref[:, psl], val)
            g_col = g_col + _dot(val, wifc_ref[psl, :])

    def exact_dot(a, b01):
        p1 = a.astype(BF16)
        r1 = a - p1.astype(F32)
        p2 = r1.astype(BF16)
        p3 = (r1 - p2.astype(F32)).astype(BF16)
        return _dot(p1, b01) + _dot(p2, b01) + _dot(p3, b01)

    def exact_dot_left(b01, a):
        p1 = a.astype(BF16)
        r1 = a - p1.astype(F32)
        p2 = r1.astype(BF16)
        p3 = (r1 - p2.astype(F32)).astype(BF16)
        return _dot(b01, p1) + _dot(b01, p2) + _dot(b01, p3)

    tt = lax.broadcasted_iota(jnp.int32, (L, L), 0)
    ss = lax.broadcasted_iota(jnp.int32, (L, L), 1)
    upper = (tt <= ss).astype(BF16)
    lower = (ss <= tt).astype(BF16)
    i_row = g_row[0:NH_B, :]
    b_row = exact_dot(-_softplus(-g_row), upper)[NH_B:2 * NH_B, :]
    w_ref[0] = i_row - b_row
    i_col = g_col[:, 0:LANES]
    b_col = exact_dot_left(lower, -_softplus(-g_col[:, LANES:2 * LANES]))
    w_col = i_col - b_col
    trow = lax.broadcasted_iota(jnp.int32, (L, LANES), 0)
    run_max = w_col
    d = 1
    while d < L:
        run_max = jnp.where(trow >= d, jnp.maximum(run_max, pltpu.roll(run_max, d, 0)), run_max)
        d *= 2
    m_prev = m_ref[0]
    big_m = jnp.maximum(m_prev, run_max)
    m_last = big_m[L - 1:L, :]
    m_ref[0] = b_col[L - 1:L, :] + m_last
    head_lane = lax.broadcasted_iota(jnp.int32, (L, LANES), 1) < NH_B
    table = jnp.where(head_lane, big_m, 0.0)
    for c, col in enumerate((jnp.exp(m_prev - big_m), jnp.exp(-(b_col + big_m)), jnp.exp(w_col - m_last))):
        table = table + pltpu.roll(jnp.where(head_lane, col, 0.0), (c + 1) * NH_B, 1)
    col_ref[...] = table


def _mlstm_pre(xm, row0, seq, conv0, m0, wc, bc, wq, wk, wv, wif_t, bif, wif_c, bif_c):
    bsz = conv0.shape[0]
    tb = min(seq, SEQ_BLOCK)
    chunk = tb
    nc, in_map, out_map = _seq_specs(bsz, seq, tb, row0)
    n = bsz * seq
    st3 = lambda bi, ci: (bi, 0, 0)
    return pl.pallas_call(
        functools.partial(_mlstm_pre_kernel, chunk),
        grid=(bsz, nc),
        in_specs=[pl.BlockSpec((tb, D), in_map), pl.BlockSpec((1, CONV_W - 1, D), st3),
                  pl.BlockSpec((1, 1, LANES), st3),
                  _full(wc.shape), _full(bc.shape), _full(wq.shape), _full(wk.shape), _full(wv.shape),
                  _full(wif_t.shape), _full(bif.shape), _full(wif_c.shape), _full(bif_c.shape)],
        out_specs=[pl.BlockSpec((tb, D), out_map)] * 4
                  + [pl.BlockSpec((tb // chunk, NH_B, chunk), lambda bi, ci: (bi * nc + ci, 0, 0)),
                     pl.BlockSpec((tb, LANES), out_map), pl.BlockSpec((1, 1, LANES), st3)],
        out_shape=[jax.ShapeDtypeStruct((n, D), F32)] + [jax.ShapeDtypeStruct((n, D), BF16)] * 3
                  + [jax.ShapeDtypeStruct((n // chunk, NH_B, chunk), F32), jax.ShapeDtypeStruct((n, LANES), F32),
                     jax.ShapeDtypeStruct(m0.shape, F32)],
        scratch_shapes=[pltpu.VMEM((tb + 8, D), F32)],
        compiler_params=_params(("parallel", "arbitrary")),
        name="mlstm_pre",
    )(xm, conv0, m0, wc, bc, wq, wk, wv, wif_t, bif, wif_c, bif_c)


def _mlstm_kernel(chunk, q_ref, k_ref, v_ref, w_ref, col_ref, xc_ref, z_ref, c0_ref, n0_ref, gh_ref, skip_ref,
                  yb_ref, c_ref, n_ref):
    tb = q_ref.shape[0]
    L = chunk
    scale = DH_B ** -0.5

    @pl.when(pl.program_id(1) == 0)
    def _():
        c_ref[...] = c0_ref[...]
        n_ref[...] = n0_ref[...]

    causal = lax.broadcasted_iota(jnp.int32, (L, L), 1) <= lax.broadcasted_iota(jnp.int32, (L, L), 0)
    heads = range(NH_B)
    hsl = [slice(h * DH_B, (h + 1) * DH_B) for h in heads]
    for j in range(tb // L):
        rows = slice(j * L, (j + 1) * L)
        cols = col_ref[rows, :]
        col = lambda c, h: cols[:, c * NH_B + h:c * NH_B + h + 1]
        q = [q_ref[rows, hsl[h]] for h in heads]
        k = [k_ref[rows, hsl[h]] for h in heads]
        v = [v_ref[rows, hsl[h]] for h in heads]
        s = [_dot_nt(q[h], k[h]) for h in heads]
        qc = [_dot(q[h], c_ref[0, h].astype(BF16)) for h in heads]
        qn = [_dot_nt(q[h], jnp.broadcast_to(n_ref[0, h:h + 1, :], (8, DH_B)).astype(BF16))[:, 0:1] for h in heads]
        p = [(s[h] * scale * jnp.where(causal, jnp.exp(w_ref[j, h:h + 1, :] - col(0, h)), 0.0)).astype(BF16)
             for h in heads]
        pv = [_dot(p[h], v[h]) for h in heads]
        psum = [_dot(p[h], jnp.ones((L, LANES), BF16))[:, 0:1] for h in heads]
        for h in heads:
            sc = col(1, h)
            keep = sc[L - 1:L, :]
            kd = k[h].astype(F32) * (col(3, h) * scale)
            c_ref[0, h] = keep * c_ref[0, h] + _dot_tn(kd.astype(BF16), v[h])
            n_ref[0, h:h + 1, :] = keep * n_ref[0, h:h + 1, :] + jnp.sum(kd, axis=0, keepdims=True)
            num = pv[h] + sc * qc[h]
            den = psum[h] + sc * qn[h]
            hs = num / jnp.maximum(jnp.abs(den), col(2, h))
            mu = jnp.mean(hs, axis=1, keepdims=True)
            dev = hs - mu
            var = jnp.mean(dev * dev, axis=1, keepdims=True)
            hn = dev * lax.rsqrt(var + EPS) * gh_ref[:, hsl[h]]
            out = (hn + skip_ref[:, hsl[h]] * xc_ref[rows, hsl[h]]) * _silu(z_ref[rows, hsl[h]])
            yb_ref[rows, hsl[h]] = out.astype(BF16)


def _mlstm(q, k, v, w, col, xc, z, row0, seq, c0, n0, gh, skip):
    bsz = c0.shape[0]
    tb = min(seq, SEQ_BLOCK)
    chunk = tb
    nc, z_map, own_map = _seq_specs(bsz, seq, tb, row0)
    st4 = lambda bi, ci: (bi, 0, 0, 0)
    st3 = lambda bi, ci: (bi, 0, 0)
    return pl.pallas_call(
        functools.partial(_mlstm_kernel, chunk),
        grid=(bsz, nc),
        in_specs=[pl.BlockSpec((tb, D), own_map)] * 3
                 + [pl.BlockSpec((tb // chunk, NH_B, chunk), lambda bi, ci: (bi * nc + ci, 0, 0)),
                    pl.BlockSpec((tb, LANES), own_map),
                    pl.BlockSpec((tb, D), own_map), pl.BlockSpec((tb, D), z_map),
                    pl.BlockSpec((1, NH_B, DH_B, DH_B), st4), pl.BlockSpec((1, NH_B, DH_B), st3),
                    _full(gh.shape), _full(skip.shape)],
        out_specs=[pl.BlockSpec((tb, D), own_map), pl.BlockSpec((1, NH_B, DH_B, DH_B), st4),
                   pl.BlockSpec((1, NH_B, DH_B), st3)],
        out_shape=[jax.ShapeDtypeStruct((bsz * seq, D), BF16), jax.ShapeDtypeStruct(c0.shape, F32),
                   jax.ShapeDtypeStruct(n0.shape, F32)],
        compiler_params=_params(("parallel", "arbitrary")),
        name="mlstm",
    )(q, k, v, w, col, xc, z, c0, n0, gh, skip)


def _attn_kernel(masked, L, q_ref, kh_ref, ko_ref, vh_ref, vo_ref, bucket_ref, relb_ref, sink_ref,
                 o_ref, bias_s, s_scr, p_scr):
    nk = WINDOW + L
    ci = pl.program_id(1)

    @pl.when((pl.program_id(0) == 0) & (ci == 0))
    def _():
        bucket = bucket_ref[...]
        for h in range(N_HEADS):
            acc = jnp.zeros((L, nk), F32)
            for b in range(NUM_BUCKETS):
                acc = jnp.where(bucket == b, relb_ref[b, h], acc)
            bias_s[h * L:(h + 1) * L, :] = acc

    kall = jnp.concatenate([kh_ref[...], ko_ref[...]], axis=0)
    vall = jnp.concatenate([vh_ref[...], vo_ref[...]], axis=0)
    gw = GROUP * HEAD_DIM
    slot = lax.broadcasted_iota(jnp.int32, (L, gw), 1) // HEAD_DIM
    kk = lax.broadcasted_iota(jnp.int32, (1, nk), 1)
    gl = GROUP * L
    n_sub = q_ref.shape[0] // L
    hl = N_HEADS * L
    for u in range(n_sub):
        kcat = kall[u * L:u * L + nk, :]
        for g in range(N_KV):
            gsl = slice(g * gw, (g + 1) * gw)
            qg = q_ref[u * L:(u + 1) * L, gsl]
            zero = jnp.zeros_like(qg)
            qs = jnp.concatenate([jnp.where(slot == j, qg, zero) for j in range(GROUP)], axis=0)
            s_scr[u * hl + g * gl:u * hl + (g + 1) * gl, :] = _dot_nt(qs, kcat[:, gsl])
    for u in range(n_sub):
        if masked:
            valid = kk >= WINDOW - u * L - ci * (n_sub * L)
        for h in range(N_HEADS):
            rows = slice(u * hl + h * L, u * hl + (h + 1) * L)
            s = s_scr[rows, :] + bias_s[h * L:(h + 1) * L, :]
            if masked:
                s = jnp.where(valid, s, NEG_INF)
            sink = sink_ref[0, h]
            mx = jnp.maximum(jnp.max(s, axis=1, keepdims=True), sink)
            e = jnp.exp(s - mx)
            den = jnp.sum(e, axis=1, keepdims=True) + jnp.exp(sink - mx)
            p_scr[rows, :] = (e / den).astype(BF16)
    for u in range(n_sub):
        vcat = vall[u * L:u * L + nk, :]
        for g in range(N_KV):
            gsl = slice(g * gw, (g + 1) * gw)
            ow = _dot(p_scr[u * hl + g * gl:u * hl + (g + 1) * gl, :], vcat[:, gsl])
            og = jnp.where(slot == 0, ow[0:L, :], 0.0)
            for j in range(1, GROUP):
                og = jnp.where(slot == j, ow[j * L:(j + 1) * L, :], og)
            o_ref[u * L:(u + 1) * L, gsl] = og.astype(BF16)


def _attention(q, k_own, v_own, k_hist, v_hist, hist_map, row0, seq, bsz, masked, bucket, rel_bias, sinks):
    L = min(seq, CHUNK)
    lb = min(seq, ATTN_BLOCK)
    nc, in_map, out_map = _seq_specs(bsz, seq, lb, row0)
    n_sub = lb // L
    nk = WINDOW + L
    smem = functools.partial(pl.BlockSpec, memory_space=pltpu.SMEM)
    return pl.pallas_call(
        functools.partial(_attn_kernel, masked, L),
        grid=(bsz, nc),
        in_specs=[pl.BlockSpec((lb, D), in_map),
                  pl.BlockSpec((WINDOW, D), hist_map), pl.BlockSpec((lb, D), in_map),
                  pl.BlockSpec((WINDOW, D), hist_map), pl.BlockSpec((lb, D), in_map),
                  _full(bucket.shape), smem(), smem()],
        out_specs=pl.BlockSpec((lb, D), out_map),
        out_shape=jax.ShapeDtypeStruct((bsz * seq, D), BF16),
        scratch_shapes=[pltpu.VMEM((N_HEADS * L, nk), F32), pltpu.VMEM((n_sub * N_HEADS * L, nk), F32),
                        pltpu.VMEM((n_sub * N_HEADS * L, nk), BF16)],
        compiler_params=_params(("arbitrary", "arbitrary")),
        name="swa",
    )(q, k_hist, k_own, v_hist, v_own, bucket, rel_bias, sinks)


def _router_kernel(x_ref, g_ref, wr_ref, info_ref, cnt_ref):
    tm = x_ref.shape[0] // SUB

    @pl.when(pl.program_id(0) == 0)
    def _():
        cnt_ref[...] = jnp.zeros_like(cnt_ref)

    xn = _rms(_tile_rows_load(x_ref), g_ref[...])
    hi = xn.astype(BF16)
    lo = (xn - hi.astype(F32)).astype(BF16)
    w = wr_ref[...]
    whi = w.astype(BF16)
    wlo = (w - whi.astype(F32)).astype(BF16)
    logits = _dot(hi, whi) + _dot(hi, wlo) + _dot(lo, whi)
    lane = lax.broadcasted_iota(jnp.int32, (tm, LANES), 1)
    logits = jnp.where(lane < N_EXPERTS, logits, NEG_INF)
    m1 = jnp.max(logits, axis=1, keepdims=True)
    i1 = jnp.min(jnp.where(logits == m1, lane, LANES), axis=1, keepdims=True)
    rest = jnp.where(lane == i1, NEG_INF, logits)
    m2 = jnp.max(rest, axis=1, keepdims=True)
    i2 = jnp.min(jnp.where(rest == m2, lane, LANES), axis=1, keepdims=True)
    e2 = jnp.exp(m2 - m1)
    g1 = 1.0 / (1.0 + e2)
    g2 = e2 / (1.0 + e2)
    sel = (lane == i1) | (lane == i2)
    tri = (lax.broadcasted_iota(jnp.int32, (tm, tm), 0) > lax.broadcasted_iota(jnp.int32, (tm, tm), 1)).astype(BF16)
    rank = cnt_ref[...] + _dot(tri, sel.astype(BF16))
    r1 = jnp.sum(jnp.where(lane == i1, rank, 0.0), axis=1, keepdims=True)
    r2 = jnp.sum(jnp.where(lane == i2, rank, 0.0), axis=1, keepdims=True)
    cnt_ref[...] = cnt_ref[...] + jnp.sum(sel.astype(F32), axis=0, keepdims=True)
    info = jnp.where(lane == 0, i1.astype(F32), 0.0)
    info = jnp.where(lane == 1, i2.astype(F32), info)
    info = jnp.where(lane == 2, g1, info)
    info = jnp.where(lane == 3, g2, info)
    info = jnp.where(lane == 4, r1, info)
    info = jnp.where(lane == 5, r2, info)
    info_ref[...] = info


def _router(x, g, wr, tm):
    n = x.shape[0] // SUB
    return pl.pallas_call(
        _router_kernel,
        grid=(n // tm,),
        in_specs=[_tile_spec(tm), _full((1, D)), _full(wr.shape)],
        out_specs=[_rows(tm, LANES), _full((1, LANES))],
        out_shape=[jax.ShapeDtypeStruct((n, LANES), F32), jax.ShapeDtypeStruct((1, LANES), F32)],
        compiler_params=_params(("arbitrary",)),
        name="moe_router",
    )(x, g, wr)


def _token_tile(ref, t):
    return ref.at[pl.ds(pl.multiple_of(t * SUB, SUB), SUB)]


def _moe_scatter_kernel(te, zpos_ref, dest_ref, x_ref, xs_hbm, zero_s, sem):
    tm = x_ref.shape[0] // SUB
    span = te * SUB

    def rows_of(tile_start):
        return xs_hbm.at[pl.ds(pl.multiple_of(tile_start * SUB, span), span)]

    @pl.when(pl.program_id(0) == 0)
    def _():
        zero_s[...] = jnp.zeros_like(zero_s)
        for e in range(N_EXPERTS):
            pltpu.make_async_copy(zero_s, rows_of(zpos_ref[e]), sem).start()
        for e in range(N_EXPERTS):
            pltpu.make_async_copy(zero_s, rows_of(zpos_ref[e]), sem).wait()

        def clear_tail(t, carry):
            tail = pltpu.make_async_copy(zero_s, rows_of(t * te), sem)
            tail.start()
            tail.wait()
            return carry

        lax.fori_loop(zpos_ref[N_EXPERTS] // te, xs_hbm.shape[0] // span, clear_tail, 0)

    def copy(r, s):
        return pltpu.make_async_copy(_token_tile(x_ref, r), _token_tile(xs_hbm, dest_ref[s * tm + r]), sem)

    def start(r, carry):
        for s in range(2):
            copy(r, s).start(priority=s)
        return carry

    def wait(r, carry):
        for s in range(2):
            copy(r, s).wait()
        return carry

    lax.fori_loop(0, tm, start, 0)
    lax.fori_loop(0, tm, wait, 0)


def _moe_scatter(x, dest, zpos, n_rows, tm, te):
    n = x.shape[0] // SUB
    return pl.pallas_call(
        functools.partial(_moe_scatter_kernel, te),
        grid_spec=pltpu.PrefetchScalarGridSpec(
            num_scalar_prefetch=1,
            grid=(n // tm,),
            in_specs=[pl.BlockSpec((2 * tm,), lambda i, zp: (i,), memory_space=pltpu.SMEM),
                      pl.BlockSpec((tm * SUB, LANES), lambda i, zp: (i, 0))],
            out_specs=pl.BlockSpec(memory_space=pl.ANY),
            scratch_shapes=[pltpu.VMEM((te * SUB, LANES), F32), pltpu.SemaphoreType.DMA(())],
        ),
        out_shape=jax.ShapeDtypeStruct((n_rows * SUB, LANES), F32),
        compiler_params=_params(("arbitrary",)),
        name="moe_scatter",
    )(zpos, dest, x)


def _moe_ffn_kernel(n_ff, te, pe_ref, na_ref, xs_ref, g_ref, w1_ref, w3_ref, w2_ref, ys_ref, xn_s, acc_s):
    i, j = pl.program_id(0), pl.program_id(1)
    span = te * SUB
    for s in range(2):
        active = s < na_ref[i]
        rows = slice(s * te, (s + 1) * te)

        @pl.when(active & (j == 0))
        def _():
            xn_s[rows, :] = _rms(_tile_rows_load(xs_ref.at[pl.ds(s * span, span)]), g_ref[...]).astype(BF16)
            acc_s[rows, :] = jnp.zeros((te, D), F32)

        @pl.when(active)
        def _():
            xn = xn_s[rows, :]
            a = (_silu(_dot(xn, w1_ref[0])) * _dot(xn, w3_ref[0])).astype(BF16)
            acc_s[rows, :] += _dot(a, w2_ref[0])

        @pl.when(j == n_ff - 1)
        def _():
            _tile_rows_store(ys_ref.at[pl.ds(s * span, span)], jnp.where(active, acc_s[rows, :], 0.0))


def _moe_ffn(xs, g, w1, w3, w2, pair_expert, pair_active, te):
    n_pairs = pair_expert.shape[0]
    d_ff = w1.shape[2]
    n_ff = FF_SPLIT_MOE
    tf = d_ff // n_ff

    def ff(i, j, na_r):
        return jnp.where(na_r[i] > 0, j, n_ff - 1)

    rows = pl.BlockSpec((2 * te * SUB, LANES), lambda i, j, pe_r, na_r: (i, 0))
    return pl.pallas_call(
        functools.partial(_moe_ffn_kernel, n_ff, te),
        grid_spec=pltpu.PrefetchScalarGridSpec(
            num_scalar_prefetch=2,
            grid=(n_pairs, n_ff),
            in_specs=[rows,
                      pl.BlockSpec((1, D), lambda i, j, pe_r, na_r: (0, 0)),
                      pl.BlockSpec((1, D, tf), lambda i, j, pe_r, na_r: (pe_r[i], 0, ff(i, j, na_r))),
                      pl.BlockSpec((1, D, tf), lambda i, j, pe_r, na_r: (pe_r[i], 0, ff(i, j, na_r))),
                      pl.BlockSpec((1, tf, D), lambda i, j, pe_r, na_r: (pe_r[i], ff(i, j, na_r), 0))],
            out_specs=rows,
            scratch_shapes=[pltpu.VMEM((2 * te, D), BF16), pltpu.VMEM((2 * te, D), F32)],
        ),
        out_shape=jax.ShapeDtypeStruct((n_pairs * 2 * te * SUB, LANES), F32),
        compiler_params=_params(("arbitrary", "arbitrary")),
        name="moe_ffn",
    )(pair_expert, pair_active, xs, g, w1, w3, w2)


def _moe_combine_kernel(n_p, n_steps, dest_ref, next_dest_ref, info_ref, x_ref, ys_hbm, op_ref, os_ref, buf, sems):
    tm = x_ref.shape[0] // SUB
    i = pl.program_id(0)
    slot = i % 2

    def copy(idx_ref, half, r, s):
        return pltpu.make_async_copy(_token_tile(ys_hbm, idx_ref[s * tm + r]), _token_tile(buf.at[half, s], r),
                                     sems.at[half])

    def start_all(idx_ref, half):
        def body(r, carry):
            for s in range(2):
                copy(idx_ref, half, r, s).start(priority=s)
            return carry
        lax.fori_loop(0, tm, body, 0)

    @pl.when(i == 0)
    def _():
        start_all(dest_ref, 0)

    @pl.when(i + 1 < n_steps)
    def _():
        start_all(next_dest_ref, 1 - slot)

    def wait(r, carry):
        for s in range(2):
            copy(dest_ref, slot, r, s).wait()
        return carry

    lax.fori_loop(0, tm, wait, 0)
    out = (_tile_rows_load(x_ref) + info_ref[:, 2:3] * _tile_rows_load(buf.at[slot, 0])
           + info_ref[:, 3:4] * _tile_rows_load(buf.at[slot, 1]))

    @pl.when(i < n_p)
    def _():
        op_ref[...] = out

    @pl.when(i >= n_p)
    def _():
        os_ref[...] = out


def _moe_combine(x, info, dest, ys, tm, n_p, n_s):
    out_p, out_s = _pair(tm, D, n_p)
    n_steps = n_p + n_s
    return pl.pallas_call(
        functools.partial(_moe_combine_kernel, n_p, n_steps),
        grid=(n_steps,),
        in_specs=[pl.BlockSpec((2 * tm,), lambda i: (i,), memory_space=pltpu.SMEM),
                  pl.BlockSpec((2 * tm,), lambda i: (jnp.minimum(i + 1, n_steps - 1),), memory_space=pltpu.SMEM),
                  _rows(tm, LANES), _tile_spec(tm), pl.BlockSpec(memory_space=pl.ANY)],
        out_specs=[out_p, out_s],
        out_shape=[jax.ShapeDtypeStruct((n_p * tm, D), F32), jax.ShapeDtypeStruct((n_s * tm, D), F32)],
        scratch_shapes=[pltpu.VMEM((2, 2, tm * SUB, LANES), F32), pltpu.SemaphoreType.DMA((2,))],
        compiler_params=_params(("arbitrary",)),
        name="moe_combine",
    )(dest, dest, info, x, ys)


def _moe(x, g, w_router, w1, w3, w2, tm, n_p, n_s):
    n = x.shape[0] // SUB
    te = min(EXPERT_TILE, tm)
    wr = jnp.zeros((D, LANES), F32).at[:, :N_EXPERTS].set(w_router)
    info, cnt = _router(x, g, wr, tm)
    e1, e2 = info[:, 0].astype(jnp.int32), info[:, 1].astype(jnp.int32)
    r1, r2 = info[:, 4].astype(jnp.int32), info[:, 5].astype(jnp.int32)
    counts = cnt[0, :N_EXPERTS].astype(jnp.int32)
    region = 2 * te
    padded = (counts + region - 1) // region * region
    ends = jnp.cumsum(padded)
    starts = ends - padded
    dest = jnp.stack([(starts[e1] + r1).reshape(n // tm, tm), (starts[e2] + r2).reshape(n // tm, tm)], axis=1)
    dest = dest.reshape(-1)
    n_pairs = (2 * n + N_EXPERTS * (region - 1)) // region
    pair_start = jnp.arange(n_pairs, dtype=jnp.int32) * region
    pair_expert = jnp.minimum(jnp.sum(pair_start[:, None] >= ends[None, :], axis=1), N_EXPERTS - 1).astype(jnp.int32)
    used_end = starts + (counts + te - 1) // te * te
    pair_active = jnp.clip((used_end[pair_expert] - pair_start) // te, 0, 2).astype(jnp.int32)
    last_region = jnp.maximum(ends - region, 0).astype(jnp.int32)
    xs = _moe_scatter(x, dest, jnp.concatenate([last_region, ends[-1:].astype(jnp.int32)]), n_pairs * region, tm,
                      region)
    ys = _moe_ffn(xs, g, w1, w3, w2, pair_expert, pair_active, te)
    return _moe_combine(x, info, dest, ys, tm, n_p, n_s)


def _t5_bucket(rel):
    n = -rel
    half = NUM_BUCKETS // 2
    ret = jnp.where(n < 0, half, 0)
    n = jnp.abs(n)
    max_exact = half // 2
    nf = jnp.maximum(n, 1).astype(F32)
    large = max_exact + (jnp.log(nf / max_exact) / math.log(MAX_DISTANCE / max_exact)
                         * (half - max_exact)).astype(jnp.int32)
    large = jnp.minimum(large, half - 1)
    return ret + jnp.where(n < max_exact, n, large)


def _bucket_table(length):
    kpos = jnp.arange(WINDOW + length) - WINDOW
    return _t5_bucket(kpos[None, :] - jnp.arange(length)[:, None]).astype(jnp.int32)


def kernel(x_prompt, x_sample, state_conv_a, state_rglru_h, state_conv_b, state_mlstm_c, state_mlstm_n, state_mlstm_m, cache_swa_k, cache_swa_v, norm_mix, norm_ffn, w_in_ab, w_conv_a, b_conv_a, w_rg_a, b_rg_a, w_rg_x, b_rg_x, rg_lambda, w_conv_b, b_conv_b, w_q_b, w_k_b, w_v_b, w_if_b, b_if_b, g_hnorm_b, skip_b, w_out_ab, w1_dense, w3_dense, w2_dense, w_in_att, g_qnorm, g_knorm, sinks, w_out_att, rel_bias, w_router, w1_moe, w3_moe, w2_moe):
    bp, tp, _ = x_prompt.shape
    bs, ts, _ = x_sample.shape
    assert norm_mix.shape[0] == 2 and w_in_ab.shape[0] == 1 and w_in_att.shape[0] == 1
    assert tp % CHUNK == 0 and ts <= CHUNK and cache_swa_k.shape[2] == WINDOW
    rows_p, rows_s = bp * tp, bs * ts
    tm = math.gcd(math.gcd(rows_p, rows_s), TOKEN_TILE)
    n_p, n_s = rows_p // tm, rows_s // tm
    xp = x_prompt.reshape(rows_p, D)
    xs = x_sample.reshape(rows_s, D)
    bf = lambda w: w.astype(BF16)
    vec = lambda v: v.reshape(1, -1)

    xa, ga, xm, z = _in_proj_ab(xp, xs, vec(norm_mix[0]), bf(w_in_ab[0]), tm)
    rg_w = (w_conv_a[0], vec(b_conv_a[0]), bf(w_rg_a[0]), vec(b_rg_a[0]), bf(w_rg_x[0]), vec(b_rg_x[0]),
            vec(rg_lambda[0]))
    ya_p, hl_p = _rglru(xa, ga, 0, tp, jnp.zeros((bp, CONV_W - 1, D), F32), jnp.zeros((bp, 1, D), F32), *rg_w)
    ya_s, hl_s = _rglru(xa, ga, rows_p, ts, state_conv_a[0], state_rglru_h[0].reshape(bs, 1, D), *rg_w)

    def gate_cols(w):
        pad = ((0, 0), (0, LANES - NH_B))
        return jnp.concatenate([jnp.pad(w[:, :NH_B], pad), jnp.pad(w[:, NH_B:], pad)], axis=1)

    pre_w = (w_conv_b[0], vec(b_conv_b[0]), bf(w_q_b[0]), bf(w_k_b[0]), bf(w_v_b[0]), bf(w_if_b[0].T),
             b_if_b[0].reshape(2 * NH_B, 1), bf(gate_cols(w_if_b[0])), gate_cols(b_if_b[0][None]))
    rec_w = (g_hnorm_b[0].reshape(1, D), vec(skip_b[0]))
    lanes = lambda m: jnp.pad(m, ((0, 0), (0, LANES - NH_B)))[:, None, :]
    pre_p = _mlstm_pre(xm, 0, tp, jnp.zeros((bp, CONV_W - 1, D), F32), jnp.zeros((bp, 1, LANES), F32), *pre_w)
    pre_s = _mlstm_pre(xm, rows_p, ts, state_conv_b[0], lanes(state_mlstm_m[0]), *pre_w)
    m_p, m_s = pre_p[6], pre_s[6]
    yb_p, c_p, nn_p = _mlstm(*pre_p[1:6], pre_p[0], z, 0, tp, jnp.zeros((bp, NH_B, DH_B, DH_B), F32),
                             jnp.zeros((bp, NH_B, DH_B), F32), *rec_w)
    yb_s, c_s, nn_s = _mlstm(*pre_s[1:6], pre_s[0], z, rows_p, ts, state_mlstm_c[0], state_mlstm_n[0], *rec_w)

    w_out = bf(w_out_ab[0])
    y = _out_proj((xp, xs), [(ya_p, ya_s), (yb_p, yb_s)], [w_out[:D], w_out[D:]], tm, n_p)
    y = _ffn_dense(y, vec(norm_ffn[0]), bf(w1_dense[0]), bf(w3_dense[0]), bf(w2_dense[0]), tm)

    tile2 = lambda gain: jnp.tile(gain, D_KV // HEAD_DIM).reshape(1, D_KV)
    q, kw, vw, k, v = _in_proj_att(y, vec(norm_mix[1]), bf(w_in_att[0]), tile2(g_qnorm[0]), tile2(g_knorm[0]), tm)
    sinks2 = sinks[0].reshape(1, N_HEADS)
    wpb = min(tp, ATTN_BLOCK) // WINDOW
    assert wpb >= 1 and tp % (wpb * WINDOW) == 0
    hist_p = lambda bi, ci: (bi * (tp // WINDOW) + jnp.maximum(ci * wpb - 1, 0), 0)
    o_p = _attention(q, kw, vw, kw, vw, hist_p, 0, tp, bp, True, _bucket_table(CHUNK), rel_bias, sinks2)

    def widen(cache):
        wide = jnp.broadcast_to(cache[:, :, :, None, :], (bs, WINDOW, N_KV, GROUP, HEAD_DIM))
        return wide.reshape(bs * WINDOW, D).astype(BF16)

    ck, cv = widen(cache_swa_k[0]), widen(cache_swa_v[0])
    hist_s = lambda bi, ci: (bi, 0)
    o_s = _attention(q, kw, vw, ck, cv, hist_s, rows_p, ts, bs, False, _bucket_table(ts), rel_bias, sinks2)
    y = _out_proj(y, [(o_p, o_s)], [bf(w_out_att[0])], tm, n_p, tiled_out=True)
    yp, ys = _moe(y, vec(norm_ffn[1]), w_router[0], bf(w1_moe[0]), bf(w3_moe[0]), bf(w2_moe[0]), tm, n_p, n_s)

    def tail(a, rows, b, t, keep):
        if rows == 0:
            return jnp.stack([a[(i + 1) * t - keep:(i + 1) * t] for i in range(b)])
        return a[rows:rows + b * t].reshape(b, t, -1)[:, t - keep:]

    kv4 = lambda a, b: a.reshape(b, -1, N_KV, HEAD_DIM)
    one = lambda a: a[None]
    k_s = jnp.concatenate([cache_swa_k[0][:, ts:], kv4(k[rows_p:], bs)], axis=1)
    v_s = jnp.concatenate([cache_swa_v[0][:, ts:], kv4(v[rows_p:], bs)], axis=1)
    return (yp.reshape(bp, tp, D), ys.reshape(bs, ts, D),
            one(tail(xa, 0, bp, tp, CONV_W - 1)), one(hl_p.reshape(bp, D)), one(tail(xm, 0, bp, tp, CONV_W - 1)),
            one(c_p), one(nn_p), one(m_p[:, 0, :NH_B]),
            one(kv4(tail(k, 0, bp, tp, WINDOW), bp)), one(kv4(tail(v, 0, bp, tp, WINDOW), bp)),
            one(tail(xa, rows_p, bs, ts, CONV_W - 1)), one(hl_s.reshape(bs, D)),
            one(tail(xm, rows_p, bs, ts, CONV_W - 1)), one(c_s), one(nn_s), one(m_s[:, 0, :NH_B]),
            one(k_s), one(v_s))
```

```python
import functools
import math

import jax
import jax.numpy as jnp
from jax import lax
from jax.experimental import pallas as pl
from jax.experimental.pallas import tpu as pltpu

F32 = jnp.float32
BF16 = jnp.bfloat16

D = 1024
CHUNK = 64
CONV_W = 4
NB_A = 8
BS_A = D // NB_A
RG_C = 8.0
NH_B = 4
DH_B = D // NH_B
N_HEADS = 16
HEAD_DIM = D // N_HEADS
N_KV = 4
GROUP = N_HEADS // N_KV
D_KV = N_KV * HEAD_DIM
WINDOW = 128
NUM_BUCKETS = 32
MAX_DISTANCE = 128
N_EXPERTS = 8
EPS = 1e-6
LANES = 128
SUB = D // LANES
NEG_INF = float("-inf")
MIN_NORMAL = 1.1754944e-38

TOKEN_TILE = 512
EXPERT_TILE = 512
FF_SPLIT_DENSE = 3
FF_SPLIT_MOE = 2
SEQ_BLOCK = 256
RGLRU_BLOCK = 512
SCAN_ROWS = 8
ATTN_BLOCK = 256
VMEM_LIMIT = 56 * 1024 * 1024


def _params(sem):
    return pltpu.CompilerParams(dimension_semantics=sem, vmem_limit_bytes=VMEM_LIMIT)


def _full(shape):
    return pl.BlockSpec(shape, lambda *_: (0,) * len(shape))


def _resident(shape):
    return pl.BlockSpec(shape, lambda *_: (0,) * len(shape), pipeline_mode=pl.Buffered(1))


def _rows(tm, c):
    return pl.BlockSpec((tm, c), lambda i: (i, 0))


def _pair(tm, c, n_p):
    return [pl.BlockSpec((tm, c), lambda i: (jnp.minimum(i, n_p - 1), 0)),
            pl.BlockSpec((tm, c), lambda i: (jnp.maximum(i - n_p, 0), 0))]


def _rms(x, g):
    ms = jnp.mean(x * x, axis=-1, keepdims=True)
    return x * lax.rsqrt(ms + EPS) * g


def _silu(x):
    return x * jax.nn.sigmoid(x)


def _sigmoid_tanh(x):
    return 0.5 * jnp.tanh(0.5 * x) + 0.5


def _softplus(x):
    return jnp.maximum(x, 0.0) + jnp.log1p(jnp.exp(-jnp.abs(x)))


def _gelu_tanh(x):
    c = math.sqrt(2.0 / math.pi)
    return x * (0.5 * (1.0 + jnp.tanh(c * (x + 0.044715 * (x * x * x)))))


def _dot(a, b):
    return jnp.dot(a, b, preferred_element_type=F32)


def _dot_nt(a, b):
    return lax.dot_general(a, b, (((1,), (1,)), ((), ())), preferred_element_type=F32)


def _dot_tn(a, b):
    return lax.dot_general(a, b, (((0,), (0,)), ((), ())), preferred_element_type=F32)


def _in_proj_ab_kernel(n_p, xp_ref, xs_ref, g_ref, w_ref, *o_refs):
    i = pl.program_id(0)
    x = jnp.where(i < n_p, xp_ref[...], xs_ref[...])
    xn = _rms(x, g_ref[...]).astype(BF16)
    for c, o_ref in enumerate(o_refs):
        o_ref[...] = _dot(xn, w_ref[:, c * D:(c + 1) * D])


def _in_proj_ab(xp, xs, g, w, tm):
    n_p, n_s = xp.shape[0] // tm, xs.shape[0] // tm
    n = xp.shape[0] + xs.shape[0]
    n_out = w.shape[1] // D
    return pl.pallas_call(
        functools.partial(_in_proj_ab_kernel, n_p),
        grid=(n_p + n_s,),
        in_specs=_pair(tm, D, n_p) + [_full((1, D)), _resident(w.shape)],
        out_specs=[_rows(tm, D)] * n_out,
        out_shape=[jax.ShapeDtypeStruct((n, D), F32)] * n_out,
        compiler_params=_params(("parallel",)),
        name="in_proj_ab",
    )(xp, xs, g, w)


def _tile_rows_load(ref):
    tm = ref.shape[0] // SUB
    return jnp.concatenate([ref[pl.ds(s, tm, stride=SUB), :] for s in range(SUB)], axis=1)


def _tile_rows_store(ref, val):
    tm = val.shape[0]
    for s in range(SUB):
        ref[pl.ds(s, tm, stride=SUB), :] = val[:, s * LANES:(s + 1) * LANES]


def _tile_spec(tm):
    return pl.BlockSpec((tm * SUB, LANES), lambda i: (i, 0))


def _out_proj_kernel(n_p, n_res, n_a, tiled_out, *refs):
    i = pl.program_id(0)
    res_refs = refs[:n_res]
    a_refs = refs[n_res:n_res + 2 * n_a]
    w_refs = refs[n_res + 2 * n_a:n_res + 3 * n_a]
    o_ref = refs[-1]
    if n_res == 2:
        acc = jnp.where(i < n_p, res_refs[0][...], res_refs[1][...])
    else:
        acc = res_refs[0][...]
    for k in range(n_a):
        a = jnp.where(i < n_p, a_refs[2 * k][...], a_refs[2 * k + 1][...])
        acc = acc + _dot(a, w_refs[k][...])
    if tiled_out:
        _tile_rows_store(o_ref, acc)
    else:
        o_ref[...] = acc


def _out_proj(res, a_pairs, ws, tm, n_p, tiled_out=False):
    res = res if isinstance(res, tuple) else (res,)
    n = sum(a.shape[0] for a in a_pairs[0])
    res_specs = _pair(tm, D, n_p) if len(res) == 2 else [_rows(tm, D)]
    a_specs = []
    for a in a_pairs:
        a_specs += _pair(tm, a[0].shape[1], n_p)
    return pl.pallas_call(
        functools.partial(_out_proj_kernel, n_p, len(res), len(a_pairs), tiled_out),
        grid=(n // tm,),
        in_specs=res_specs + a_specs + [_resident(w.shape) for w in ws],
        out_specs=_tile_spec(tm) if tiled_out else _rows(tm, D),
        out_shape=jax.ShapeDtypeStruct((n * SUB, LANES) if tiled_out else (n, D), F32),
        compiler_params=_params(("parallel",)),
        name="out_proj",
    )(*res, *[x for a in a_pairs for x in a], *ws)


def _ffn_dense_kernel(x_ref, g_ref, w1_ref, w3_ref, w2_ref, o_ref):
    x = x_ref[...]
    xn = _rms(x, g_ref[...]).astype(BF16)
    step = w1_ref.shape[1] // FF_SPLIT_DENSE
    acc = x
    for c in range(FF_SPLIT_DENSE):
        sl = slice(c * step, (c + 1) * step)
        a = (_silu(_dot(xn, w1_ref[:, sl])) * _dot(xn, w3_ref[:, sl])).astype(BF16)
        acc = acc + _dot(a, w2_ref[sl, :])
    o_ref[...] = acc


def _ffn_dense(x, g, w1, w3, w2, tm):
    n = x.shape[0]
    return pl.pallas_call(
        _ffn_dense_kernel,
        grid=(n // tm,),
        in_specs=[_rows(tm, D), _full((1, D)), _resident(w1.shape), _resident(w3.shape), _resident(w2.shape)],
        out_specs=_rows(tm, D),
        out_shape=jax.ShapeDtypeStruct((n, D), F32),
        compiler_params=_params(("parallel",)),
        name="ffn_dense",
    )(x, g, w1, w3, w2)


def _in_proj_att_kernel(x_ref, g_ref, w_ref, gq_ref, gk_ref, q_ref, kw_ref, vw_ref, k_ref, v_ref):
    xn = _rms(x_ref[...], g_ref[...]).astype(BF16)
    cw = D_KV
    r = lax.broadcasted_iota(jnp.int32, (cw, cw), 0) // HEAD_DIM
    c = lax.broadcasted_iota(jnp.int32, (cw, cw), 1) // HEAD_DIM
    group_ones = (r == c).astype(BF16)
    er = lax.broadcasted_iota(jnp.int32, (D_KV, D), 0)
    ec = lax.broadcasted_iota(jnp.int32, (D_KV, D), 1)
    widen = ((er // HEAD_DIM == ec // (GROUP * HEAD_DIM)) & (er % HEAD_DIM == ec % HEAD_DIM)).astype(BF16)

    def head_norm(y, gain):
        ms = _dot((y * y).astype(BF16), group_ones) * (1.0 / HEAD_DIM)
        return y * lax.rsqrt(ms + EPS) * gain

    for b in range(D // cw):
        y = _dot(xn, w_ref[:, b * cw:(b + 1) * cw])
        q_ref[:, b * cw:(b + 1) * cw] = (head_norm(y, gq_ref[...]) * (HEAD_DIM ** -0.5)).astype(BF16)
    k = head_norm(_dot(xn, w_ref[:, D:D + D_KV]), gk_ref[...])
    v = _dot(xn, w_ref[:, D + D_KV:D + 2 * D_KV])
    k_ref[...] = k
    v_ref[...] = v
    kw_ref[...] = _dot(k.astype(BF16), widen).astype(BF16)
    vw_ref[...] = _dot(v.astype(BF16), widen).astype(BF16)


def _in_proj_att(x, g, w, gq, gk, tm):
    n = x.shape[0]
    return pl.pallas_call(
        _in_proj_att_kernel,
        grid=(n // tm,),
        in_specs=[_rows(tm, D), _full((1, D)), _resident(w.shape), _full((1, D_KV)), _full((1, D_KV))],
        out_specs=[_rows(tm, D)] * 3 + [_rows(tm, D_KV)] * 2,
        out_shape=[jax.ShapeDtypeStruct((n, D), BF16)] * 3 + [jax.ShapeDtypeStruct((n, D_KV), F32)] * 2,
        compiler_params=_params(("parallel",)),
        name="in_proj_att",
    )(x, g, w, gq, gk)


def _causal_conv(x, xbuf, conv0_ref, wc_ref, bc_ref, first):
    tb = x.shape[0]

    @pl.when(first)
    def _():
        xbuf[5:8, :] = conv0_ref[0]

    xbuf[8:8 + tb, :] = x
    y = bc_ref[...] + xbuf[5:5 + tb, :] * wc_ref[0:1, :]
    for j in range(1, CONV_W):
        y = y + xbuf[5 + j:5 + j + tb, :] * wc_ref[j:j + 1, :]
    xbuf[5:8, :] = xbuf[5 + tb:8 + tb, :]
    return y


def _seq_specs(bsz, seq, tb, row0):
    nc = seq // tb
    off = row0 // tb
    return nc, (lambda bi, ci: (off + bi * nc + ci, 0)), (lambda bi, ci: (bi * nc + ci, 0))


def _scan_by_doubling(a, u, pos, axis, length):
    d = 1
    while d < length:
        keep = pos >= d
        u = jnp.where(keep, a * pltpu.roll(u, d, axis) + u, u)
        a = jnp.where(keep, a * pltpu.roll(a, d, axis), a)
        d *= 2
    return a, u


def _linear_scan(a, u, h0, row):
    tb = a.shape[0]
    sub = min(tb, SCAN_ROWS)
    out = []
    for k in range(tb // sub):
        rows = slice(k * sub, (k + 1) * sub)
        a_k, u_k = _scan_by_doubling(a[rows, :], u[rows, :], row[0:sub, :], 0, sub)
        h_k = a_k * h0 + u_k
        h0 = h_k[sub - 1:sub, :]
        out.append(h_k)
    return jnp.concatenate(out, axis=0)


def _rglru_kernel(xa_ref, ga_ref, conv0_ref, h0_ref, wc_ref, bc_ref, wa_ref, ba_ref, wx_ref, bx_ref, lam_ref,
                  ya_ref, hl_ref, xbuf):
    first = pl.program_id(1) == 0
    tb = xa_ref.shape[0]

    @pl.when(first)
    def _():
        hl_ref[0] = h0_ref[0]

    y = _causal_conv(xa_ref[...], xbuf, conv0_ref, wc_ref, bc_ref, first)
    row = lax.broadcasted_iota(jnp.int32, (tb, BS_A), 0)
    for n in range(NB_A):
        sl = slice(n * BS_A, (n + 1) * BS_A)
        yn = y[:, sl]
        yb = yn.astype(BF16)
        r = _sigmoid_tanh(_dot(yb, wa_ref[n]) + ba_ref[:, sl])
        ig = _sigmoid_tanh(_dot(yb, wx_ref[n]) + bx_ref[:, sl])
        log_a = (-RG_C) * r * _softplus(-lam_ref[:, sl])
        a = jnp.exp(log_a)
        th = jnp.tanh(log_a)
        gain2 = -2.0 * th / (1.0 - th)
        u = yn * ig * (gain2 * lax.rsqrt(jnp.maximum(gain2, MIN_NORMAL)))
        h = _linear_scan(a, u, hl_ref[0, :, sl], row)
        hl_ref[0, :, sl] = h[tb - 1:tb, :]
        ya_ref[:, sl] = (h * _gelu_tanh(ga_ref[:, sl])).astype(BF16)


def _rglru(xa, ga, row0, seq, conv0, h0, wc, bc, wa, ba, wx, bx, lam):
    bsz = conv0.shape[0]
    tb = math.gcd(seq, RGLRU_BLOCK)
    nc, in_map, out_map = _seq_specs(bsz, seq, tb, row0)
    state3 = lambda bi, ci: (bi, 0, 0)
    return pl.pallas_call(
        _rglru_kernel,
        grid=(bsz, nc),
        in_specs=[pl.BlockSpec((tb, D), in_map), pl.BlockSpec((tb, D), in_map),
                  pl.BlockSpec((1, CONV_W - 1, D), state3), pl.BlockSpec((1, 1, D), state3),
                  _full(wc.shape), _full(bc.shape), _full(wa.shape), _full(ba.shape), _full(wx.shape),
                  _full(bx.shape), _full(lam.shape)],
        out_specs=[pl.BlockSpec((tb, D), out_map), pl.BlockSpec((1, 1, D), state3)],
        out_shape=[jax.ShapeDtypeStruct((bsz * seq, D), BF16), jax.ShapeDtypeStruct((bsz, 1, D), F32)],
        scratch_shapes=[pltpu.VMEM((tb + 8, D), F32)],
        compiler_params=_params(("parallel", "arbitrary")),
        name="rglru",
    )(xa, ga, conv0, h0, wc, bc, wa, ba, wx, bx, lam)


def _mlstm_pre_kernel(chunk, xm_ref, conv0_ref, m0_ref, wc_ref, bc_ref, wq_ref, wk_ref, wv_ref, wif_ref, bif_ref,
                      wifc_ref, bifc_ref, xc_ref, q_ref, k_ref, v_ref, w_ref, col_ref, m_ref, xbuf):
    first = pl.program_id(1) == 0
    tb = xm_ref.shape[0]
    L = chunk

    @pl.when(first)
    def _():
        m_ref[...] = m0_ref[...]

    x = xm_ref[...]
    xc = _silu(_causal_conv(x, xbuf, conv0_ref, wc_ref, bc_ref, first))
    xc_ref[...] = xc
    g_row = bif_ref[...]
    g_col = bifc_ref[...]
    for h in range(NH_B):
        sl = slice(h * DH_B, (h + 1) * DH_B)
        xch = xc[:, sl].astype(BF16)
        q = _dot(xch, wq_ref[h]).astype(BF16)
        k = _dot(xch, wk_ref[h]).astype(BF16)
        v = _dot(x[:, sl].astype(BF16), wv_ref[h]).astype(BF16)
        q_ref[:, sl] = q
        k_ref[:, sl] = k
        v_ref[:, sl] = v
        for part, val in enumerate((q, k, v)):
            psl = slice(part * D + h * DH_B, part * D + (h + 1) * DH_B)
            g_row = g_row + _dot_nt(wif_ref[:, psl], val)
            g_col = g_col + _dot(val, wifc_ref[psl, :])

    def exact_dot(a, b01):
        p1 = a.astype(BF16)
        r1 = a - p1.astype(F32)
        p2 = r1.astype(BF16)
        p3 = (r1 - p2.astype(F32)).astype(BF16)
        return _dot(p1, b01) + _dot(p2, b01) + _dot(p3, b01)

    def exact_dot_left(b01, a):
        p1 = a.astype(BF16)
        r1 = a - p1.astype(F32)
        p2 = r1.astype(BF16)
        p3 = (r1 - p2.astype(F32)).astype(BF16)
        return _dot(b01, p1) + _dot(b01, p2) + _dot(b01, p3)

    tt = lax.broadcasted_iota(jnp.int32, (L, L), 0)
    ss = lax.broadcasted_iota(jnp.int32, (L, L), 1)
    upper = (tt <= ss).astype(BF16)
    lower = (ss <= tt).astype(BF16)
    i_row = g_row[0:NH_B, :]
    b_row = exact_dot(-_softplus(-g_row), upper)[NH_B:2 * NH_B, :]
    w_ref[0] = i_row - b_row
    i_col = g_col[:, 0:LANES]
    b_col = exact_dot_left(lower, -_softplus(-g_col[:, LANES:2 * LANES]))
    w_col = i_col - b_col
    trow = lax.broadcasted_iota(jnp.int32, (L, LANES), 0)
    run_max = w_col
    d = 1
    while d < L:
        run_max = jnp.where(trow >= d, jnp.maximum(run_max, pltpu.roll(run_max, d, 0)), run_max)
        d *= 2
    m_prev = m_ref[0]
    big_m = jnp.maximum(m_prev, run_max)
    m_last = big_m[L - 1:L, :]
    m_ref[0] = b_col[L - 1:L, :] + m_last
    head_lane = lax.broadcasted_iota(jnp.int32, (L, LANES), 1) < NH_B
    table = jnp.where(head_lane, big_m, 0.0)
    for c, col in enumerate((jnp.exp(m_prev - big_m), jnp.exp(-(b_col + big_m)), jnp.exp(w_col - m_last))):
        table = table + pltpu.roll(jnp.where(head_lane, col, 0.0), (c + 1) * NH_B, 1)
    col_ref[...] = table


def _mlstm_pre(xm, row0, seq, conv0, m0, wc, bc, wq, wk, wv, wif_t, bif, wif_c, bif_c):
    bsz = conv0.shape[0]
    tb = min(seq, SEQ_BLOCK)
    chunk = tb
    nc, in_map, out_map = _seq_specs(bsz, seq, tb, row0)
    n = bsz * seq
    st3 = lambda bi, ci: (bi, 0, 0)
    return pl.pallas_call(
        functools.partial(_mlstm_pre_kernel, chunk),
        grid=(bsz, nc),
        in_specs=[pl.BlockSpec((tb, D), in_map), pl.BlockSpec((1, CONV_W - 1, D), st3),
                  pl.BlockSpec((1, 1, LANES), st3),
                  _full(wc.shape), _full(bc.shape), _full(wq.shape), _full(wk.shape), _full(wv.shape),
                  _full(wif_t.shape), _full(bif.shape), _full(wif_c.shape), _full(bif_c.shape)],
        out_specs=[pl.BlockSpec((tb, D), out_map)] * 4
                  + [pl.BlockSpec((tb // chunk, NH_B, chunk), lambda bi, ci: (bi * nc + ci, 0, 0)),
                     pl.BlockSpec((tb, LANES), out_map), pl.BlockSpec((1, 1, LANES), st3)],
        out_shape=[jax.ShapeDtypeStruct((n, D), F32)] + [jax.ShapeDtypeStruct((n, D), BF16)] * 3
                  + [jax.ShapeDtypeStruct((n // chunk, NH_B, chunk), F32), jax.ShapeDtypeStruct((n, LANES), F32),
                     jax.ShapeDtypeStruct(m0.shape, F32)],
        scratch_shapes=[pltpu.VMEM((tb + 8, D), F32)],
        compiler_params=_params(("parallel", "arbitrary")),
        name="mlstm_pre",
    )(xm, conv0, m0, wc, bc, wq, wk, wv, wif_t, bif, wif_c, bif_c)


def _mlstm_kernel(chunk, q_ref, k_ref, v_ref, w_ref, col_ref, xc_ref, z_ref, c0_ref, n0_ref, gh_ref, skip_ref,
                  yb_ref, c_ref, n_ref):
    tb = q_ref.shape[0]
    L = chunk
    scale = DH_B ** -0.5

    @pl.when(pl.program_id(1) == 0)
    def _():
        c_ref[...] = c0_ref[...]
        n_ref[...] = n0_ref[...]

    causal = lax.broadcasted_iota(jnp.int32, (L, L), 1) <= lax.broadcasted_iota(jnp.int32, (L, L), 0)
    heads = range(NH_B)
    hsl = [slice(h * DH_B, (h + 1) * DH_B) for h in heads]
    for j in range(tb // L):
        rows = slice(j * L, (j + 1) * L)
        cols = col_ref[rows, :]
        col = lambda c, h: cols[:, c * NH_B + h:c * NH_B + h + 1]
        q = [q_ref[rows, hsl[h]] for h in heads]
        k = [k_ref[rows, hsl[h]] for h in heads]
        v = [v_ref[rows, hsl[h]] for h in heads]
        s = [_dot_nt(q[h], k[h]) for h in heads]
        qc = [_dot(q[h], c_ref[0, h].astype(BF16)) for h in heads]
        qn = [_dot_nt(q[h], jnp.broadcast_to(n_ref[0, h:h + 1, :], (8, DH_B)).astype(BF16))[:, 0:1] for h in heads]
        p = [(s[h] * scale * jnp.where(causal, jnp.exp(w_ref[j, h:h + 1, :] - col(0, h)), 0.0)).astype(BF16)
             for h in heads]
        pv = [_dot(p[h], v[h]) for h in heads]
        psum = [_dot(p[h], jnp.ones((L, LANES), BF16))[:, 0:1] for h in heads]
        for h in heads:
            sc = col(1, h)
            keep = sc[L - 1:L, :]
            kd = k[h].astype(F32) * (col(3, h) * scale)
            c_ref[0, h] = keep * c_ref[0, h] + _dot_tn(kd.astype(BF16), v[h])
            n_ref[0, h:h + 1, :] = keep * n_ref[0, h:h + 1, :] + jnp.sum(kd, axis=0, keepdims=True)
            num = pv[h] + sc * qc[h]
            den = psum[h] + sc * qn[h]
            hs = num / jnp.maximum(jnp.abs(den), col(2, h))
            mu = jnp.mean(hs, axis=1, keepdims=True)
            dev = hs - mu
            var = jnp.mean(dev * dev, axis=1, keepdims=True)
            hn = dev * lax.rsqrt(var + EPS) * gh_ref[:, hsl[h]]
            out = (hn + skip_ref[:, hsl[h]] * xc_ref[rows, hsl[h]]) * _silu(z_ref[rows, hsl[h]])
            yb_ref[rows, hsl[h]] = out.astype(BF16)


def _mlstm(q, k, v, w, col, xc, z, row0, seq, c0, n0, gh, skip):
    bsz = c0.shape[0]
    tb = min(seq, SEQ_BLOCK)
    chunk = tb
    nc, z_map, own_map = _seq_specs(bsz, seq, tb, row0)
    st4 = lambda bi, ci: (bi, 0, 0, 0)
    st3 = lambda bi, ci: (bi, 0, 0)
    return pl.pallas_call(
        functools.partial(_mlstm_kernel, chunk),
        grid=(bsz, nc),
        in_specs=[pl.BlockSpec((tb, D), own_map)] * 3
                 + [pl.BlockSpec((tb // chunk, NH_B, chunk), lambda bi, ci: (bi * nc + ci, 0, 0)),
                    pl.BlockSpec((tb, LANES), own_map),
                    pl.BlockSpec((tb, D), own_map), pl.BlockSpec((tb, D), z_map),
                    pl.BlockSpec((1, NH_B, DH_B, DH_B), st4), pl.BlockSpec((1, NH_B, DH_B), st3),
                    _full(gh.shape), _full(skip.shape)],
        out_specs=[pl.BlockSpec((tb, D), own_map), pl.BlockSpec((1, NH_B, DH_B, DH_B), st4),
                   pl.BlockSpec((1, NH_B, DH_B), st3)],
        out_shape=[jax.ShapeDtypeStruct((bsz * seq, D), BF16), jax.ShapeDtypeStruct(c0.shape, F32),
                   jax.ShapeDtypeStruct(n0.shape, F32)],
        compiler_params=_params(("parallel", "arbitrary")),
        name="mlstm",
    )(q, k, v, w, col, xc, z, c0, n0, gh, skip)


def _attn_kernel(masked, L, q_ref, kh_ref, ko_ref, vh_ref, vo_ref, bucket_ref, relb_ref, sink_ref,
                 o_ref, bias_s, s_scr, p_scr):
    nk = WINDOW + L
    ci = pl.program_id(1)

    @pl.when((pl.program_id(0) == 0) & (ci == 0))
    def _():
        bucket = bucket_ref[...]
        for h in range(N_HEADS):
            acc = jnp.zeros((L, nk), F32)
            for b in range(NUM_BUCKETS):
                acc = jnp.where(bucket == b, relb_ref[b, h], acc)
            bias_s[h * L:(h + 1) * L, :] = acc

    kall = jnp.concatenate([kh_ref[...], ko_ref[...]], axis=0)
    vall = jnp.concatenate([vh_ref[...], vo_ref[...]], axis=0)
    gw = GROUP * HEAD_DIM
    slot = lax.broadcasted_iota(jnp.int32, (L, gw), 1) // HEAD_DIM
    kk = lax.broadcasted_iota(jnp.int32, (1, nk), 1)
    gl = GROUP * L
    n_sub = q_ref.shape[0] // L
    hl = N_HEADS * L
    for u in range(n_sub):
        kcat = kall[u * L:u * L + nk, :]
        for g in range(N_KV):
            gsl = slice(g * gw, (g + 1) * gw)
            qg = q_ref[u * L:(u + 1) * L, gsl]
            zero = jnp.zeros_like(qg)
            qs = jnp.concatenate([jnp.where(slot == j, qg, zero) for j in range(GROUP)], axis=0)
            s_scr[u * hl + g * gl:u * hl + (g + 1) * gl, :] = _dot_nt(qs, kcat[:, gsl])
    for u in range(n_sub):
        if masked:
            valid = kk >= WINDOW - u * L - ci * (n_sub * L)
        for h in range(N_HEADS):
            rows = slice(u * hl + h * L, u * hl + (h + 1) * L)
            s = s_scr[rows, :] + bias_s[h * L:(h + 1) * L, :]
            if masked:
                s = jnp.where(valid, s, NEG_INF)
            sink = sink_ref[0, h]
            mx = jnp.maximum(jnp.max(s, axis=1, keepdims=True), sink)
            e = jnp.exp(s - mx)
            den = jnp.sum(e, axis=1, keepdims=True) + jnp.exp(sink - mx)
            p_scr[rows, :] = (e / den).astype(BF16)
    for u in range(n_sub):
        vcat = vall[u * L:u * L + nk, :]
        for g in range(N_KV):
            gsl = slice(g * gw, (g + 1) * gw)
            ow = _dot(p_scr[u * hl + g * gl:u * hl + (g + 1) * gl, :], vcat[:, gsl])
            og = jnp.where(slot == 0, ow[0:L, :], 0.0)
            for j in range(1, GROUP):
                og = jnp.where(slot == j, ow[j * L:(j + 1) * L, :], og)
            o_ref[u * L:(u + 1) * L, gsl] = og.astype(BF16)


def _attention(q, k_own, v_own, k_hist, v_hist, hist_map, row0, seq, bsz, masked, bucket, rel_bias, sinks):
    L = min(seq, CHUNK)
    lb = min(seq, ATTN_BLOCK)
    nc, in_map, out_map = _seq_specs(bsz, seq, lb, row0)
    n_sub = lb // L
    nk = WINDOW + L
    smem = functools.partial(pl.BlockSpec, memory_space=pltpu.SMEM)
    return pl.pallas_call(
        functools.partial(_attn_kernel, masked, L),
        grid=(bsz, nc),
        in_specs=[pl.BlockSpec((lb, D), in_map),
                  pl.BlockSpec((WINDOW, D), hist_map), pl.BlockSpec((lb, D), in_map),
                  pl.BlockSpec((WINDOW, D), hist_map), pl.BlockSpec((lb, D), in_map),
                  _full(bucket.shape), smem(), smem()],
        out_specs=pl.BlockSpec((lb, D), out_map),
        out_shape=jax.ShapeDtypeStruct((bsz * seq, D), BF16),
        scratch_shapes=[pltpu.VMEM((N_HEADS * L, nk), F32), pltpu.VMEM((n_sub * N_HEADS * L, nk), F32),
                        pltpu.VMEM((n_sub * N_HEADS * L, nk), BF16)],
        compiler_params=_params(("arbitrary", "arbitrary")),
        name="swa",
    )(q, k_hist, k_own, v_hist, v_own, bucket, rel_bias, sinks)


def _router_kernel(x_ref, g_ref, wr_ref, info_ref, cnt_ref):
    tm = x_ref.shape[0] // SUB

    @pl.when(pl.program_id(0) == 0)
    def _():
        cnt_ref[...] = jnp.zeros_like(cnt_ref)

    xn = _rms(_tile_rows_load(x_ref), g_ref[...])
    hi = xn.astype(BF16)
    lo = (xn - hi.astype(F32)).astype(BF16)
    w = wr_ref[...]
    whi = w.astype(BF16)
    wlo = (w - whi.astype(F32)).astype(BF16)
    logits = _dot(hi, whi) + _dot(hi, wlo) + _dot(lo, whi)
    lane = lax.broadcasted_iota(jnp.int32, (tm, LANES), 1)
    logits = jnp.where(lane < N_EXPERTS, logits, NEG_INF)
    m1 = jnp.max(logits, axis=1, keepdims=True)
    i1 = jnp.min(jnp.where(logits == m1, lane, LANES), axis=1, keepdims=True)
    rest = jnp.where(lane == i1, NEG_INF, logits)
    m2 = jnp.max(rest, axis=1, keepdims=True)
    i2 = jnp.min(jnp.where(rest == m2, lane, LANES), axis=1, keepdims=True)
    e2 = jnp.exp(m2 - m1)
    g1 = 1.0 / (1.0 + e2)
    g2 = e2 / (1.0 + e2)
    sel = (lane == i1) | (lane == i2)
    tri = (lax.broadcasted_iota(jnp.int32, (tm, tm), 0) > lax.broadcasted_iota(jnp.int32, (tm, tm), 1)).astype(BF16)
    rank = cnt_ref[...] + _dot(tri, sel.astype(BF16))
    r1 = jnp.sum(jnp.where(lane == i1, rank, 0.0), axis=1, keepdims=True)
    r2 = jnp.sum(jnp.where(lane == i2, rank, 0.0), axis=1, keepdims=True)
    cnt_ref[...] = cnt_ref[...] + jnp.sum(sel.astype(F32), axis=0, keepdims=True)
    info = jnp.where(lane == 0, i1.astype(F32), 0.0)
    info = jnp.where(lane == 1, i2.astype(F32), info)
    info = jnp.where(lane == 2, g1, info)
    info = jnp.where(lane == 3, g2, info)
    info = jnp.where(lane == 4, r1, info)
    info = jnp.where(lane == 5, r2, info)
    info_ref[...] = info


def _router(x, g, wr, tm):
    n = x.shape[0] // SUB
    return pl.pallas_call(
        _router_kernel,
        grid=(n // tm,),
        in_specs=[_tile_spec(tm), _full((1, D)), _full(wr.shape)],
        out_specs=[_rows(tm, LANES), _full((1, LANES))],
        out_shape=[jax.ShapeDtypeStruct((n, LANES), F32), jax.ShapeDtypeStruct((1, LANES), F32)],
        compiler_params=_params(("arbitrary",)),
        name="moe_router",
    )(x, g, wr)


def _token_tile(ref, t):
    return ref.at[pl.ds(pl.multiple_of(t * SUB, SUB), SUB)]


def _moe_scatter_kernel(te, zpos_ref, dest_ref, x_ref, xs_hbm, zero_s, sem):
    tm = x_ref.shape[0] // SUB
    span = te * SUB

    def rows_of(tile_start):
        return xs_hbm.at[pl.ds(pl.multiple_of(tile_start * SUB, span), span)]

    @pl.when(pl.program_id(0) == 0)
    def _():
        zero_s[...] = jnp.zeros_like(zero_s)
        for e in range(N_EXPERTS):
            pltpu.make_async_copy(zero_s, rows_of(zpos_ref[e]), sem).start()
        for e in range(N_EXPERTS):
            pltpu.make_async_copy(zero_s, rows_of(zpos_ref[e]), sem).wait()

        def clear_tail(t, carry):
            tail = pltpu.make_async_copy(zero_s, rows_of(t * te), sem)
            tail.start()
            tail.wait()
            return carry

        lax.fori_loop(zpos_ref[N_EXPERTS] // te, xs_hbm.shape[0] // span, clear_tail, 0)

    def copy(r, s):
        return pltpu.make_async_copy(_token_tile(x_ref, r), _token_tile(xs_hbm, dest_ref[s * tm + r]), sem)

    def start(r, carry):
        for s in range(2):
            copy(r, s).start(priority=s)
        return carry

    def wait(r, carry):
        for s in range(2):
            copy(r, s).wait()
        return carry

    lax.fori_loop(0, tm, start, 0)
    lax.fori_loop(0, tm, wait, 0)


def _moe_scatter(x, dest, zpos, n_rows, tm, te):
    n = x.shape[0] // SUB
    return pl.pallas_call(
        functools.partial(_moe_scatter_kernel, te),
        grid_spec=pltpu.PrefetchScalarGridSpec(
            num_scalar_prefetch=1,
            grid=(n // tm,),
            in_specs=[pl.BlockSpec((2 * tm,), lambda i, zp: (i,), memory_space=pltpu.SMEM),
                      pl.BlockSpec((tm * SUB, LANES), lambda i, zp: (i, 0))],
            out_specs=pl.BlockSpec(memory_space=pl.ANY),
            scratch_shapes=[pltpu.VMEM((te * SUB, LANES), F32), pltpu.SemaphoreType.DMA(())],
        ),
        out_shape=jax.ShapeDtypeStruct((n_rows * SUB, LANES), F32),
        compiler_params=_params(("arbitrary",)),
        name="moe_scatter",
    )(zpos, dest, x)


def _moe_ffn_kernel(n_ff, te, pe_ref, na_ref, xs_ref, g_ref, w1_ref, w3_ref, w2_ref, ys_ref, xn_s, acc_s):
    i, j = pl.program_id(0), pl.program_id(1)
    span = te * SUB
    for s in range(2):
        active = s < na_ref[i]
        rows = slice(s * te, (s + 1) * te)

        @pl.when(active & (j == 0))
        def _():
            xn_s[rows, :] = _rms(_tile_rows_load(xs_ref.at[pl.ds(s * span, span)]), g_ref[...]).astype(BF16)
            acc_s[rows, :] = jnp.zeros((te, D), F32)

        @pl.when(active)
        def _():
            xn = xn_s[rows, :]
            a = (_silu(_dot(xn, w1_ref[0])) * _dot(xn, w3_ref[0])).astype(BF16)
            acc_s[rows, :] += _dot(a, w2_ref[0])

        @pl.when(j == n_ff - 1)
        def _():
            _tile_rows_store(ys_ref.at[pl.ds(s * span, span)], jnp.where(active, acc_s[rows, :], 0.0))


def _moe_ffn(xs, g, w1, w3, w2, pair_expert, pair_active, te):
    n_pairs = pair_expert.shape[0]
    d_ff = w1.shape[2]
    n_ff = FF_SPLIT_MOE
    tf = d_ff // n_ff

    def ff(i, j, na_r):
        return jnp.where(na_r[i] > 0, j, n_ff - 1)

    rows = pl.BlockSpec((2 * te * SUB, LANES), lambda i, j, pe_r, na_r: (i, 0))
    return pl.pallas_call(
        functools.partial(_moe_ffn_kernel, n_ff, te),
        grid_spec=pltpu.PrefetchScalarGridSpec(
            num_scalar_prefetch=2,
            grid=(n_pairs, n_ff),
            in_specs=[rows,
                      pl.BlockSpec((1, D), lambda i, j, pe_r, na_r: (0, 0)),
                      pl.BlockSpec((1, D, tf), lambda i, j, pe_r, na_r: (pe_r[i], 0, ff(i, j, na_r))),
                      pl.BlockSpec((1, D, tf), lambda i, j, pe_r, na_r: (pe_r[i], 0, ff(i, j, na_r))),
                      pl.BlockSpec((1, tf, D), lambda i, j, pe_r, na_r: (pe_r[i], ff(i, j, na_r), 0))],
            out_specs=rows,
            scratch_shapes=[pltpu.VMEM((2 * te, D), BF16), pltpu.VMEM((2 * te, D), F32)],
        ),
        out_shape=jax.ShapeDtypeStruct((n_pairs * 2 * te * SUB, LANES), F32),
        compiler_params=_params(("arbitrary", "arbitrary")),
        name="moe_ffn",
    )(pair_expert, pair_active, xs, g, w1, w3, w2)


def _moe_combine_kernel(n_p, dest_ref, info_ref, x_ref, ys_hbm, op_ref, os_ref, buf, sem):
    tm = x_ref.shape[0] // SUB
    i = pl.program_id(0)

    def copy(r, s):
        return pltpu.make_async_copy(_token_tile(ys_hbm, dest_ref[s * tm + r]), _token_tile(buf.at[s], r), sem)

    def start(r, carry):
        for s in range(2):
            copy(r, s).start(priority=s)
        return carry

    def wait(r, carry):
        for s in range(2):
            copy(r, s).wait()
        return carry

    lax.fori_loop(0, tm, start, 0)
    lax.fori_loop(0, tm, wait, 0)
    out = (_tile_rows_load(x_ref) + info_ref[:, 2:3] * _tile_rows_load(buf.at[0])
           + info_ref[:, 3:4] * _tile_rows_load(buf.at[1]))

    @pl.when(i < n_p)
    def _():
        op_ref[...] = out

    @pl.when(i >= n_p)
    def _():
        os_ref[...] = out


def _moe_combine(x, info, dest, ys, tm, n_p, n_s):
    out_p, out_s = _pair(tm, D, n_p)
    return pl.pallas_call(
        functools.partial(_moe_combine_kernel, n_p),
        grid=(n_p + n_s,),
        in_specs=[pl.BlockSpec((2 * tm,), lambda i: (i,), memory_space=pltpu.SMEM),
                  _rows(tm, LANES), _tile_spec(tm), pl.BlockSpec(memory_space=pl.ANY)],
        out_specs=[out_p, out_s],
        out_shape=[jax.ShapeDtypeStruct((n_p * tm, D), F32), jax.ShapeDtypeStruct((n_s * tm, D), F32)],
        scratch_shapes=[pltpu.VMEM((2, tm * SUB, LANES), F32), pltpu.SemaphoreType.DMA(())],
        compiler_params=_params(("arbitrary",)),
        name="moe_combine",
    )(dest, info, x, ys)


def _moe(x, g, w_router, w1, w3, w2, tm, n_p, n_s):
    n = x.shape[0] // SUB
    te = min(EXPERT_TILE, tm)
    wr = jnp.zeros((D, LANES), F32).at[:, :N_EXPERTS].set(w_router)
    info, cnt = _router(x, g, wr, tm)
    e1, e2 = info[:, 0].astype(jnp.int32), info[:, 1].astype(jnp.int32)
    r1, r2 = info[:, 4].astype(jnp.int32), info[:, 5].astype(jnp.int32)
    counts = cnt[0, :N_EXPERTS].astype(jnp.int32)
    region = 2 * te
    padded = (counts + region - 1) // region * region
    ends = jnp.cumsum(padded)
    starts = ends - padded
    dest = jnp.stack([(starts[e1] + r1).reshape(n // tm, tm), (starts[e2] + r2).reshape(n // tm, tm)], axis=1)
    dest = dest.reshape(-1)
    n_pairs = (2 * n + N_EXPERTS * (region - 1)) // region
    pair_start = jnp.arange(n_pairs, dtype=jnp.int32) * region
    pair_expert = jnp.minimum(jnp.sum(pair_start[:, None] >= ends[None, :], axis=1), N_EXPERTS - 1).astype(jnp.int32)
    used_end = starts + (counts + te - 1) // te * te
    pair_active = jnp.clip((used_end[pair_expert] - pair_start) // te, 0, 2).astype(jnp.int32)
    last_region = jnp.maximum(ends - region, 0).astype(jnp.int32)
    xs = _moe_scatter(x, dest, jnp.concatenate([last_region, ends[-1:].astype(jnp.int32)]), n_pairs * region, tm,
                      region)
    ys = _moe_ffn(xs, g, w1, w3, w2, pair_expert, pair_active, te)
    return _moe_combine(x, info, dest, ys, tm, n_p, n_s)


def _t5_bucket(rel):
    n = -rel
    half = NUM_BUCKETS // 2
    ret = jnp.where(n < 0, half, 0)
    n = jnp.abs(n)
    max_exact = half // 2
    nf = jnp.maximum(n, 1).astype(F32)
    large = max_exact + (jnp.log(nf / max_exact) / math.log(MAX_DISTANCE / max_exact)
                         * (half - max_exact)).astype(jnp.int32)
    large = jnp.minimum(large, half - 1)
    return ret + jnp.where(n < max_exact, n, large)


def _bucket_table(length):
    kpos = jnp.arange(WINDOW + length) - WINDOW
    return _t5_bucket(kpos[None, :] - jnp.arange(length)[:, None]).astype(jnp.int32)


def kernel(x_prompt, x_sample, state_conv_a, state_rglru_h, state_conv_b, state_mlstm_c, state_mlstm_n, state_mlstm_m, cache_swa_k, cache_swa_v, norm_mix, norm_ffn, w_in_ab, w_conv_a, b_conv_a, w_rg_a, b_rg_a, w_rg_x, b_rg_x, rg_lambda, w_conv_b, b_conv_b, w_q_b, w_k_b, w_v_b, w_if_b, b_if_b, g_hnorm_b, skip_b, w_out_ab, w1_dense, w3_dense, w2_dense, w_in_att, g_qnorm, g_knorm, sinks, w_out_att, rel_bias, w_router, w1_moe, w3_moe, w2_moe):
    bp, tp, _ = x_prompt.shape
    bs, ts, _ = x_sample.shape
    assert norm_mix.shape[0] == 2 and w_in_ab.shape[0] == 1 and w_in_att.shape[0] == 1
    assert tp % CHUNK == 0 and ts <= CHUNK and cache_swa_k.shape[2] == WINDOW
    rows_p, rows_s = bp * tp, bs * ts
    tm = math.gcd(math.gcd(rows_p, rows_s), TOKEN_TILE)
    n_p, n_s = rows_p // tm, rows_s // tm
    xp = x_prompt.reshape(rows_p, D)
    xs = x_sample.reshape(rows_s, D)
    bf = lambda w: w.astype(BF16)
    vec = lambda v: v.reshape(1, -1)

    xa, ga, xm, z = _in_proj_ab(xp, xs, vec(norm_mix[0]), bf(w_in_ab[0]), tm)
    rg_w = (w_conv_a[0], vec(b_conv_a[0]), bf(w_rg_a[0]), vec(b_rg_a[0]), bf(w_rg_x[0]), vec(b_rg_x[0]),
            vec(rg_lambda[0]))
    ya_p, hl_p = _rglru(xa, ga, 0, tp, jnp.zeros((bp, CONV_W - 1, D), F32), jnp.zeros((bp, 1, D), F32), *rg_w)
    ya_s, hl_s = _rglru(xa, ga, rows_p, ts, state_conv_a[0], state_rglru_h[0].reshape(bs, 1, D), *rg_w)

    def gate_cols(w):
        pad = ((0, 0), (0, LANES - NH_B))
        return jnp.concatenate([jnp.pad(w[:, :NH_B], pad), jnp.pad(w[:, NH_B:], pad)], axis=1)

    pre_w = (w_conv_b[0], vec(b_conv_b[0]), bf(w_q_b[0]), bf(w_k_b[0]), bf(w_v_b[0]), bf(w_if_b[0].T),
             b_if_b[0].reshape(2 * NH_B, 1), bf(gate_cols(w_if_b[0])), gate_cols(b_if_b[0][None]))
    rec_w = (g_hnorm_b[0].reshape(1, D), vec(skip_b[0]))
    lanes = lambda m: jnp.pad(m, ((0, 0), (0, LANES - NH_B)))[:, None, :]
    pre_p = _mlstm_pre(xm, 0, tp, jnp.zeros((bp, CONV_W - 1, D), F32), jnp.zeros((bp, 1, LANES), F32), *pre_w)
    pre_s = _mlstm_pre(xm, rows_p, ts, state_conv_b[0], lanes(state_mlstm_m[0]), *pre_w)
    m_p, m_s = pre_p[6], pre_s[6]
    yb_p, c_p, nn_p = _mlstm(*pre_p[1:6], pre_p[0], z, 0, tp, jnp.zeros((bp, NH_B, DH_B, DH_B), F32),
                             jnp.zeros((bp, NH_B, DH_B), F32), *rec_w)
    yb_s, c_s, nn_s = _mlstm(*pre_s[1:6], pre_s[0], z, rows_p, ts, state_mlstm_c[0], state_mlstm_n[0], *rec_w)

    w_out = bf(w_out_ab[0])
    y = _out_proj((xp, xs), [(ya_p, ya_s), (yb_p, yb_s)], [w_out[:D], w_out[D:]], tm, n_p)
    y = _ffn_dense(y, vec(norm_ffn[0]), bf(w1_dense[0]), bf(w3_dense[0]), bf(w2_dense[0]), tm)

    tile2 = lambda gain: jnp.tile(gain, D_KV // HEAD_DIM).reshape(1, D_KV)
    q, kw, vw, k, v = _in_proj_att(y, vec(norm_mix[1]), bf(w_in_att[0]), tile2(g_qnorm[0]), tile2(g_knorm[0]), tm)
    sinks2 = sinks[0].reshape(1, N_HEADS)
    wpb = min(tp, ATTN_BLOCK) // WINDOW
    assert wpb >= 1 and tp % (wpb * WINDOW) == 0
    hist_p = lambda bi, ci: (bi * (tp // WINDOW) + jnp.maximum(ci * wpb - 1, 0), 0)
    o_p = _attention(q, kw, vw, kw, vw, hist_p, 0, tp, bp, True, _bucket_table(CHUNK), rel_bias, sinks2)

    def widen(cache):
        wide = jnp.broadcast_to(cache[:, :, :, None, :], (bs, WINDOW, N_KV, GROUP, HEAD_DIM))
        return wide.reshape(bs * WINDOW, D).astype(BF16)

    ck, cv = widen(cache_swa_k[0]), widen(cache_swa_v[0])
    hist_s = lambda bi, ci: (bi, 0)
    o_s = _attention(q, kw, vw, ck, cv, hist_s, rows_p, ts, bs, False, _bucket_table(ts), rel_bias, sinks2)
    y = _out_proj(y, [(o_p, o_s)], [bf(w_out_att[0])], tm, n_p, tiled_out=True)
    yp, ys = _moe(y, vec(norm_ffn[1]), w_router[0], bf(w1_moe[0]), bf(w3_moe[0]), bf(w2_moe[0]), tm, n_p, n_s)

    def tail(a, rows, b, t, keep):
        if rows == 0:
            return jnp.stack([a[(i + 1) * t - keep:(i + 1) * t] for i in range(b)])
        return a[rows:rows + b * t].reshape(b, t, -1)[:, t - keep:]

    kv4 = lambda a, b: a.reshape(b, -1, N_KV, HEAD_DIM)
    one = lambda a: a[None]
    k_s = jnp.concatenate([cache_swa_k[0][:, ts:], kv4(k[rows_p:], bs)], axis=1)
    v_s = jnp.concatenate([cache_swa_v[0][:, ts:], kv4(v[rows_p:], bs)], axis=1)
    return (yp.reshape(bp, tp, D), ys.reshape(bs, ts, D),
            one(tail(xa, 0, bp, tp, CONV_W - 1)), one(hl_p.reshape(bp, D)), one(tail(xm, 0, bp, tp, CONV_W - 1)),
            one(c_p), one(nn_p), one(m_p[:, 0, :NH_B]),
            one(kv4(tail(k, 0, bp, tp, WINDOW), bp)), one(kv4(tail(v, 0, bp, tp, WINDOW), bp)),
            one(tail(xa, rows_p, bs, ts, CONV_W - 1)), one(hl_s.reshape(bs, D)),
            one(tail(xm, rows_p, bs, ts, CONV_W - 1)), one(c_s), one(nn_s), one(m_s[:, 0, :NH_B]),
            one(k_s), one(v_s))
```

```python
import functools
import math

import jax
import jax.numpy as jnp
from jax import lax
from jax.experimental import pallas as pl
from jax.experimental.pallas import tpu as pltpu

F32 = jnp.float32
BF16 = jnp.bfloat16

D = 1024
CHUNK = 64
CONV_W = 4
NB_A = 8
BS_A = D // NB_A
RG_C = 8.0
NH_B = 4
DH_B = D // NH_B
N_HEADS = 16
HEAD_DIM = D // N_HEADS
N_KV = 4
GROUP = N_HEADS // N_KV
D_KV = N_KV * HEAD_DIM
WINDOW = 128
NUM_BUCKETS = 32
MAX_DISTANCE = 128
N_EXPERTS = 8
EPS = 1e-6
LANES = 128
SUB = D // LANES
NEG_INF = float("-inf")
MIN_NORMAL = 1.1754944e-38

TOKEN_TILE = 512
EXPERT_TILE = 512
FF_SPLIT_DENSE = 3
FF_SPLIT_MOE = 2
SEQ_BLOCK = 256
RGLRU_BLOCK = 512
SCAN_ROWS = 8
ATTN_BLOCK = 256
VMEM_LIMIT = 56 * 1024 * 1024


def _params(sem):
    return pltpu.CompilerParams(dimension_semantics=sem, vmem_limit_bytes=VMEM_LIMIT)


def _full(shape):
    return pl.BlockSpec(shape, lambda *_: (0,) * len(shape))


def _resident(shape):
    return pl.BlockSpec(shape, lambda *_: (0,) * len(shape), pipeline_mode=pl.Buffered(1))


def _rows(tm, c):
    return pl.BlockSpec((tm, c), lambda i: (i, 0))


def _pair(tm, c, n_p):
    return [pl.BlockSpec((tm, c), lambda i: (jnp.minimum(i, n_p - 1), 0)),
            pl.BlockSpec((tm, c), lambda i: (jnp.maximum(i - n_p, 0), 0))]


def _rms(x, g):
    ms = jnp.mean(x * x, axis=-1, keepdims=True)
    return x * lax.rsqrt(ms + EPS) * g


def _silu(x):
    return x * jax.nn.sigmoid(x)


def _sigmoid_tanh(x):
    return 0.5 * jnp.tanh(0.5 * x) + 0.5


def _softplus(x):
    return jnp.maximum(x, 0.0) + jnp.log1p(jnp.exp(-jnp.abs(x)))


def _gelu_tanh(x):
    c = math.sqrt(2.0 / math.pi)
    return x * (0.5 * (1.0 + jnp.tanh(c * (x + 0.044715 * (x * x * x)))))


def _dot(a, b):
    return jnp.dot(a, b, preferred_element_type=F32)


def _dot_nt(a, b):
    return lax.dot_general(a, b, (((1,), (1,)), ((), ())), preferred_element_type=F32)


def _dot_tn(a, b):
    return lax.dot_general(a, b, (((0,), (0,)), ((), ())), preferred_element_type=F32)


def _in_proj_ab_kernel(n_p, xp_ref, xs_ref, g_ref, w_ref, *o_refs):
    i = pl.program_id(0)
    x = jnp.where(i < n_p, xp_ref[...], xs_ref[...])
    xn = _rms(x, g_ref[...]).astype(BF16)
    for c, o_ref in enumerate(o_refs):
        o_ref[...] = _dot(xn, w_ref[:, c * D:(c + 1) * D])


def _in_proj_ab(xp, xs, g, w, tm):
    n_p, n_s = xp.shape[0] // tm, xs.shape[0] // tm
    n = xp.shape[0] + xs.shape[0]
    n_out = w.shape[1] // D
    return pl.pallas_call(
        functools.partial(_in_proj_ab_kernel, n_p),
        grid=(n_p + n_s,),
        in_specs=_pair(tm, D, n_p) + [_full((1, D)), _resident(w.shape)],
        out_specs=[_rows(tm, D)] * n_out,
        out_shape=[jax.ShapeDtypeStruct((n, D), F32)] * n_out,
        compiler_params=_params(("parallel",)),
        name="in_proj_ab",
    )(xp, xs, g, w)


def _tile_rows_load(ref):
    tm = ref.shape[0] // SUB
    return jnp.concatenate([ref[pl.ds(s, tm, stride=SUB), :] for s in range(SUB)], axis=1)


def _tile_rows_store(ref, val):
    tm = val.shape[0]
    for s in range(SUB):
        ref[pl.ds(s, tm, stride=SUB), :] = val[:, s * LANES:(s + 1) * LANES]


def _tile_spec(tm):
    return pl.BlockSpec((tm * SUB, LANES), lambda i: (i, 0))


def _out_proj_kernel(n_p, n_res, n_a, tiled_out, *refs):
    i = pl.program_id(0)
    res_refs = refs[:n_res]
    a_refs = refs[n_res:n_res + 2 * n_a]
    w_refs = refs[n_res + 2 * n_a:n_res + 3 * n_a]
    o_ref = refs[-1]
    if n_res == 2:
        acc = jnp.where(i < n_p, res_refs[0][...], res_refs[1][...])
    else:
        acc = res_refs[0][...]
    for k in range(n_a):
        a = jnp.where(i < n_p, a_refs[2 * k][...], a_refs[2 * k + 1][...])
        acc = acc + _dot(a, w_refs[k][...])
    if tiled_out:
        _tile_rows_store(o_ref, acc)
    else:
        o_ref[...] = acc


def _out_proj(res, a_pairs, ws, tm, n_p, tiled_out=False):
    res = res if isinstance(res, tuple) else (res,)
    n = sum(a.shape[0] for a in a_pairs[0])
    res_specs = _pair(tm, D, n_p) if len(res) == 2 else [_rows(tm, D)]
    a_specs = []
    for a in a_pairs:
        a_specs += _pair(tm, a[0].shape[1], n_p)
    return pl.pallas_call(
        functools.partial(_out_proj_kernel, n_p, len(res), len(a_pairs), tiled_out),
        grid=(n // tm,),
        in_specs=res_specs + a_specs + [_resident(w.shape) for w in ws],
        out_specs=_tile_spec(tm) if tiled_out else _rows(tm, D),
        out_shape=jax.ShapeDtypeStruct((n * SUB, LANES) if tiled_out else (n, D), F32),
        compiler_params=_params(("parallel",)),
        name="out_proj",
    )(*res, *[x for a in a_pairs for x in a], *ws)


def _ffn_dense_kernel(x_ref, g_ref, w1_ref, w3_ref, w2_ref, o_ref):
    x = x_ref[...]
    xn = _rms(x, g_ref[...]).astype(BF16)
    step = w1_ref.shape[1] // FF_SPLIT_DENSE
    acc = x
    for c in range(FF_SPLIT_DENSE):
        sl = slice(c * step, (c + 1) * step)
        a = (_silu(_dot(xn, w1_ref[:, sl])) * _dot(xn, w3_ref[:, sl])).astype(BF16)
        acc = acc + _dot(a, w2_ref[sl, :])
    o_ref[...] = acc


def _ffn_dense(x, g, w1, w3, w2, tm):
    n = x.shape[0]
    return pl.pallas_call(
        _ffn_dense_kernel,
        grid=(n // tm,),
        in_specs=[_rows(tm, D), _full((1, D)), _resident(w1.shape), _resident(w3.shape), _resident(w2.shape)],
        out_specs=_rows(tm, D),
        out_shape=jax.ShapeDtypeStruct((n, D), F32),
        compiler_params=_params(("parallel",)),
        name="ffn_dense",
    )(x, g, w1, w3, w2)


def _in_proj_att_kernel(x_ref, g_ref, w_ref, gq_ref, gk_ref, q_ref, kw_ref, vw_ref, k_ref, v_ref):
    xn = _rms(x_ref[...], g_ref[...]).astype(BF16)
    cw = D_KV
    r = lax.broadcasted_iota(jnp.int32, (cw, cw), 0) // HEAD_DIM
    c = lax.broadcasted_iota(jnp.int32, (cw, cw), 1) // HEAD_DIM
    group_ones = (r == c).astype(BF16)
    er = lax.broadcasted_iota(jnp.int32, (D_KV, D), 0)
    ec = lax.broadcasted_iota(jnp.int32, (D_KV, D), 1)
    widen = ((er // HEAD_DIM == ec // (GROUP * HEAD_DIM)) & (er % HEAD_DIM == ec % HEAD_DIM)).astype(BF16)

    def head_norm(y, gain):
        ms = _dot((y * y).astype(BF16), group_ones) * (1.0 / HEAD_DIM)
        return y * lax.rsqrt(ms + EPS) * gain

    for b in range(D // cw):
        y = _dot(xn, w_ref[:, b * cw:(b + 1) * cw])
        q_ref[:, b * cw:(b + 1) * cw] = (head_norm(y, gq_ref[...]) * (HEAD_DIM ** -0.5)).astype(BF16)
    k = head_norm(_dot(xn, w_ref[:, D:D + D_KV]), gk_ref[...])
    v = _dot(xn, w_ref[:, D + D_KV:D + 2 * D_KV])
    k_ref[...] = k
    v_ref[...] = v
    kw_ref[...] = _dot(k.astype(BF16), widen).astype(BF16)
    vw_ref[...] = _dot(v.astype(BF16), widen).astype(BF16)


def _in_proj_att(x, g, w, gq, gk, tm):
    n = x.shape[0]
    return pl.pallas_call(
        _in_proj_att_kernel,
        grid=(n // tm,),
        in_specs=[_rows(tm, D), _full((1, D)), _resident(w.shape), _full((1, D_KV)), _full((1, D_KV))],
        out_specs=[_rows(tm, D)] * 3 + [_rows(tm, D_KV)] * 2,
        out_shape=[jax.ShapeDtypeStruct((n, D), BF16)] * 3 + [jax.ShapeDtypeStruct((n, D_KV), F32)] * 2,
        compiler_params=_params(("parallel",)),
        name="in_proj_att",
    )(x, g, w, gq, gk)


def _causal_conv(x, xbuf, conv0_ref, wc_ref, bc_ref, first):
    tb = x.shape[0]

    @pl.when(first)
    def _():
        xbuf[5:8, :] = conv0_ref[0]

    xbuf[8:8 + tb, :] = x
    y = bc_ref[...] + xbuf[5:5 + tb, :] * wc_ref[0:1, :]
    for j in range(1, CONV_W):
        y = y + xbuf[5 + j:5 + j + tb, :] * wc_ref[j:j + 1, :]
    xbuf[5:8, :] = xbuf[5 + tb:8 + tb, :]
    return y


def _seq_specs(bsz, seq, tb, row0):
    nc = seq // tb
    off = row0 // tb
    return nc, (lambda bi, ci: (off + bi * nc + ci, 0)), (lambda bi, ci: (bi * nc + ci, 0))


def _scan_by_doubling(a, u, pos, axis, length):
    d = 1
    while d < length:
        keep = pos >= d
        u = jnp.where(keep, a * pltpu.roll(u, d, axis) + u, u)
        a = jnp.where(keep, a * pltpu.roll(a, d, axis), a)
        d *= 2
    return a, u


def _linear_scan(a, u, h0, row):
    tb = a.shape[0]
    sub = min(tb, SCAN_ROWS)
    out = []
    for k in range(tb // sub):
        rows = slice(k * sub, (k + 1) * sub)
        a_k, u_k = _scan_by_doubling(a[rows, :], u[rows, :], row[0:sub, :], 0, sub)
        h_k = a_k * h0 + u_k
        h0 = h_k[sub - 1:sub, :]
        out.append(h_k)
    return jnp.concatenate(out, axis=0)


def _rglru_kernel(xa_ref, ga_ref, conv0_ref, h0_ref, wc_ref, bc_ref, wa_ref, ba_ref, wx_ref, bx_ref, lam_ref,
                  ya_ref, hl_ref, xbuf):
    first = pl.program_id(1) == 0
    tb = xa_ref.shape[0]

    @pl.when(first)
    def _():
        hl_ref[0] = h0_ref[0]

    y = _causal_conv(xa_ref[...], xbuf, conv0_ref, wc_ref, bc_ref, first)
    row = lax.broadcasted_iota(jnp.int32, (tb, BS_A), 0)
    for n in range(NB_A):
        sl = slice(n * BS_A, (n + 1) * BS_A)
        yn = y[:, sl]
        yb = yn.astype(BF16)
        r = _sigmoid_tanh(_dot(yb, wa_ref[n]) + ba_ref[:, sl])
        ig = _sigmoid_tanh(_dot(yb, wx_ref[n]) + bx_ref[:, sl])
        log_a = (-RG_C) * r * _softplus(-lam_ref[:, sl])
        a = jnp.exp(log_a)
        th = jnp.tanh(log_a)
        gain2 = -2.0 * th / (1.0 - th)
        u = yn * ig * (gain2 * lax.rsqrt(jnp.maximum(gain2, MIN_NORMAL)))
        h = _linear_scan(a, u, hl_ref[0, :, sl], row)
        hl_ref[0, :, sl] = h[tb - 1:tb, :]
        ya_ref[:, sl] = (h * _gelu_tanh(ga_ref[:, sl])).astype(BF16)


def _rglru(xa, ga, row0, seq, conv0, h0, wc, bc, wa, ba, wx, bx, lam):
    bsz = conv0.shape[0]
    tb = math.gcd(seq, RGLRU_BLOCK)
    nc, in_map, out_map = _seq_specs(bsz, seq, tb, row0)
    state3 = lambda bi, ci: (bi, 0, 0)
    return pl.pallas_call(
        _rglru_kernel,
        grid=(bsz, nc),
        in_specs=[pl.BlockSpec((tb, D), in_map), pl.BlockSpec((tb, D), in_map),
                  pl.BlockSpec((1, CONV_W - 1, D), state3), pl.BlockSpec((1, 1, D), state3),
                  _full(wc.shape), _full(bc.shape), _full(wa.shape), _full(ba.shape), _full(wx.shape),
                  _full(bx.shape), _full(lam.shape)],
        out_specs=[pl.BlockSpec((tb, D), out_map), pl.BlockSpec((1, 1, D), state3)],
        out_shape=[jax.ShapeDtypeStruct((bsz * seq, D), BF16), jax.ShapeDtypeStruct((bsz, 1, D), F32)],
        scratch_shapes=[pltpu.VMEM((tb + 8, D), F32)],
        compiler_params=_params(("parallel", "arbitrary")),
        name="rglru",
    )(xa, ga, conv0, h0, wc, bc, wa, ba, wx, bx, lam)


def _mlstm_pre_kernel(chunk, xm_ref, conv0_ref, m0_ref, wc_ref, bc_ref, wq_ref, wk_ref, wv_ref, wif_ref, bif_ref,
                      wifc_ref, bifc_ref, xc_ref, qkv_ref, w_ref, col_ref, m_ref, xbuf):
    first = pl.program_id(1) == 0
    tb = xm_ref.shape[0]
    L = chunk

    @pl.when(first)
    def _():
        m_ref[...] = m0_ref[...]

    x = xm_ref[...]
    xc = _silu(_causal_conv(x, xbuf, conv0_ref, wc_ref, bc_ref, first))
    xc_ref[...] = xc
    g_row = bif_ref[...]
    g_col = bifc_ref[...]
    for h in range(NH_B):
        sl = slice(h * DH_B, (h + 1) * DH_B)
        xch = xc[:, sl].astype(BF16)
        q = _dot(xch, wq_ref[h]).astype(BF16)
        k = _dot(xch, wk_ref[h]).astype(BF16)
        v = _dot(x[:, sl].astype(BF16), wv_ref[h]).astype(BF16)
        for part, val in enumerate((q, k, v)):
            psl = slice(part * D + h * DH_B, part * D + (h + 1) * DH_B)
            qkv_ref[:, psl] = val
            g_row = g_row + _dot_nt(wif_ref[:, psl], val)
            g_col = g_col + _dot(val, wifc_ref[psl, :])

    def exact_dot(a, b01):
        p1 = a.astype(BF16)
        r1 = a - p1.astype(F32)
        p2 = r1.astype(BF16)
        p3 = (r1 - p2.astype(F32)).astype(BF16)
        return _dot(p1, b01) + _dot(p2, b01) + _dot(p3, b01)

    def exact_dot_left(b01, a):
        p1 = a.astype(BF16)
        r1 = a - p1.astype(F32)
        p2 = r1.astype(BF16)
        p3 = (r1 - p2.astype(F32)).astype(BF16)
        return _dot(b01, p1) + _dot(b01, p2) + _dot(b01, p3)

    tt = lax.broadcasted_iota(jnp.int32, (L, L), 0)
    ss = lax.broadcasted_iota(jnp.int32, (L, L), 1)
    upper = (tt <= ss).astype(BF16)
    lower = (ss <= tt).astype(BF16)
    i_row = g_row[0:NH_B, :]
    b_row = exact_dot(-_softplus(-g_row), upper)[NH_B:2 * NH_B, :]
    w_ref[0] = i_row - b_row
    i_col = g_col[:, 0:LANES]
    b_col = exact_dot_left(lower, -_softplus(-g_col[:, LANES:2 * LANES]))
    w_col = i_col - b_col
    trow = lax.broadcasted_iota(jnp.int32, (L, LANES), 0)
    run_max = w_col
    d = 1
    while d < L:
        run_max = jnp.where(trow >= d, jnp.maximum(run_max, pltpu.roll(run_max, d, 0)), run_max)
        d *= 2
    m_prev = m_ref[0]
    big_m = jnp.maximum(m_prev, run_max)
    m_last = big_m[L - 1:L, :]
    m_ref[0] = b_col[L - 1:L, :] + m_last
    head_lane = lax.broadcasted_iota(jnp.int32, (L, LANES), 1) < NH_B
    table = jnp.where(head_lane, big_m, 0.0)
    for c, col in enumerate((jnp.exp(m_prev - big_m), jnp.exp(-(b_col + big_m)), jnp.exp(w_col - m_last))):
        table = table + pltpu.roll(jnp.where(head_lane, col, 0.0), (c + 1) * NH_B, 1)
    col_ref[...] = table


def _mlstm_pre(xm, row0, seq, conv0, m0, wc, bc, wq, wk, wv, wif_t, bif, wif_c, bif_c):
    bsz = conv0.shape[0]
    tb = min(seq, SEQ_BLOCK)
    chunk = tb
    nc, in_map, out_map = _seq_specs(bsz, seq, tb, row0)
    n = bsz * seq
    st3 = lambda bi, ci: (bi, 0, 0)
    return pl.pallas_call(
        functools.partial(_mlstm_pre_kernel, chunk),
        grid=(bsz, nc),
        in_specs=[pl.BlockSpec((tb, D), in_map), pl.BlockSpec((1, CONV_W - 1, D), st3),
                  pl.BlockSpec((1, 1, LANES), st3),
                  _full(wc.shape), _full(bc.shape), _full(wq.shape), _full(wk.shape), _full(wv.shape),
                  _full(wif_t.shape), _full(bif.shape), _full(wif_c.shape), _full(bif_c.shape)],
        out_specs=[pl.BlockSpec((tb, D), out_map), pl.BlockSpec((tb, 3 * D), out_map)]
                  + [pl.BlockSpec((tb // chunk, NH_B, chunk), lambda bi, ci: (bi * nc + ci, 0, 0)),
                     pl.BlockSpec((tb, LANES), out_map), pl.BlockSpec((1, 1, LANES), st3)],
        out_shape=[jax.ShapeDtypeStruct((n, D), F32), jax.ShapeDtypeStruct((n, 3 * D), BF16)]
                  + [jax.ShapeDtypeStruct((n // chunk, NH_B, chunk), F32), jax.ShapeDtypeStruct((n, LANES), F32),
                     jax.ShapeDtypeStruct(m0.shape, F32)],
        scratch_shapes=[pltpu.VMEM((tb + 8, D), F32)],
        compiler_params=_params(("parallel", "arbitrary")),
        name="mlstm_pre",
    )(xm, conv0, m0, wc, bc, wq, wk, wv, wif_t, bif, wif_c, bif_c)


def _mlstm_kernel(chunk, qkv_ref, w_ref, col_ref, xc_ref, z_ref, c0_ref, n0_ref, gh_ref, skip_ref,
                  yb_ref, c_ref, n_ref):
    tb = qkv_ref.shape[0]
    L = chunk
    scale = DH_B ** -0.5

    @pl.when(pl.program_id(1) == 0)
    def _():
        c_ref[...] = c0_ref[...]
        n_ref[...] = n0_ref[...]

    causal = lax.broadcasted_iota(jnp.int32, (L, L), 1) <= lax.broadcasted_iota(jnp.int32, (L, L), 0)
    heads = range(NH_B)
    hsl = [slice(h * DH_B, (h + 1) * DH_B) for h in heads]
    for j in range(tb // L):
        rows = slice(j * L, (j + 1) * L)
        cols = col_ref[rows, :]
        col = lambda c, h: cols[:, c * NH_B + h:c * NH_B + h + 1]
        part = lambda p, h: qkv_ref[rows, p * D + h * DH_B:p * D + (h + 1) * DH_B]
        q = [part(0, h) for h in heads]
        k = [part(1, h) for h in heads]
        v = [part(2, h) for h in heads]
        s = [_dot_nt(q[h], k[h]) for h in heads]
        qc = [_dot(q[h], c_ref[0, h].astype(BF16)) for h in heads]
        qn = [_dot_nt(q[h], jnp.broadcast_to(n_ref[0, h:h + 1, :], (8, DH_B)).astype(BF16))[:, 0:1] for h in heads]
        p = [(s[h] * scale * jnp.where(causal, jnp.exp(w_ref[j, h:h + 1, :] - col(0, h)), 0.0)).astype(BF16)
             for h in heads]
        pv = [_dot(p[h], v[h]) for h in heads]
        psum = [_dot(p[h], jnp.ones((L, LANES), BF16))[:, 0:1] for h in heads]
        for h in heads:
            sc = col(1, h)
            keep = sc[L - 1:L, :]
            kd = k[h].astype(F32) * (col(3, h) * scale)
            c_ref[0, h] = keep * c_ref[0, h] + _dot_tn(kd.astype(BF16), v[h])
            n_ref[0, h:h + 1, :] = keep * n_ref[0, h:h + 1, :] + jnp.sum(kd, axis=0, keepdims=True)
            num = pv[h] + sc * qc[h]
            den = psum[h] + sc * qn[h]
            hs = num / jnp.maximum(jnp.abs(den), col(2, h))
            mu = jnp.mean(hs, axis=1, keepdims=True)
            dev = hs - mu
            var = jnp.mean(dev * dev, axis=1, keepdims=True)
            hn = dev * lax.rsqrt(var + EPS) * gh_ref[:, hsl[h]]
            out = (hn + skip_ref[:, hsl[h]] * xc_ref[rows, hsl[h]]) * _silu(z_ref[rows, hsl[h]])
            yb_ref[rows, hsl[h]] = out.astype(BF16)


def _mlstm(qkv, w, col, xc, z, row0, seq, c0, n0, gh, skip):
    bsz = c0.shape[0]
    tb = min(seq, SEQ_BLOCK)
    chunk = tb
    nc, z_map, own_map = _seq_specs(bsz, seq, tb, row0)
    st4 = lambda bi, ci: (bi, 0, 0, 0)
    st3 = lambda bi, ci: (bi, 0, 0)
    return pl.pallas_call(
        functools.partial(_mlstm_kernel, chunk),
        grid=(bsz, nc),
        in_specs=[pl.BlockSpec((tb, 3 * D), own_map)]
                 + [pl.BlockSpec((tb // chunk, NH_B, chunk), lambda bi, ci: (bi * nc + ci, 0, 0)),
                    pl.BlockSpec((tb, LANES), own_map),
                    pl.BlockSpec((tb, D), own_map), pl.BlockSpec((tb, D), z_map),
                    pl.BlockSpec((1, NH_B, DH_B, DH_B), st4), pl.BlockSpec((1, NH_B, DH_B), st3),
                    _full(gh.shape), _full(skip.shape)],
        out_specs=[pl.BlockSpec((tb, D), own_map), pl.BlockSpec((1, NH_B, DH_B, DH_B), st4),
                   pl.BlockSpec((1, NH_B, DH_B), st3)],
        out_shape=[jax.ShapeDtypeStruct((bsz * seq, D), BF16), jax.ShapeDtypeStruct(c0.shape, F32),
                   jax.ShapeDtypeStruct(n0.shape, F32)],
        compiler_params=_params(("parallel", "arbitrary")),
        name="mlstm",
    )(qkv, w, col, xc, z, c0, n0, gh, skip)


def _attn_kernel(masked, L, q_ref, kh_ref, ko_ref, vh_ref, vo_ref, bucket_ref, relb_ref, sink_ref,
                 o_ref, bias_s, s_scr, p_scr):
    nk = WINDOW + L
    ci = pl.program_id(1)
    n_sub = q_ref.shape[0] // L
    hl = N_HEADS * L
    kk = lax.broadcasted_iota(jnp.int32, (1, nk), 1)
    n_mask = min(n_sub, WINDOW // L) if masked else 0

    @pl.when((pl.program_id(0) == 0) & (ci == 0))
    def _():
        bucket = bucket_ref[...]
        for h in range(N_HEADS):
            acc = jnp.zeros((L, nk), F32)
            for b in range(NUM_BUCKETS):
                acc = jnp.where(bucket == b, relb_ref[b, h], acc)
            bias_s[h * L:(h + 1) * L, :] = acc
            for u in range(n_mask):
                bias_s[(1 + u) * hl + h * L:(1 + u) * hl + (h + 1) * L, :] = jnp.where(kk >= WINDOW - u * L, acc, NEG_INF)

    kall = jnp.concatenate([kh_ref[...], ko_ref[...]], axis=0)
    vall = jnp.concatenate([vh_ref[...], vo_ref[...]], axis=0)
    gw = GROUP * HEAD_DIM
    slot = lax.broadcasted_iota(jnp.int32, (L, gw), 1) // HEAD_DIM
    gl = GROUP * L
    for u in range(n_sub):
        kcat = kall[u * L:u * L + nk, :]
        for g in range(N_KV):
            gsl = slice(g * gw, (g + 1) * gw)
            qg = q_ref[u * L:(u + 1) * L, gsl]
            zero = jnp.zeros_like(qg)
            qs = jnp.concatenate([jnp.where(slot == j, qg, zero) for j in range(GROUP)], axis=0)
            s_scr[u * hl + g * gl:u * hl + (g + 1) * gl, :] = _dot_nt(qs, kcat[:, gsl])
    for u in range(n_sub):
        for h in range(N_HEADS):
            rows = slice(u * hl + h * L, u * hl + (h + 1) * L)
            if u < n_mask:
                first = jnp.where(ci == 0, (1 + u) * hl, 0)
                bias = bias_s[pl.ds(pl.multiple_of(first + h * L, L), L), :]
            else:
                bias = bias_s[h * L:(h + 1) * L, :]
            s = s_scr[rows, :] + bias
            sink = sink_ref[0, h]
            mx = jnp.maximum(jnp.max(s, axis=1, keepdims=True), sink)
            e = jnp.exp(s - mx)
            den = jnp.sum(e, axis=1, keepdims=True) + jnp.exp(sink - mx)
            p_scr[rows, :] = (e / den).astype(BF16)
    for u in range(n_sub):
        vcat = vall[u * L:u * L + nk, :]
        for g in range(N_KV):
            gsl = slice(g * gw, (g + 1) * gw)
            ow = _dot(p_scr[u * hl + g * gl:u * hl + (g + 1) * gl, :], vcat[:, gsl])
            og = jnp.where(slot == 0, ow[0:L, :], 0.0)
            for j in range(1, GROUP):
                og = jnp.where(slot == j, ow[j * L:(j + 1) * L, :], og)
            o_ref[u * L:(u + 1) * L, gsl] = og.astype(BF16)


def _attention(q, k_own, v_own, k_hist, v_hist, hist_map, row0, seq, bsz, masked, bucket, rel_bias, sinks):
    L = min(seq, CHUNK)
    lb = min(seq, ATTN_BLOCK)
    nc, in_map, out_map = _seq_specs(bsz, seq, lb, row0)
    n_sub = lb // L
    nk = WINDOW + L
    smem = functools.partial(pl.BlockSpec, memory_space=pltpu.SMEM)
    return pl.pallas_call(
        functools.partial(_attn_kernel, masked, L),
        grid=(bsz, nc),
        in_specs=[pl.BlockSpec((lb, D), in_map),
                  pl.BlockSpec((WINDOW, D), hist_map), pl.BlockSpec((lb, D), in_map),
                  pl.BlockSpec((WINDOW, D), hist_map), pl.BlockSpec((lb, D), in_map),
                  _full(bucket.shape), smem(), smem()],
        out_specs=pl.BlockSpec((lb, D), out_map),
        out_shape=jax.ShapeDtypeStruct((bsz * seq, D), BF16),
        scratch_shapes=[pltpu.VMEM(((1 + (min(n_sub, WINDOW // L) if masked else 0)) * N_HEADS * L, nk), F32),
                        pltpu.VMEM((n_sub * N_HEADS * L, nk), F32),
                        pltpu.VMEM((n_sub * N_HEADS * L, nk), BF16)],
        compiler_params=_params(("arbitrary", "arbitrary")),
        name="swa",
    )(q, k_hist, k_own, v_hist, v_own, bucket, rel_bias, sinks)


def _router_kernel(x_ref, g_ref, wr_ref, info_ref, cnt_ref):
    tm = x_ref.shape[0] // SUB

    @pl.when(pl.program_id(0) == 0)
    def _():
        cnt_ref[...] = jnp.zeros_like(cnt_ref)

    xn = _rms(_tile_rows_load(x_ref), g_ref[...])
    hi = xn.astype(BF16)
    lo = (xn - hi.astype(F32)).astype(BF16)
    w = wr_ref[...]
    whi = w.astype(BF16)
    wlo = (w - whi.astype(F32)).astype(BF16)
    logits = _dot(hi, whi) + _dot(hi, wlo) + _dot(lo, whi)
    lane = lax.broadcasted_iota(jnp.int32, (tm, LANES), 1)
    logits = jnp.where(lane < N_EXPERTS, logits, NEG_INF)
    m1 = jnp.max(logits, axis=1, keepdims=True)
    i1 = jnp.min(jnp.where(logits == m1, lane, LANES), axis=1, keepdims=True)
    rest = jnp.where(lane == i1, NEG_INF, logits)
    m2 = jnp.max(rest, axis=1, keepdims=True)
    i2 = jnp.min(jnp.where(rest == m2, lane, LANES), axis=1, keepdims=True)
    e2 = jnp.exp(m2 - m1)
    g1 = 1.0 / (1.0 + e2)
    g2 = e2 / (1.0 + e2)
    sel = (lane == i1) | (lane == i2)
    tri = (lax.broadcasted_iota(jnp.int32, (tm, tm), 0) > lax.broadcasted_iota(jnp.int32, (tm, tm), 1)).astype(BF16)
    rank = cnt_ref[...] + _dot(tri, sel.astype(BF16))
    r1 = jnp.sum(jnp.where(lane == i1, rank, 0.0), axis=1, keepdims=True)
    r2 = jnp.sum(jnp.where(lane == i2, rank, 0.0), axis=1, keepdims=True)
    cnt_ref[...] = cnt_ref[...] + jnp.sum(sel.astype(F32), axis=0, keepdims=True)
    info = jnp.where(lane == 0, i1.astype(F32), 0.0)
    info = jnp.where(lane == 1, i2.astype(F32), info)
    info = jnp.where(lane == 2, g1, info)
    info = jnp.where(lane == 3, g2, info)
    info = jnp.where(lane == 4, r1, info)
    info = jnp.where(lane == 5, r2, info)
    info_ref[...] = info


def _router(x, g, wr, tm):
    n = x.shape[0] // SUB
    return pl.pallas_call(
        _router_kernel,
        grid=(n // tm,),
        in_specs=[_tile_spec(tm), _full((1, D)), _full(wr.shape)],
        out_specs=[_rows(tm, LANES), _full((1, LANES))],
        out_shape=[jax.ShapeDtypeStruct((n, LANES), F32), jax.ShapeDtypeStruct((1, LANES), F32)],
        compiler_params=_params(("arbitrary",)),
        name="moe_router",
    )(x, g, wr)


def _token_tile(ref, t):
    return ref.at[pl.ds(pl.multiple_of(t * SUB, SUB), SUB)]


def _moe_scatter_kernel(te, zpos_ref, dest_ref, x_ref, xs_hbm, zero_s, sem):
    tm = x_ref.shape[0] // SUB
    span = te * SUB

    def rows_of(tile_start):
        return xs_hbm.at[pl.ds(pl.multiple_of(tile_start * SUB, span), span)]

    @pl.when(pl.program_id(0) == 0)
    def _():
        zero_s[...] = jnp.zeros_like(zero_s)
        for e in range(N_EXPERTS):
            pltpu.make_async_copy(zero_s, rows_of(zpos_ref[e]), sem).start()
        for e in range(N_EXPERTS):
            pltpu.make_async_copy(zero_s, rows_of(zpos_ref[e]), sem).wait()

        def clear_tail(t, carry):
            tail = pltpu.make_async_copy(zero_s, rows_of(t * te), sem)
            tail.start()
            tail.wait()
            return carry

        lax.fori_loop(zpos_ref[N_EXPERTS] // te, xs_hbm.shape[0] // span, clear_tail, 0)

    def copy(r, s):
        return pltpu.make_async_copy(_token_tile(x_ref, r), _token_tile(xs_hbm, dest_ref[s * tm + r]), sem)

    def start(r, carry):
        for s in range(2):
            copy(r, s).start(priority=s)
        return carry

    def wait(r, carry):
        for s in range(2):
            copy(r, s).wait()
        return carry

    lax.fori_loop(0, tm, start, 0)
    lax.fori_loop(0, tm, wait, 0)


def _moe_scatter(x, dest, zpos, n_rows, tm, te):
    n = x.shape[0] // SUB
    return pl.pallas_call(
        functools.partial(_moe_scatter_kernel, te),
        grid_spec=pltpu.PrefetchScalarGridSpec(
            num_scalar_prefetch=1,
            grid=(n // tm,),
            in_specs=[pl.BlockSpec((2 * tm,), lambda i, zp: (i,), memory_space=pltpu.SMEM),
                      pl.BlockSpec((tm * SUB, LANES), lambda i, zp: (i, 0))],
            out_specs=pl.BlockSpec(memory_space=pl.ANY),
            scratch_shapes=[pltpu.VMEM((te * SUB, LANES), F32), pltpu.SemaphoreType.DMA(())],
        ),
        out_shape=jax.ShapeDtypeStruct((n_rows * SUB, LANES), F32),
        compiler_params=_params(("arbitrary",)),
        name="moe_scatter",
    )(zpos, dest, x)


def _moe_ffn_kernel(n_ff, te, pe_ref, na_ref, xs_ref, g_ref, w1_ref, w3_ref, w2_ref, ys_ref, xn_s, acc_s):
    i, j = pl.program_id(0), pl.program_id(1)
    span = te * SUB
    for s in range(2):
        active = s < na_ref[i]
        rows = slice(s * te, (s + 1) * te)

        @pl.when(active & (j == 0))
        def _():
            xn_s[rows, :] = _rms(_tile_rows_load(xs_ref.at[pl.ds(s * span, span)]), g_ref[...]).astype(BF16)
            acc_s[rows, :] = jnp.zeros((te, D), F32)

        @pl.when(active)
        def _():
            xn = xn_s[rows, :]
            a = (_silu(_dot(xn, w1_ref[0])) * _dot(xn, w3_ref[0])).astype(BF16)
            acc_s[rows, :] += _dot(a, w2_ref[0])

        @pl.when(j == n_ff - 1)
        def _():
            _tile_rows_store(ys_ref.at[pl.ds(s * span, span)], jnp.where(active, acc_s[rows, :], 0.0))


def _moe_ffn(xs, g, w1, w3, w2, pair_expert, pair_active, te):
    n_pairs = pair_expert.shape[0]
    d_ff = w1.shape[2]
    n_ff = FF_SPLIT_MOE
    tf = d_ff // n_ff

    def ff(i, j, na_r):
        return jnp.where(na_r[i] > 0, j, n_ff - 1)

    rows = pl.BlockSpec((2 * te * SUB, LANES), lambda i, j, pe_r, na_r: (i, 0))
    return pl.pallas_call(
        functools.partial(_moe_ffn_kernel, n_ff, te),
        grid_spec=pltpu.PrefetchScalarGridSpec(
            num_scalar_prefetch=2,
            grid=(n_pairs, n_ff),
            in_specs=[rows,
                      pl.BlockSpec((1, D), lambda i, j, pe_r, na_r: (0, 0)),
                      pl.BlockSpec((1, D, tf), lambda i, j, pe_r, na_r: (pe_r[i], 0, ff(i, j, na_r))),
                      pl.BlockSpec((1, D, tf), lambda i, j, pe_r, na_r: (pe_r[i], 0, ff(i, j, na_r))),
                      pl.BlockSpec((1, tf, D), lambda i, j, pe_r, na_r: (pe_r[i], ff(i, j, na_r), 0))],
            out_specs=rows,
            scratch_shapes=[pltpu.VMEM((2 * te, D), BF16), pltpu.VMEM((2 * te, D), F32)],
        ),
        out_shape=jax.ShapeDtypeStruct((n_pairs * 2 * te * SUB, LANES), F32),
        compiler_params=_params(("arbitrary", "arbitrary")),
        name="moe_ffn",
    )(pair_expert, pair_active, xs, g, w1, w3, w2)


def _moe_combine_kernel(n_p, dest_ref, info_ref, x_ref, ys_hbm, op_ref, os_ref, buf, sem):
    tm = x_ref.shape[0] // SUB
    i = pl.program_id(0)

    def copy(r, s):
        return pltpu.make_async_copy(_token_tile(ys_hbm, dest_ref[s * tm + r]), _token_tile(buf.at[s], r), sem)

    def start(r, carry):
        for s in range(2):
            copy(r, s).start(priority=s)
        return carry

    def wait(r, carry):
        for s in range(2):
            copy(r, s).wait()
        return carry

    lax.fori_loop(0, tm, start, 0)
    lax.fori_loop(0, tm, wait, 0)
    out = (_tile_rows_load(x_ref) + info_ref[:, 2:3] * _tile_rows_load(buf.at[0])
           + info_ref[:, 3:4] * _tile_rows_load(buf.at[1]))

    @pl.when(i < n_p)
    def _():
        op_ref[...] = out

    @pl.when(i >= n_p)
    def _():
        os_ref[...] = out


def _moe_combine(x, info, dest, ys, tm, n_p, n_s):
    out_p, out_s = _pair(tm, D, n_p)
    return pl.pallas_call(
        functools.partial(_moe_combine_kernel, n_p),
        grid=(n_p + n_s,),
        in_specs=[pl.BlockSpec((2 * tm,), lambda i: (i,), memory_space=pltpu.SMEM),
                  _rows(tm, LANES), _tile_spec(tm), pl.BlockSpec(memory_space=pl.ANY)],
        out_specs=[out_p, out_s],
        out_shape=[jax.ShapeDtypeStruct((n_p * tm, D), F32), jax.ShapeDtypeStruct((n_s * tm, D), F32)],
        scratch_shapes=[pltpu.VMEM((2, tm * SUB, LANES), F32), pltpu.SemaphoreType.DMA(())],
        compiler_params=_params(("arbitrary",)),
        name="moe_combine",
    )(dest, info, x, ys)


def _moe(x, g, w_router, w1, w3, w2, tm, n_p, n_s):
    n = x.shape[0] // SUB
    te = min(EXPERT_TILE, tm)
    wr = jnp.zeros((D, LANES), F32).at[:, :N_EXPERTS].set(w_router)
    info, cnt = _router(x, g, wr, tm)
    e1, e2 = info[:, 0].astype(jnp.int32), info[:, 1].astype(jnp.int32)
    r1, r2 = info[:, 4].astype(jnp.int32), info[:, 5].astype(jnp.int32)
    counts = cnt[0, :N_EXPERTS].astype(jnp.int32)
    region = 2 * te
    padded = (counts + region - 1) // region * region
    ends = jnp.cumsum(padded)
    starts = ends - padded
    dest = jnp.stack([(starts[e1] + r1).reshape(n // tm, tm), (starts[e2] + r2).reshape(n // tm, tm)], axis=1)
    dest = dest.reshape(-1)
    n_pairs = (2 * n + N_EXPERTS * (region - 1)) // region
    pair_start = jnp.arange(n_pairs, dtype=jnp.int32) * region
    pair_expert = jnp.minimum(jnp.sum(pair_start[:, None] >= ends[None, :], axis=1), N_EXPERTS - 1).astype(jnp.int32)
    used_end = starts + (counts + te - 1) // te * te
    pair_active = jnp.clip((used_end[pair_expert] - pair_start) // te, 0, 2).astype(jnp.int32)
    last_region = jnp.maximum(ends - region, 0).astype(jnp.int32)
    xs = _moe_scatter(x, dest, jnp.concatenate([last_region, ends[-1:].astype(jnp.int32)]), n_pairs * region, tm,
                      region)
    ys = _moe_ffn(xs, g, w1, w3, w2, pair_expert, pair_active, te)
    return _moe_combine(x, info, dest, ys, tm, n_p, n_s)


def _t5_bucket(rel):
    n = -rel
    half = NUM_BUCKETS // 2
    ret = jnp.where(n < 0, half, 0)
    n = jnp.abs(n)
    max_exact = half // 2
    nf = jnp.maximum(n, 1).astype(F32)
    large = max_exact + (jnp.log(nf / max_exact) / math.log(MAX_DISTANCE / max_exact)
                         * (half - max_exact)).astype(jnp.int32)
    large = jnp.minimum(large, half - 1)
    return ret + jnp.where(n < max_exact, n, large)


def _bucket_table(length):
    kpos = jnp.arange(WINDOW + length) - WINDOW
    return _t5_bucket(kpos[None, :] - jnp.arange(length)[:, None]).astype(jnp.int32)


def kernel(x_prompt, x_sample, state_conv_a, state_rglru_h, state_conv_b, state_mlstm_c, state_mlstm_n, state_mlstm_m, cache_swa_k, cache_swa_v, norm_mix, norm_ffn, w_in_ab, w_conv_a, b_conv_a, w_rg_a, b_rg_a, w_rg_x, b_rg_x, rg_lambda, w_conv_b, b_conv_b, w_q_b, w_k_b, w_v_b, w_if_b, b_if_b, g_hnorm_b, skip_b, w_out_ab, w1_dense, w3_dense, w2_dense, w_in_att, g_qnorm, g_knorm, sinks, w_out_att, rel_bias, w_router, w1_moe, w3_moe, w2_moe):
    bp, tp, _ = x_prompt.shape
    bs, ts, _ = x_sample.shape
    assert norm_mix.shape[0] == 2 and w_in_ab.shape[0] == 1 and w_in_att.shape[0] == 1
    assert tp % CHUNK == 0 and ts <= CHUNK and cache_swa_k.shape[2] == WINDOW
    rows_p, rows_s = bp * tp, bs * ts
    tm = math.gcd(math.gcd(rows_p, rows_s), TOKEN_TILE)
    n_p, n_s = rows_p // tm, rows_s // tm
    xp = x_prompt.reshape(rows_p, D)
    xs = x_sample.reshape(rows_s, D)
    bf = lambda w: w.astype(BF16)
    vec = lambda v: v.reshape(1, -1)

    xa, ga, xm, z = _in_proj_ab(xp, xs, vec(norm_mix[0]), bf(w_in_ab[0]), tm)
    rg_w = (w_conv_a[0], vec(b_conv_a[0]), bf(w_rg_a[0]), vec(b_rg_a[0]), bf(w_rg_x[0]), vec(b_rg_x[0]),
            vec(rg_lambda[0]))
    ya_p, hl_p = _rglru(xa, ga, 0, tp, jnp.zeros((bp, CONV_W - 1, D), F32), jnp.zeros((bp, 1, D), F32), *rg_w)
    ya_s, hl_s = _rglru(xa, ga, rows_p, ts, state_conv_a[0], state_rglru_h[0].reshape(bs, 1, D), *rg_w)

    def gate_cols(w):
        pad = ((0, 0), (0, LANES - NH_B))
        return jnp.concatenate([jnp.pad(w[:, :NH_B], pad), jnp.pad(w[:, NH_B:], pad)], axis=1)

    pre_w = (w_conv_b[0], vec(b_conv_b[0]), bf(w_q_b[0]), bf(w_k_b[0]), bf(w_v_b[0]), bf(w_if_b[0].T),
             b_if_b[0].reshape(2 * NH_B, 1), bf(gate_cols(w_if_b[0])), gate_cols(b_if_b[0][None]))
    rec_w = (g_hnorm_b[0].reshape(1, D), vec(skip_b[0]))
    lanes = lambda m: jnp.pad(m, ((0, 0), (0, LANES - NH_B)))[:, None, :]
    pre_p = _mlstm_pre(xm, 0, tp, jnp.zeros((bp, CONV_W - 1, D), F32), jnp.zeros((bp, 1, LANES), F32), *pre_w)
    pre_s = _mlstm_pre(xm, rows_p, ts, state_conv_b[0], lanes(state_mlstm_m[0]), *pre_w)
    m_p, m_s = pre_p[4], pre_s[4]
    yb_p, c_p, nn_p = _mlstm(*pre_p[1:4], pre_p[0], z, 0, tp, jnp.zeros((bp, NH_B, DH_B, DH_B), F32),
                             jnp.zeros((bp, NH_B, DH_B), F32), *rec_w)
    yb_s, c_s, nn_s = _mlstm(*pre_s[1:4], pre_s[0], z, rows_p, ts, state_mlstm_c[0], state_mlstm_n[0], *rec_w)

    w_out = bf(w_out_ab[0])
    y = _out_proj((xp, xs), [(ya_p, ya_s), (yb_p, yb_s)], [w_out[:D], w_out[D:]], tm, n_p)
    y = _ffn_dense(y, vec(norm_ffn[0]), bf(w1_dense[0]), bf(w3_dense[0]), bf(w2_dense[0]), tm)

    tile2 = lambda gain: jnp.tile(gain, D_KV // HEAD_DIM).reshape(1, D_KV)
    q, kw, vw, k, v = _in_proj_att(y, vec(norm_mix[1]), bf(w_in_att[0]), tile2(g_qnorm[0]), tile2(g_knorm[0]), tm)
    sinks2 = sinks[0].reshape(1, N_HEADS)
    wpb = min(tp, ATTN_BLOCK) // WINDOW
    assert wpb >= 1 and tp % (wpb * WINDOW) == 0
    hist_p = lambda bi, ci: (bi * (tp // WINDOW) + jnp.maximum(ci * wpb - 1, 0), 0)
    o_p = _attention(q, kw, vw, kw, vw, hist_p, 0, tp, bp, True, _bucket_table(CHUNK), rel_bias, sinks2)

    def widen(cache):
        wide = jnp.broadcast_to(cache[:, :, :, None, :], (bs, WINDOW, N_KV, GROUP, HEAD_DIM))
        return wide.reshape(bs * WINDOW, D).astype(BF16)

    ck, cv = widen(cache_swa_k[0]), widen(cache_swa_v[0])
    hist_s = lambda bi, ci: (bi, 0)
    o_s = _attention(q, kw, vw, ck, cv, hist_s, rows_p, ts, bs, False, _bucket_table(ts), rel_bias, sinks2)
    y = _out_proj(y, [(o_p, o_s)], [bf(w_out_att[0])], tm, n_p, tiled_out=True)
    yp, ys = _moe(y, vec(norm_ffn[1]), w_router[0], bf(w1_moe[0]), bf(w3_moe[0]), bf(w2_moe[0]), tm, n_p, n_s)

    def tail(a, rows, b, t, keep):
        if rows == 0:
            return jnp.stack([a[(i + 1) * t - keep:(i + 1) * t] for i in range(b)])
        return a[rows:rows + b * t].reshape(b, t, -1)[:, t - keep:]

    kv4 = lambda a, b: a.reshape(b, -1, N_KV, HEAD_DIM)
    one = lambda a: a[None]
    k_s = jnp.concatenate([cache_swa_k[0][:, ts:], kv4(k[rows_p:], bs)], axis=1)
    v_s = jnp.concatenate([cache_swa_v[0][:, ts:], kv4(v[rows_p:], bs)], axis=1)
    return (yp.reshape(bp, tp, D), ys.reshape(bs, ts, D),
            one(tail(xa, 0, bp, tp, CONV_W - 1)), one(hl_p.reshape(bp, D)), one(tail(xm, 0, bp, tp, CONV_W - 1)),
            one(c_p), one(nn_p), one(m_p[:, 0, :NH_B]),
            one(kv4(tail(k, 0, bp, tp, WINDOW), bp)), one(kv4(tail(v, 0, bp, tp, WINDOW), bp)),
            one(tail(xa, rows_p, bs, ts, CONV_W - 1)), one(hl_s.reshape(bs, D)),
            one(tail(xm, rows_p, bs, ts, CONV_W - 1)), one(c_s), one(nn_s), one(m_s[:, 0, :NH_B]),
            one(k_s), one(v_s))
```

```python
import functools
import math

import jax
import jax.numpy as jnp
from jax import lax
from jax.experimental import pallas as pl
from jax.experimental.pallas import tpu as pltpu

F32 = jnp.float32
BF16 = jnp.bfloat16

D = 1024
CHUNK = 64
CONV_W = 4
NB_A = 8
BS_A = D // NB_A
RG_C = 8.0
NH_B = 4
DH_B = D // NH_B
N_HEADS = 16
HEAD_DIM = D // N_HEADS
N_KV = 4
GROUP = N_HEADS // N_KV
D_KV = N_KV * HEAD_DIM
WINDOW = 128
NUM_BUCKETS = 32
MAX_DISTANCE = 128
N_EXPERTS = 8
EPS = 1e-6
LANES = 128
SUB = D // LANES
NEG_INF = float("-inf")
MIN_NORMAL = 1.1754944e-38

TOKEN_TILE = 512
EXPERT_TILE = 512
FF_SPLIT_DENSE = 3
FF_SPLIT_MOE = 2
SEQ_BLOCK = 256
RGLRU_BLOCK = 512
SCAN_ROWS = 8
ATTN_BLOCK = 256
VMEM_LIMIT = 56 * 1024 * 1024


def _params(sem):
    return pltpu.CompilerParams(dimension_semantics=sem, vmem_limit_bytes=VMEM_LIMIT)


def _full(shape):
    return pl.BlockSpec(shape, lambda *_: (0,) * len(shape))


def _resident(shape):
    return pl.BlockSpec(shape, lambda *_: (0,) * len(shape), pipeline_mode=pl.Buffered(1))


def _rows(tm, c):
    return pl.BlockSpec((tm, c), lambda i: (i, 0))


def _pair(tm, c, n_p):
    return [pl.BlockSpec((tm, c), lambda i: (jnp.minimum(i, n_p - 1), 0)),
            pl.BlockSpec((tm, c), lambda i: (jnp.maximum(i - n_p, 0), 0))]


def _rms(x, g):
    ms = jnp.mean(x * x, axis=-1, keepdims=True)
    return x * lax.rsqrt(ms + EPS) * g


def _silu(x):
    return x * jax.nn.sigmoid(x)


def _sigmoid_tanh(x):
    return 0.5 * jnp.tanh(0.5 * x) + 0.5


def _softplus(x):
    return jnp.maximum(x, 0.0) + jnp.log1p(jnp.exp(-jnp.abs(x)))


def _gelu_tanh(x):
    c = math.sqrt(2.0 / math.pi)
    return x * (0.5 * (1.0 + jnp.tanh(c * (x + 0.044715 * (x * x * x)))))


def _dot(a, b):
    return jnp.dot(a, b, preferred_element_type=F32)


def _dot_nt(a, b):
    return lax.dot_general(a, b, (((1,), (1,)), ((), ())), preferred_element_type=F32)


def _dot_tn(a, b):
    return lax.dot_general(a, b, (((0,), (0,)), ((), ())), preferred_element_type=F32)


def _in_proj_ab_kernel(n_p, xp_ref, xs_ref, g_ref, w_ref, *o_refs):
    i = pl.program_id(0)
    x = jnp.where(i < n_p, xp_ref[...], xs_ref[...])
    xn = _rms(x, g_ref[...]).astype(BF16)
    for c, o_ref in enumerate(o_refs):
        o_ref[...] = _dot(xn, w_ref[:, c * D:(c + 1) * D])


def _in_proj_ab(xp, xs, g, w, tm):
    n_p, n_s = xp.shape[0] // tm, xs.shape[0] // tm
    n = xp.shape[0] + xs.shape[0]
    n_out = w.shape[1] // D
    return pl.pallas_call(
        functools.partial(_in_proj_ab_kernel, n_p),
        grid=(n_p + n_s,),
        in_specs=_pair(tm, D, n_p) + [_full((1, D)), _resident(w.shape)],
        out_specs=[_rows(tm, D)] * n_out,
        out_shape=[jax.ShapeDtypeStruct((n, D), F32)] * n_out,
        compiler_params=_params(("parallel",)),
        name="in_proj_ab",
    )(xp, xs, g, w)


def _tile_rows_load(ref):
    tm = ref.shape[0] // SUB
    return jnp.concatenate([ref[pl.ds(s, tm, stride=SUB), :] for s in range(SUB)], axis=1)


def _tile_rows_store(ref, val):
    tm = val.shape[0]
    for s in range(SUB):
        ref[pl.ds(s, tm, stride=SUB), :] = val[:, s * LANES:(s + 1) * LANES]


def _tile_spec(tm):
    return pl.BlockSpec((tm * SUB, LANES), lambda i: (i, 0))


def _out_proj_kernel(n_p, n_res, n_a, tiled_out, *refs):
    i = pl.program_id(0)
    res_refs = refs[:n_res]
    a_refs = refs[n_res:n_res + 2 * n_a]
    w_refs = refs[n_res + 2 * n_a:n_res + 3 * n_a]
    o_ref = refs[-1]
    if n_res == 2:
        acc = jnp.where(i < n_p, res_refs[0][...], res_refs[1][...])
    else:
        acc = res_refs[0][...]
    for k in range(n_a):
        a = jnp.where(i < n_p, a_refs[2 * k][...], a_refs[2 * k + 1][...])
        acc = acc + _dot(a, w_refs[k][...])
    if tiled_out:
        _tile_rows_store(o_ref, acc)
    else:
        o_ref[...] = acc


def _out_proj(res, a_pairs, ws, tm, n_p, tiled_out=False):
    res = res if isinstance(res, tuple) else (res,)
    n = sum(a.shape[0] for a in a_pairs[0])
    res_specs = _pair(tm, D, n_p) if len(res) == 2 else [_rows(tm, D)]
    a_specs = []
    for a in a_pairs:
        a_specs += _pair(tm, a[0].shape[1], n_p)
    return pl.pallas_call(
        functools.partial(_out_proj_kernel, n_p, len(res), len(a_pairs), tiled_out),
        grid=(n // tm,),
        in_specs=res_specs + a_specs + [_resident(w.shape) for w in ws],
        out_specs=_tile_spec(tm) if tiled_out else _rows(tm, D),
        out_shape=jax.ShapeDtypeStruct((n * SUB, LANES) if tiled_out else (n, D), F32),
        compiler_params=_params(("parallel",)),
        name="out_proj",
    )(*res, *[x for a in a_pairs for x in a], *ws)


def _ffn_dense_kernel(x_ref, g_ref, w1_ref, w3_ref, w2_ref, o_ref):
    x = x_ref[...]
    xn = _rms(x, g_ref[...]).astype(BF16)
    step = w1_ref.shape[1] // FF_SPLIT_DENSE
    acc = x
    for c in range(FF_SPLIT_DENSE):
        sl = slice(c * step, (c + 1) * step)
        a = (_silu(_dot(xn, w1_ref[:, sl])) * _dot(xn, w3_ref[:, sl])).astype(BF16)
        acc = acc + _dot(a, w2_ref[sl, :])
    o_ref[...] = acc


def _ffn_dense(x, g, w1, w3, w2, tm):
    n = x.shape[0]
    return pl.pallas_call(
        _ffn_dense_kernel,
        grid=(n // tm,),
        in_specs=[_rows(tm, D), _full((1, D)), _resident(w1.shape), _resident(w3.shape), _resident(w2.shape)],
        out_specs=_rows(tm, D),
        out_shape=jax.ShapeDtypeStruct((n, D), F32),
        compiler_params=_params(("parallel",)),
        name="ffn_dense",
    )(x, g, w1, w3, w2)


def _in_proj_att_kernel(x_ref, g_ref, w_ref, gq_ref, gk_ref, q_ref, kw_ref, vw_ref, k_ref, v_ref):
    xn = _rms(x_ref[...], g_ref[...]).astype(BF16)
    cw = D_KV
    r = lax.broadcasted_iota(jnp.int32, (cw, cw), 0) // HEAD_DIM
    c = lax.broadcasted_iota(jnp.int32, (cw, cw), 1) // HEAD_DIM
    group_ones = (r == c).astype(BF16)
    er = lax.broadcasted_iota(jnp.int32, (D_KV, D), 0)
    ec = lax.broadcasted_iota(jnp.int32, (D_KV, D), 1)
    widen = ((er // HEAD_DIM == ec // (GROUP * HEAD_DIM)) & (er % HEAD_DIM == ec % HEAD_DIM)).astype(BF16)

    def head_norm(y, gain):
        ms = _dot((y * y).astype(BF16), group_ones) * (1.0 / HEAD_DIM)
        return y * lax.rsqrt(ms + EPS) * gain

    for b in range(D // cw):
        y = _dot(xn, w_ref[:, b * cw:(b + 1) * cw])
        q_ref[:, b * cw:(b + 1) * cw] = (head_norm(y, gq_ref[...]) * (HEAD_DIM ** -0.5)).astype(BF16)
    k = head_norm(_dot(xn, w_ref[:, D:D + D_KV]), gk_ref[...])
    v = _dot(xn, w_ref[:, D + D_KV:D + 2 * D_KV])
    k_ref[...] = k
    v_ref[...] = v
    kw_ref[...] = _dot(k.astype(BF16), widen).astype(BF16)
    vw_ref[...] = _dot(v.astype(BF16), widen).astype(BF16)


def _in_proj_att(x, g, w, gq, gk, tm):
    n = x.shape[0]
    return pl.pallas_call(
        _in_proj_att_kernel,
        grid=(n // tm,),
        in_specs=[_rows(tm, D), _full((1, D)), _resident(w.shape), _full((1, D_KV)), _full((1, D_KV))],
        out_specs=[_rows(tm, D)] * 3 + [_rows(tm, D_KV)] * 2,
        out_shape=[jax.ShapeDtypeStruct((n, D), BF16)] * 3 + [jax.ShapeDtypeStruct((n, D_KV), F32)] * 2,
        compiler_params=_params(("parallel",)),
        name="in_proj_att",
    )(x, g, w, gq, gk)


def _causal_conv(x, xbuf, conv0_ref, wc_ref, bc_ref, first):
    tb = x.shape[0]

    @pl.when(first)
    def _():
        xbuf[5:8, :] = conv0_ref[0]

    xbuf[8:8 + tb, :] = x
    y = bc_ref[...] + xbuf[5:5 + tb, :] * wc_ref[0:1, :]
    for j in range(1, CONV_W):
        y = y + xbuf[5 + j:5 + j + tb, :] * wc_ref[j:j + 1, :]
    xbuf[5:8, :] = xbuf[5 + tb:8 + tb, :]
    return y


def _seq_specs(bsz, seq, tb, row0):
    nc = seq // tb
    off = row0 // tb
    return nc, (lambda bi, ci: (off + bi * nc + ci, 0)), (lambda bi, ci: (bi * nc + ci, 0))


def _scan_by_doubling(a, u, pos, axis, length):
    d = 1
    while d < length:
        keep = pos >= d
        u = jnp.where(keep, a * pltpu.roll(u, d, axis) + u, u)
        a = jnp.where(keep, a * pltpu.roll(a, d, axis), a)
        d *= 2
    return a, u


def _linear_scan(a, u, h0, row):
    tb = a.shape[0]
    sub = min(tb, SCAN_ROWS)
    out = []
    for k in range(tb // sub):
        rows = slice(k * sub, (k + 1) * sub)
        a_k, u_k = _scan_by_doubling(a[rows, :], u[rows, :], row[0:sub, :], 0, sub)
        h_k = a_k * h0 + u_k
        h0 = h_k[sub - 1:sub, :]
        out.append(h_k)
    return jnp.concatenate(out, axis=0)


def _rglru_kernel(xa_ref, ga_ref, conv0_ref, h0_ref, wc_ref, bc_ref, wa_ref, ba_ref, wx_ref, bx_ref, lam_ref,
                  ya_ref, hl_ref, xbuf):
    first = pl.program_id(1) == 0
    tb = xa_ref.shape[0]

    @pl.when(first)
    def _():
        hl_ref[0] = h0_ref[0]

    y = _causal_conv(xa_ref[...], xbuf, conv0_ref, wc_ref, bc_ref, first)
    row = lax.broadcasted_iota(jnp.int32, (tb, BS_A), 0)
    for n in range(NB_A):
        sl = slice(n * BS_A, (n + 1) * BS_A)
        yn = y[:, sl]
        yb = yn.astype(BF16)
        r = _sigmoid_tanh(_dot(yb, wa_ref[n]) + ba_ref[:, sl])
        ig = _sigmoid_tanh(_dot(yb, wx_ref[n]) + bx_ref[:, sl])
        log_a = (-RG_C) * r * _softplus(-lam_ref[:, sl])
        a = jnp.exp(log_a)
        th = jnp.tanh(log_a)
        gain2 = -2.0 * th / (1.0 - th)
        u = yn * ig * (gain2 * lax.rsqrt(jnp.maximum(gain2, MIN_NORMAL)))
        h = _linear_scan(a, u, hl_ref[0, :, sl], row)
        hl_ref[0, :, sl] = h[tb - 1:tb, :]
        ya_ref[:, sl] = (h * _gelu_tanh(ga_ref[:, sl])).astype(BF16)


def _rglru(xa, ga, row0, seq, conv0, h0, wc, bc, wa, ba, wx, bx, lam):
    bsz = conv0.shape[0]
    tb = math.gcd(seq, RGLRU_BLOCK)
    nc, in_map, out_map = _seq_specs(bsz, seq, tb, row0)
    state3 = lambda bi, ci: (bi, 0, 0)
    return pl.pallas_call(
        _rglru_kernel,
        grid=(bsz, nc),
        in_specs=[pl.BlockSpec((tb, D), in_map), pl.BlockSpec((tb, D), in_map),
                  pl.BlockSpec((1, CONV_W - 1, D), state3), pl.BlockSpec((1, 1, D), state3),
                  _full(wc.shape), _full(bc.shape), _full(wa.shape), _full(ba.shape), _full(wx.shape),
                  _full(bx.shape), _full(lam.shape)],
        out_specs=[pl.BlockSpec((tb, D), out_map), pl.BlockSpec((1, 1, D), state3)],
        out_shape=[jax.ShapeDtypeStruct((bsz * seq, D), BF16), jax.ShapeDtypeStruct((bsz, 1, D), F32)],
        scratch_shapes=[pltpu.VMEM((tb + 8, D), F32)],
        compiler_params=_params(("parallel", "arbitrary")),
        name="rglru",
    )(xa, ga, conv0, h0, wc, bc, wa, ba, wx, bx, lam)


def _mlstm_pre_kernel(chunk, xm_ref, conv0_ref, m0_ref, wc_ref, bc_ref, wq_ref, wk_ref, wv_ref, wif_ref, bif_ref,
                      wifc_ref, bifc_ref, xc_ref, qkv_ref, w_ref, col_ref, m_ref, xbuf):
    first = pl.program_id(1) == 0
    tb = xm_ref.shape[0]
    L = chunk

    @pl.when(first)
    def _():
        m_ref[...] = m0_ref[...]

    x = xm_ref[...]
    xc = _silu(_causal_conv(x, xbuf, conv0_ref, wc_ref, bc_ref, first))
    xc_ref[...] = xc
    g_row = bif_ref[...]
    g_col = bifc_ref[...]
    for h in range(NH_B):
        sl = slice(h * DH_B, (h + 1) * DH_B)
        xch = xc[:, sl].astype(BF16)
        q = _dot(xch, wq_ref[h]).astype(BF16)
        k = _dot(xch, wk_ref[h]).astype(BF16)
        v = _dot(x[:, sl].astype(BF16), wv_ref[h]).astype(BF16)
        for part, val in enumerate((q, k, v)):
            psl = slice(part * D + h * DH_B, part * D + (h + 1) * DH_B)
            qkv_ref[:, psl] = val
            g_row = g_row + _dot_nt(wif_ref[:, psl], val)
            g_col = g_col + _dot(val, wifc_ref[psl, :])

    def exact_dot(a, b01):
        p1 = a.astype(BF16)
        r1 = a - p1.astype(F32)
        p2 = r1.astype(BF16)
        p3 = (r1 - p2.astype(F32)).astype(BF16)
        return _dot(p1, b01) + _dot(p2, b01) + _dot(p3, b01)

    def exact_dot_left(b01, a):
        p1 = a.astype(BF16)
        r1 = a - p1.astype(F32)
        p2 = r1.astype(BF16)
        p3 = (r1 - p2.astype(F32)).astype(BF16)
        return _dot(b01, p1) + _dot(b01, p2) + _dot(b01, p3)

    tt = lax.broadcasted_iota(jnp.int32, (L, L), 0)
    ss = lax.broadcasted_iota(jnp.int32, (L, L), 1)
    upper = (tt <= ss).astype(BF16)
    lower = (ss <= tt).astype(BF16)
    i_row = g_row[0:NH_B, :]
    b_row = exact_dot(-_softplus(-g_row), upper)[NH_B:2 * NH_B, :]
    w_ref[0] = i_row - b_row
    i_col = g_col[:, 0:LANES]
    b_col = exact_dot_left(lower, -_softplus(-g_col[:, LANES:2 * LANES]))
    w_col = i_col - b_col
    trow = lax.broadcasted_iota(jnp.int32, (L, LANES), 0)
    run_max = w_col
    d = 1
    while d < L:
        run_max = jnp.where(trow >= d, jnp.maximum(run_max, pltpu.roll(run_max, d, 0)), run_max)
        d *= 2
    m_prev = m_ref[0]
    big_m = jnp.maximum(m_prev, run_max)
    m_last = big_m[L - 1:L, :]
    m_ref[0] = b_col[L - 1:L, :] + m_last
    head_lane = lax.broadcasted_iota(jnp.int32, (L, LANES), 1) < NH_B
    table = jnp.where(head_lane, big_m, 0.0)
    for c, col in enumerate((jnp.exp(m_prev - big_m), jnp.exp(-(b_col + big_m)), jnp.exp(w_col - m_last))):
        table = table + pltpu.roll(jnp.where(head_lane, col, 0.0), (c + 1) * NH_B, 1)
    col_ref[...] = table


def _mlstm_pre(xm, row0, seq, conv0, m0, wc, bc, wq, wk, wv, wif_t, bif, wif_c, bif_c):
    bsz = conv0.shape[0]
    tb = min(seq, SEQ_BLOCK)
    chunk = tb
    nc, in_map, out_map = _seq_specs(bsz, seq, tb, row0)
    n = bsz * seq
    st3 = lambda bi, ci: (bi, 0, 0)
    return pl.pallas_call(
        functools.partial(_mlstm_pre_kernel, chunk),
        grid=(bsz, nc),
        in_specs=[pl.BlockSpec((tb, D), in_map), pl.BlockSpec((1, CONV_W - 1, D), st3),
                  pl.BlockSpec((1, 1, LANES), st3),
                  _full(wc.shape), _full(bc.shape), _full(wq.shape), _full(wk.shape), _full(wv.shape),
                  _full(wif_t.shape), _full(bif.shape), _full(wif_c.shape), _full(bif_c.shape)],
        out_specs=[pl.BlockSpec((tb, D), out_map), pl.BlockSpec((tb, 3 * D), out_map)]
                  + [pl.BlockSpec((tb // chunk, NH_B, chunk), lambda bi, ci: (bi * nc + ci, 0, 0)),
                     pl.BlockSpec((tb, LANES), out_map), pl.BlockSpec((1, 1, LANES), st3)],
        out_shape=[jax.ShapeDtypeStruct((n, D), F32), jax.ShapeDtypeStruct((n, 3 * D), BF16)]
                  + [jax.ShapeDtypeStruct((n // chunk, NH_B, chunk), F32), jax.ShapeDtypeStruct((n, LANES), F32),
                     jax.ShapeDtypeStruct(m0.shape, F32)],
        scratch_shapes=[pltpu.VMEM((tb + 8, D), F32)],
        compiler_params=_params(("parallel", "arbitrary")),
        name="mlstm_pre",
    )(xm, conv0, m0, wc, bc, wq, wk, wv, wif_t, bif, wif_c, bif_c)


def _mlstm_kernel(chunk, qkv_ref, w_ref, col_ref, xc_ref, z_ref, c0_ref, n0_ref, gh_ref, skip_ref,
                  yb_ref, c_ref, n_ref):
    tb = qkv_ref.shape[0]
    L = chunk
    scale = DH_B ** -0.5

    @pl.when(pl.program_id(1) == 0)
    def _():
        c_ref[...] = c0_ref[...]
        n_ref[...] = n0_ref[...]

    causal = lax.broadcasted_iota(jnp.int32, (L, L), 1) <= lax.broadcasted_iota(jnp.int32, (L, L), 0)
    heads = range(NH_B)
    hsl = [slice(h * DH_B, (h + 1) * DH_B) for h in heads]
    for j in range(tb // L):
        rows = slice(j * L, (j + 1) * L)
        cols = col_ref[rows, :]
        col = lambda c, h: cols[:, c * NH_B + h:c * NH_B + h + 1]
        part = lambda p, h: qkv_ref[rows, p * D + h * DH_B:p * D + (h + 1) * DH_B]
        q = [part(0, h) for h in heads]
        k = [part(1, h) for h in heads]
        v = [part(2, h) for h in heads]
        s = [_dot_nt(q[h], k[h]) for h in heads]
        qc = [_dot(q[h], c_ref[0, h].astype(BF16)) for h in heads]
        qn = [_dot_nt(q[h], jnp.broadcast_to(n_ref[0, h:h + 1, :], (8, DH_B)).astype(BF16))[:, 0:1] for h in heads]
        p = [(s[h] * scale * jnp.where(causal, jnp.exp(w_ref[j, h:h + 1, :] - col(0, h)), 0.0)).astype(BF16)
             for h in heads]
        pv = [_dot(p[h], v[h]) for h in heads]
        psum = [_dot(p[h], jnp.ones((L, LANES), BF16))[:, 0:1] for h in heads]
        for h in heads:
            sc = col(1, h)
            keep = sc[L - 1:L, :]
            kd = k[h].astype(F32) * (col(3, h) * scale)
            c_ref[0, h] = keep * c_ref[0, h] + _dot_tn(kd.astype(BF16), v[h])
            n_ref[0, h:h + 1, :] = keep * n_ref[0, h:h + 1, :] + jnp.sum(kd, axis=0, keepdims=True)
            num = pv[h] + sc * qc[h]
            den = psum[h] + sc * qn[h]
            hs = num / jnp.maximum(jnp.abs(den), col(2, h))
            mu = jnp.mean(hs, axis=1, keepdims=True)
            dev = hs - mu
            var = jnp.mean(dev * dev, axis=1, keepdims=True)
            hn = dev * lax.rsqrt(var + EPS) * gh_ref[:, hsl[h]]
            out = (hn + skip_ref[:, hsl[h]] * xc_ref[rows, hsl[h]]) * _silu(z_ref[rows, hsl[h]])
            yb_ref[rows, hsl[h]] = out.astype(BF16)


def _mlstm(qkv, w, col, xc, z, row0, seq, c0, n0, gh, skip):
    bsz = c0.shape[0]
    tb = min(seq, SEQ_BLOCK)
    chunk = tb
    nc, z_map, own_map = _seq_specs(bsz, seq, tb, row0)
    st4 = lambda bi, ci: (bi, 0, 0, 0)
    st3 = lambda bi, ci: (bi, 0, 0)
    return pl.pallas_call(
        functools.partial(_mlstm_kernel, chunk),
        grid=(bsz, nc),
        in_specs=[pl.BlockSpec((tb, 3 * D), own_map)]
                 + [pl.BlockSpec((tb // chunk, NH_B, chunk), lambda bi, ci: (bi * nc + ci, 0, 0)),
                    pl.BlockSpec((tb, LANES), own_map),
                    pl.BlockSpec((tb, D), own_map), pl.BlockSpec((tb, D), z_map),
                    pl.BlockSpec((1, NH_B, DH_B, DH_B), st4), pl.BlockSpec((1, NH_B, DH_B), st3),
                    _full(gh.shape), _full(skip.shape)],
        out_specs=[pl.BlockSpec((tb, D), own_map), pl.BlockSpec((1, NH_B, DH_B, DH_B), st4),
                   pl.BlockSpec((1, NH_B, DH_B), st3)],
        out_shape=[jax.ShapeDtypeStruct((bsz * seq, D), BF16), jax.ShapeDtypeStruct(c0.shape, F32),
                   jax.ShapeDtypeStruct(n0.shape, F32)],
        compiler_params=_params(("parallel", "arbitrary")),
        name="mlstm",
    )(qkv, w, col, xc, z, c0, n0, gh, skip)


def _attn_kernel(masked, L, q_ref, kh_ref, ko_ref, vh_ref, vo_ref, bucket_ref, relb_ref, sink_ref,
                 o_ref, bias_s, s_scr, p_scr):
    nk = WINDOW + L
    ci = pl.program_id(1)
    n_sub = q_ref.shape[0] // L
    hl = N_HEADS * L
    kk = lax.broadcasted_iota(jnp.int32, (1, nk), 1)
    n_mask = min(n_sub, WINDOW // L) if masked else 0

    @pl.when((pl.program_id(0) == 0) & (ci == 0))
    def _():
        bucket = bucket_ref[...]
        for h in range(N_HEADS):
            acc = jnp.zeros((L, nk), F32)
            for b in range(NUM_BUCKETS):
                acc = jnp.where(bucket == b, relb_ref[b, h], acc)
            bias_s[h * L:(h + 1) * L, :] = acc
            for u in range(n_mask):
                bias_s[(1 + u) * hl + h * L:(1 + u) * hl + (h + 1) * L, :] = jnp.where(kk >= WINDOW - u * L, acc, NEG_INF)

    kall = jnp.concatenate([kh_ref[...], ko_ref[...]], axis=0)
    vall = jnp.concatenate([vh_ref[...], vo_ref[...]], axis=0)
    gw = GROUP * HEAD_DIM
    slot = lax.broadcasted_iota(jnp.int32, (L, gw), 1) // HEAD_DIM
    gl = GROUP * L
    for u in range(n_sub):
        kcat = kall[u * L:u * L + nk, :]
        for g in range(N_KV):
            gsl = slice(g * gw, (g + 1) * gw)
            qg = q_ref[u * L:(u + 1) * L, gsl]
            zero = jnp.zeros_like(qg)
            qs = jnp.concatenate([jnp.where(slot == j, qg, zero) for j in range(GROUP)], axis=0)
            s_scr[u * hl + g * gl:u * hl + (g + 1) * gl, :] = _dot_nt(qs, kcat[:, gsl])
    for u in range(n_sub):
        for h in range(N_HEADS):
            rows = slice(u * hl + h * L, u * hl + (h + 1) * L)
            if u < n_mask:
                first = jnp.where(ci == 0, (1 + u) * hl, 0)
                bias = bias_s[pl.ds(pl.multiple_of(first + h * L, L), L), :]
            else:
                bias = bias_s[h * L:(h + 1) * L, :]
            s = s_scr[rows, :] + bias
            sink = sink_ref[0, h]
            mx = jnp.maximum(jnp.max(s, axis=1, keepdims=True), sink)
            e = jnp.exp(s - mx)
            den = jnp.sum(e, axis=1, keepdims=True) + jnp.exp(sink - mx)
            p_scr[rows, :] = (e / den).astype(BF16)
    for u in range(n_sub):
        vcat = vall[u * L:u * L + nk, :]
        for g in range(N_KV):
            gsl = slice(g * gw, (g + 1) * gw)
            ow = _dot(p_scr[u * hl + g * gl:u * hl + (g + 1) * gl, :], vcat[:, gsl])
            og = jnp.where(slot == 0, ow[0:L, :], 0.0)
            for j in range(1, GROUP):
                og = jnp.where(slot == j, ow[j * L:(j + 1) * L, :], og)
            o_ref[u * L:(u + 1) * L, gsl] = og.astype(BF16)


def _attention(q, k_own, v_own, k_hist, v_hist, hist_map, row0, seq, bsz, masked, bucket, rel_bias, sinks):
    L = min(seq, CHUNK)
    lb = min(seq, ATTN_BLOCK)
    nc, in_map, out_map = _seq_specs(bsz, seq, lb, row0)
    n_sub = lb // L
    nk = WINDOW + L
    smem = functools.partial(pl.BlockSpec, memory_space=pltpu.SMEM)
    return pl.pallas_call(
        functools.partial(_attn_kernel, masked, L),
        grid=(bsz, nc),
        in_specs=[pl.BlockSpec((lb, D), in_map),
                  pl.BlockSpec((WINDOW, D), hist_map), pl.BlockSpec((lb, D), in_map),
                  pl.BlockSpec((WINDOW, D), hist_map), pl.BlockSpec((lb, D), in_map),
                  _full(bucket.shape), smem(), smem()],
        out_specs=pl.BlockSpec((lb, D), out_map),
        out_shape=jax.ShapeDtypeStruct((bsz * seq, D), BF16),
        scratch_shapes=[pltpu.VMEM(((1 + (min(n_sub, WINDOW // L) if masked else 0)) * N_HEADS * L, nk), F32),
                        pltpu.VMEM((n_sub * N_HEADS * L, nk), F32),
                        pltpu.VMEM((n_sub * N_HEADS * L, nk), BF16)],
        compiler_params=_params(("arbitrary", "arbitrary")),
        name="swa",
    )(q, k_hist, k_own, v_hist, v_own, bucket, rel_bias, sinks)


def _router_kernel(x_ref, g_ref, wr_ref, info_ref, cnt_ref):
    tm = x_ref.shape[0] // SUB

    @pl.when(pl.program_id(0) == 0)
    def _():
        cnt_ref[...] = jnp.zeros_like(cnt_ref)

    xn = _rms(_tile_rows_load(x_ref), g_ref[...])
    hi = xn.astype(BF16)
    lo = (xn - hi.astype(F32)).astype(BF16)
    w = wr_ref[...]
    whi = w.astype(BF16)
    wlo = (w - whi.astype(F32)).astype(BF16)
    logits = _dot(hi, whi) + _dot(hi, wlo) + _dot(lo, whi)
    lane = lax.broadcasted_iota(jnp.int32, (tm, LANES), 1)
    logits = jnp.where(lane < N_EXPERTS, logits, NEG_INF)
    m1 = jnp.max(logits, axis=1, keepdims=True)
    i1 = jnp.min(jnp.where(logits == m1, lane, LANES), axis=1, keepdims=True)
    rest = jnp.where(lane == i1, NEG_INF, logits)
    m2 = jnp.max(rest, axis=1, keepdims=True)
    i2 = jnp.min(jnp.where(rest == m2, lane, LANES), axis=1, keepdims=True)
    e2 = jnp.exp(m2 - m1)
    g1 = 1.0 / (1.0 + e2)
    g2 = e2 / (1.0 + e2)
    sel = (lane == i1) | (lane == i2)
    tri = (lax.broadcasted_iota(jnp.int32, (tm, tm), 0) > lax.broadcasted_iota(jnp.int32, (tm, tm), 1)).astype(BF16)
    rank = cnt_ref[...] + _dot(tri, sel.astype(BF16))
    r1 = jnp.sum(jnp.where(lane == i1, rank, 0.0), axis=1, keepdims=True)
    r2 = jnp.sum(jnp.where(lane == i2, rank, 0.0), axis=1, keepdims=True)
    cnt_ref[...] = cnt_ref[...] + jnp.sum(sel.astype(F32), axis=0, keepdims=True)
    info = jnp.where(lane == 0, i1.astype(F32), 0.0)
    info = jnp.where(lane == 1, i2.astype(F32), info)
    info = jnp.where(lane == 2, g1, info)
    info = jnp.where(lane == 3, g2, info)
    info = jnp.where(lane == 4, r1, info)
    info = jnp.where(lane == 5, r2, info)
    info_ref[...] = info


def _router(x, g, wr, tm):
    n = x.shape[0] // SUB
    return pl.pallas_call(
        _router_kernel,
        grid=(n // tm,),
        in_specs=[_tile_spec(tm), _full((1, D)), _full(wr.shape)],
        out_specs=[_rows(tm, LANES), _full((1, LANES))],
        out_shape=[jax.ShapeDtypeStruct((n, LANES), F32), jax.ShapeDtypeStruct((1, LANES), F32)],
        compiler_params=_params(("arbitrary",)),
        name="moe_router",
    )(x, g, wr)


def _token_tile(ref, t):
    return ref.at[pl.ds(pl.multiple_of(t * SUB, SUB), SUB)]


def _moe_scatter_kernel(te, zpos_ref, dest_ref, x_ref, w1_ref, w3_ref, w2_ref, xs_hbm, w1b_ref, w3b_ref, w2b_ref,
                        zero_s, sem):
    tm = x_ref.shape[0] // SUB
    span = te * SUB

    def rows_of(tile_start):
        return xs_hbm.at[pl.ds(pl.multiple_of(tile_start * SUB, span), span)]

    @pl.when(pl.program_id(0) == 0)
    def _():
        zero_s[...] = jnp.zeros_like(zero_s)
        for e in range(N_EXPERTS):
            pltpu.make_async_copy(zero_s, rows_of(zpos_ref[e]), sem).start()
        for e in range(N_EXPERTS):
            pltpu.make_async_copy(zero_s, rows_of(zpos_ref[e]), sem).wait()

        def clear_tail(t, carry):
            tail = pltpu.make_async_copy(zero_s, rows_of(t * te), sem)
            tail.start()
            tail.wait()
            return carry

        lax.fori_loop(zpos_ref[N_EXPERTS] // te, xs_hbm.shape[0] // span, clear_tail, 0)

    def copy(r, s):
        return pltpu.make_async_copy(_token_tile(x_ref, r), _token_tile(xs_hbm, dest_ref[s * tm + r]), sem)

    def start(r, carry):
        for s in range(2):
            copy(r, s).start(priority=s)
        return carry

    def wait(r, carry):
        for s in range(2):
            copy(r, s).wait()
        return carry

    lax.fori_loop(0, tm, start, 0)
    for src, dst in ((w1_ref, w1b_ref), (w3_ref, w3b_ref), (w2_ref, w2b_ref)):
        dst[...] = src[...].astype(BF16)
    lax.fori_loop(0, tm, wait, 0)


def _moe_scatter(x, dest, zpos, n_rows, tm, te, weights):
    n = x.shape[0] // SUB
    n_steps = n // tm
    n_slabs = 1 << (n_steps.bit_length() - 1)
    flat = [w.reshape(-1, w.shape[-1]) for w in weights]
    slab = lambda w: pl.BlockSpec((w.shape[0] // n_slabs, w.shape[1]), lambda i, zp: (jnp.minimum(i, n_slabs - 1), 0))
    out = pl.pallas_call(
        functools.partial(_moe_scatter_kernel, te),
        grid_spec=pltpu.PrefetchScalarGridSpec(
            num_scalar_prefetch=1,
            grid=(n_steps,),
            in_specs=[pl.BlockSpec((2 * tm,), lambda i, zp: (i,), memory_space=pltpu.SMEM),
                      pl.BlockSpec((tm * SUB, LANES), lambda i, zp: (i, 0))] + [slab(w) for w in flat],
            out_specs=[pl.BlockSpec(memory_space=pl.ANY)] + [slab(w) for w in flat],
            scratch_shapes=[pltpu.VMEM((te * SUB, LANES), F32), pltpu.SemaphoreType.DMA(())],
        ),
        out_shape=[jax.ShapeDtypeStruct((n_rows * SUB, LANES), F32)]
                  + [jax.ShapeDtypeStruct(w.shape, BF16) for w in flat],
        compiler_params=_params(("arbitrary",)),
        name="moe_scatter",
    )(zpos, dest, x, *flat)
    return out[0], [wb.reshape(w.shape) for wb, w in zip(out[1:], weights)]


def _moe_ffn_kernel(n_ff, te, pe_ref, na_ref, xs_ref, g_ref, w1_ref, w3_ref, w2_ref, ys_ref, xn_s, acc_s):
    i, j = pl.program_id(0), pl.program_id(1)
    span = te * SUB
    for s in range(2):
        active = s < na_ref[i]
        rows = slice(s * te, (s + 1) * te)

        @pl.when(active & (j == 0))
        def _():
            xn_s[rows, :] = _rms(_tile_rows_load(xs_ref.at[pl.ds(s * span, span)]), g_ref[...]).astype(BF16)
            acc_s[rows, :] = jnp.zeros((te, D), F32)

        @pl.when(active)
        def _():
            xn = xn_s[rows, :]
            a = (_silu(_dot(xn, w1_ref[0])) * _dot(xn, w3_ref[0])).astype(BF16)
            acc_s[rows, :] += _dot(a, w2_ref[0])

        @pl.when(j == n_ff - 1)
        def _():
            _tile_rows_store(ys_ref.at[pl.ds(s * span, span)], jnp.where(active, acc_s[rows, :], 0.0))


def _moe_ffn(xs, g, w1, w3, w2, pair_expert, pair_active, te):
    n_pairs = pair_expert.shape[0]
    d_ff = w1.shape[2]
    n_ff = FF_SPLIT_MOE
    tf = d_ff // n_ff

    def ff(i, j, na_r):
        return jnp.where(na_r[i] > 0, j, n_ff - 1)

    rows = pl.BlockSpec((2 * te * SUB, LANES), lambda i, j, pe_r, na_r: (i, 0))
    return pl.pallas_call(
        functools.partial(_moe_ffn_kernel, n_ff, te),
        grid_spec=pltpu.PrefetchScalarGridSpec(
            num_scalar_prefetch=2,
            grid=(n_pairs, n_ff),
            in_specs=[rows,
                      pl.BlockSpec((1, D), lambda i, j, pe_r, na_r: (0, 0)),
                      pl.BlockSpec((1, D, tf), lambda i, j, pe_r, na_r: (pe_r[i], 0, ff(i, j, na_r))),
                      pl.BlockSpec((1, D, tf), lambda i, j, pe_r, na_r: (pe_r[i], 0, ff(i, j, na_r))),
                      pl.BlockSpec((1, tf, D), lambda i, j, pe_r, na_r: (pe_r[i], ff(i, j, na_r), 0))],
            out_specs=rows,
            scratch_shapes=[pltpu.VMEM((2 * te, D), BF16), pltpu.VMEM((2 * te, D), F32)],
        ),
        out_shape=jax.ShapeDtypeStruct((n_pairs * 2 * te * SUB, LANES), F32),
        compiler_params=_params(("arbitrary", "arbitrary")),
        name="moe_ffn",
    )(pair_expert, pair_active, xs, g, w1, w3, w2)


def _moe_combine_kernel(n_p, dest_ref, info_ref, x_ref, ys_hbm, op_ref, os_ref, buf, sem):
    tm = x_ref.shape[0] // SUB
    i = pl.program_id(0)

    def copy(r, s):
        return pltpu.make_async_copy(_token_tile(ys_hbm, dest_ref[s * tm + r]), _token_tile(buf.at[s], r), sem)

    def start(r, carry):
        for s in range(2):
            copy(r, s).start(priority=s)
        return carry

    def wait(r, carry):
        for s in range(2):
            copy(r, s).wait()
        return carry

    lax.fori_loop(0, tm, start, 0)
    lax.fori_loop(0, tm, wait, 0)
    out = (_tile_rows_load(x_ref) + info_ref[:, 2:3] * _tile_rows_load(buf.at[0])
           + info_ref[:, 3:4] * _tile_rows_load(buf.at[1]))

    @pl.when(i < n_p)
    def _():
        op_ref[...] = out

    @pl.when(i >= n_p)
    def _():
        os_ref[...] = out


def _moe_combine(x, info, dest, ys, tm, n_p, n_s):
    out_p, out_s = _pair(tm, D, n_p)
    return pl.pallas_call(
        functools.partial(_moe_combine_kernel, n_p),
        grid=(n_p + n_s,),
        in_specs=[pl.BlockSpec((2 * tm,), lambda i: (i,), memory_space=pltpu.SMEM),
                  _rows(tm, LANES), _tile_spec(tm), pl.BlockSpec(memory_space=pl.ANY)],
        out_specs=[out_p, out_s],
        out_shape=[jax.ShapeDtypeStruct((n_p * tm, D), F32), jax.ShapeDtypeStruct((n_s * tm, D), F32)],
        scratch_shapes=[pltpu.VMEM((2, tm * SUB, LANES), F32), pltpu.SemaphoreType.DMA(())],
        compiler_params=_params(("arbitrary",)),
        name="moe_combine",
    )(dest, info, x, ys)


def _moe(x, g, w_router, w1, w3, w2, tm, n_p, n_s):
    n = x.shape[0] // SUB
    te = min(EXPERT_TILE, tm)
    wr = jnp.zeros((D, LANES), F32).at[:, :N_EXPERTS].set(w_router)
    info, cnt = _router(x, g, wr, tm)
    e1, e2 = info[:, 0].astype(jnp.int32), info[:, 1].astype(jnp.int32)
    r1, r2 = info[:, 4].astype(jnp.int32), info[:, 5].astype(jnp.int32)
    counts = cnt[0, :N_EXPERTS].astype(jnp.int32)
    region = 2 * te
    padded = (counts + region - 1) // region * region
    ends = jnp.cumsum(padded)
    starts = ends - padded
    dest = jnp.stack([(starts[e1] + r1).reshape(n // tm, tm), (starts[e2] + r2).reshape(n // tm, tm)], axis=1)
    dest = dest.reshape(-1)
    n_pairs = (2 * n + N_EXPERTS * (region - 1)) // region
    pair_start = jnp.arange(n_pairs, dtype=jnp.int32) * region
    pair_expert = jnp.minimum(jnp.sum(pair_start[:, None] >= ends[None, :], axis=1), N_EXPERTS - 1).astype(jnp.int32)
    used_end = starts + (counts + te - 1) // te * te
    pair_active = jnp.clip((used_end[pair_expert] - pair_start) // te, 0, 2).astype(jnp.int32)
    last_region = jnp.maximum(ends - region, 0).astype(jnp.int32)
    xs, (w1b, w3b, w2b) = _moe_scatter(x, dest, jnp.concatenate([last_region, ends[-1:].astype(jnp.int32)]),
                                       n_pairs * region, tm, region, (w1, w3, w2))
    ys = _moe_ffn(xs, g, w1b, w3b, w2b, pair_expert, pair_active, te)
    return _moe_combine(x, info, dest, ys, tm, n_p, n_s)


def _t5_bucket(rel):
    n = -rel
    half = NUM_BUCKETS // 2
    ret = jnp.where(n < 0, half, 0)
    n = jnp.abs(n)
    max_exact = half // 2
    nf = jnp.maximum(n, 1).astype(F32)
    large = max_exact + (jnp.log(nf / max_exact) / math.log(MAX_DISTANCE / max_exact)
                         * (half - max_exact)).astype(jnp.int32)
    large = jnp.minimum(large, half - 1)
    return ret + jnp.where(n < max_exact, n, large)


def _bucket_table(length):
    kpos = jnp.arange(WINDOW + length) - WINDOW
    return _t5_bucket(kpos[None, :] - jnp.arange(length)[:, None]).astype(jnp.int32)


def kernel(x_prompt, x_sample, state_conv_a, state_rglru_h, state_conv_b, state_mlstm_c, state_mlstm_n, state_mlstm_m, cache_swa_k, cache_swa_v, norm_mix, norm_ffn, w_in_ab, w_conv_a, b_conv_a, w_rg_a, b_rg_a, w_rg_x, b_rg_x, rg_lambda, w_conv_b, b_conv_b, w_q_b, w_k_b, w_v_b, w_if_b, b_if_b, g_hnorm_b, skip_b, w_out_ab, w1_dense, w3_dense, w2_dense, w_in_att, g_qnorm, g_knorm, sinks, w_out_att, rel_bias, w_router, w1_moe, w3_moe, w2_moe):
    bp, tp, _ = x_prompt.shape
    bs, ts, _ = x_sample.shape
    assert norm_mix.shape[0] == 2 and w_in_ab.shape[0] == 1 and w_in_att.shape[0] == 1
    assert tp % CHUNK == 0 and ts <= CHUNK and cache_swa_k.shape[2] == WINDOW
    rows_p, rows_s = bp * tp, bs * ts
    tm = math.gcd(math.gcd(rows_p, rows_s), TOKEN_TILE)
    n_p, n_s = rows_p // tm, rows_s // tm
    xp = x_prompt.reshape(rows_p, D)
    xs = x_sample.reshape(rows_s, D)
    bf = lambda w: w.astype(BF16)
    vec = lambda v: v.reshape(1, -1)

    xa, ga, xm, z = _in_proj_ab(xp, xs, vec(norm_mix[0]), bf(w_in_ab[0]), tm)
    rg_w = (w_conv_a[0], vec(b_conv_a[0]), bf(w_rg_a[0]), vec(b_rg_a[0]), bf(w_rg_x[0]), vec(b_rg_x[0]),
            vec(rg_lambda[0]))
    ya_p, hl_p = _rglru(xa, ga, 0, tp, jnp.zeros((bp, CONV_W - 1, D), F32), jnp.zeros((bp, 1, D), F32), *rg_w)
    ya_s, hl_s = _rglru(xa, ga, rows_p, ts, state_conv_a[0], state_rglru_h[0].reshape(bs, 1, D), *rg_w)

    def gate_cols(w):
        pad = ((0, 0), (0, LANES - NH_B))
        return jnp.concatenate([jnp.pad(w[:, :NH_B], pad), jnp.pad(w[:, NH_B:], pad)], axis=1)

    pre_w = (w_conv_b[0], vec(b_conv_b[0]), bf(w_q_b[0]), bf(w_k_b[0]), bf(w_v_b[0]), bf(w_if_b[0].T),
             b_if_b[0].reshape(2 * NH_B, 1), bf(gate_cols(w_if_b[0])), gate_cols(b_if_b[0][None]))
    rec_w = (g_hnorm_b[0].reshape(1, D), vec(skip_b[0]))
    lanes = lambda m: jnp.pad(m, ((0, 0), (0, LANES - NH_B)))[:, None, :]
    pre_p = _mlstm_pre(xm, 0, tp, jnp.zeros((bp, CONV_W - 1, D), F32), jnp.zeros((bp, 1, LANES), F32), *pre_w)
    pre_s = _mlstm_pre(xm, rows_p, ts, state_conv_b[0], lanes(state_mlstm_m[0]), *pre_w)
    m_p, m_s = pre_p[4], pre_s[4]
    yb_p, c_p, nn_p = _mlstm(*pre_p[1:4], pre_p[0], z, 0, tp, jnp.zeros((bp, NH_B, DH_B, DH_B), F32),
                             jnp.zeros((bp, NH_B, DH_B), F32), *rec_w)
    yb_s, c_s, nn_s = _mlstm(*pre_s[1:4], pre_s[0], z, rows_p, ts, state_mlstm_c[0], state_mlstm_n[0], *rec_w)

    w_out = bf(w_out_ab[0])
    y = _out_proj((xp, xs), [(ya_p, ya_s), (yb_p, yb_s)], [w_out[:D], w_out[D:]], tm, n_p)
    y = _ffn_dense(y, vec(norm_ffn[0]), bf(w1_dense[0]), bf(w3_dense[0]), bf(w2_dense[0]), tm)

    tile2 = lambda gain: jnp.tile(gain, D_KV // HEAD_DIM).reshape(1, D_KV)
    q, kw, vw, k, v = _in_proj_att(y, vec(norm_mix[1]), bf(w_in_att[0]), tile2(g_qnorm[0]), tile2(g_knorm[0]), tm)
    sinks2 = sinks[0].reshape(1, N_HEADS)
    wpb = min(tp, ATTN_BLOCK) // WINDOW
    assert wpb >= 1 and tp % (wpb * WINDOW) == 0
    hist_p = lambda bi, ci: (bi * (tp // WINDOW) + jnp.maximum(ci * wpb - 1, 0), 0)
    o_p = _attention(q, kw, vw, kw, vw, hist_p, 0, tp, bp, True, _bucket_table(CHUNK), rel_bias, sinks2)

    def widen(cache):
        wide = jnp.broadcast_to(cache[:, :, :, None, :], (bs, WINDOW, N_KV, GROUP, HEAD_DIM))
        return wide.reshape(bs * WINDOW, D).astype(BF16)

    ck, cv = widen(cache_swa_k[0]), widen(cache_swa_v[0])
    hist_s = lambda bi, ci: (bi, 0)
    o_s = _attention(q, kw, vw, ck, cv, hist_s, rows_p, ts, bs, False, _bucket_table(ts), rel_bias, sinks2)
    y = _out_proj(y, [(o_p, o_s)], [bf(w_out_att[0])], tm, n_p, tiled_out=True)
    yp, ys = _moe(y, vec(norm_ffn[1]), w_router[0], w1_moe[0], w3_moe[0], w2_moe[0], tm, n_p, n_s)

    def tail(a, rows, b, t, keep):
        if rows == 0:
            return jnp.stack([a[(i + 1) * t - keep:(i + 1) * t] for i in range(b)])
        return a[rows:rows + b * t].reshape(b, t, -1)[:, t - keep:]

    kv4 = lambda a, b: a.reshape(b, -1, N_KV, HEAD_DIM)
    one = lambda a: a[None]
    k_s = jnp.concatenate([cache_swa_k[0][:, ts:], kv4(k[rows_p:], bs)], axis=1)
    v_s = jnp.concatenate([cache_swa_v[0][:, ts:], kv4(v[rows_p:], bs)], axis=1)
    return (yp.reshape(bp, tp, D), ys.reshape(bs, ts, D),
            one(tail(xa, 0, bp, tp, CONV_W - 1)), one(hl_p.reshape(bp, D)), one(tail(xm, 0, bp, tp, CONV_W - 1)),
            one(c_p), one(nn_p), one(m_p[:, 0, :NH_B]),
            one(kv4(tail(k, 0, bp, tp, WINDOW), bp)), one(kv4(tail(v, 0, bp, tp, WINDOW), bp)),
            one(tail(xa, rows_p, bs, ts, CONV_W - 1)), one(hl_s.reshape(bs, D)),
            one(tail(xm, rows_p, bs, ts, CONV_W - 1)), one(c_s), one(nn_s), one(m_s[:, 0, :NH_B]),
            one(k_s), one(v_s))
```

```python
import functools
import math

import jax
import jax.numpy as jnp
from jax import lax
from jax.experimental import pallas as pl
from jax.experimental.pallas import tpu as pltpu

F32 = jnp.float32
BF16 = jnp.bfloat16

D = 1024
CHUNK = 64
CONV_W = 4
NB_A = 8
BS_A = D // NB_A
RG_C = 8.0
NH_B = 4
DH_B = D // NH_B
N_HEADS = 16
HEAD_DIM = D // N_HEADS
N_KV = 4
GROUP = N_HEADS // N_KV
D_KV = N_KV * HEAD_DIM
WINDOW = 128
NUM_BUCKETS = 32
MAX_DISTANCE = 128
N_EXPERTS = 8
EPS = 1e-6
LANES = 128
SUB = D // LANES
NEG_INF = float("-inf")
MIN_NORMAL = 1.1754944e-38

TOKEN_TILE = 512
EXPERT_TILE = 512
FF_SPLIT_DENSE = 3
FF_SPLIT_MOE = 2
SEQ_BLOCK = 256
RGLRU_BLOCK = 512
SCAN_ROWS = 8
ATTN_BLOCK = 256
VMEM_LIMIT = 56 * 1024 * 1024


def _params(sem):
    return pltpu.CompilerParams(dimension_semantics=sem, vmem_limit_bytes=VMEM_LIMIT)


def _full(shape):
    return pl.BlockSpec(shape, lambda *_: (0,) * len(shape))


def _resident(shape):
    return pl.BlockSpec(shape, lambda *_: (0,) * len(shape), pipeline_mode=pl.Buffered(1))


def _rows(tm, c):
    return pl.BlockSpec((tm, c), lambda i: (i, 0))


def _pair(tm, c, n_p):
    return [pl.BlockSpec((tm, c), lambda i: (jnp.minimum(i, n_p - 1), 0)),
            pl.BlockSpec((tm, c), lambda i: (jnp.maximum(i - n_p, 0), 0))]


def _rms(x, g):
    ms = jnp.mean(x * x, axis=-1, keepdims=True)
    return x * lax.rsqrt(ms + EPS) * g


def _silu(x):
    return x * jax.nn.sigmoid(x)


def _sigmoid_tanh(x):
    return 0.5 * jnp.tanh(0.5 * x) + 0.5


def _softplus(x):
    return jnp.maximum(x, 0.0) + jnp.log1p(jnp.exp(-jnp.abs(x)))


def _gelu_tanh(x):
    c = math.sqrt(2.0 / math.pi)
    return x * (0.5 * (1.0 + jnp.tanh(c * (x + 0.044715 * (x * x * x)))))


def _dot(a, b):
    return jnp.dot(a, b, preferred_element_type=F32)


def _dot_nt(a, b):
    return lax.dot_general(a, b, (((1,), (1,)), ((), ())), preferred_element_type=F32)


def _dot_tn(a, b):
    return lax.dot_general(a, b, (((0,), (0,)), ((), ())), preferred_element_type=F32)


def _in_proj_ab_kernel(n_p, n_out, n_later, xp_ref, xs_ref, g_ref, w_ref, *refs):
    later_refs, o_refs, later_out = refs[:n_later], refs[n_later:n_later + n_out], refs[n_later + n_out:]
    i = pl.program_id(0)
    x = jnp.where(i < n_p, xp_ref[...], xs_ref[...])
    xn = _rms(x, g_ref[...]).astype(BF16)
    for c, o_ref in enumerate(o_refs):
        o_ref[...] = _dot(xn, w_ref[:, c * D:(c + 1) * D])
    for src, dst in zip(later_refs, later_out):
        dst[...] = src[...].astype(BF16)


def _in_proj_ab(xp, xs, g, w, tm, later):
    n_p, n_s = xp.shape[0] // tm, xs.shape[0] // tm
    n = xp.shape[0] + xs.shape[0]
    n_out = w.shape[1] // D
    n_steps = n_p + n_s
    n_slabs = 1 << (n_steps.bit_length() - 1)
    slab = lambda a: pl.BlockSpec((a.shape[0] // n_slabs, a.shape[1]), lambda i: (jnp.minimum(i, n_slabs - 1), 0))
    return pl.pallas_call(
        functools.partial(_in_proj_ab_kernel, n_p, n_out, len(later)),
        grid=(n_steps,),
        in_specs=_pair(tm, D, n_p) + [_full((1, D)), _resident(w.shape)] + [slab(a) for a in later],
        out_specs=[_rows(tm, D)] * n_out + [slab(a) for a in later],
        out_shape=[jax.ShapeDtypeStruct((n, D), F32)] * n_out + [jax.ShapeDtypeStruct(a.shape, BF16) for a in later],
        compiler_params=_params(("arbitrary",)),
        name="in_proj_ab",
    )(xp, xs, g, w, *later)


def _tile_rows_load(ref):
    tm = ref.shape[0] // SUB
    return jnp.concatenate([ref[pl.ds(s, tm, stride=SUB), :] for s in range(SUB)], axis=1)


def _tile_rows_store(ref, val):
    tm = val.shape[0]
    for s in range(SUB):
        ref[pl.ds(s, tm, stride=SUB), :] = val[:, s * LANES:(s + 1) * LANES]


def _tile_spec(tm):
    return pl.BlockSpec((tm * SUB, LANES), lambda i: (i, 0))


def _out_proj_kernel(n_p, n_res, n_a, tiled_out, *refs):
    i = pl.program_id(0)
    res_refs = refs[:n_res]
    a_refs = refs[n_res:n_res + 2 * n_a]
    w_refs = refs[n_res + 2 * n_a:n_res + 3 * n_a]
    o_ref = refs[-1]
    if n_res == 2:
        acc = jnp.where(i < n_p, res_refs[0][...], res_refs[1][...])
    else:
        acc = res_refs[0][...]
    for k in range(n_a):
        a = jnp.where(i < n_p, a_refs[2 * k][...], a_refs[2 * k + 1][...])
        acc = acc + _dot(a, w_refs[k][...])
    if tiled_out:
        _tile_rows_store(o_ref, acc)
    else:
        o_ref[...] = acc


def _out_proj(res, a_pairs, ws, tm, n_p, tiled_out=False):
    res = res if isinstance(res, tuple) else (res,)
    n = sum(a.shape[0] for a in a_pairs[0])
    res_specs = _pair(tm, D, n_p) if len(res) == 2 else [_rows(tm, D)]
    a_specs = []
    for a in a_pairs:
        a_specs += _pair(tm, a[0].shape[1], n_p)
    return pl.pallas_call(
        functools.partial(_out_proj_kernel, n_p, len(res), len(a_pairs), tiled_out),
        grid=(n // tm,),
        in_specs=res_specs + a_specs + [_resident(w.shape) for w in ws],
        out_specs=_tile_spec(tm) if tiled_out else _rows(tm, D),
        out_shape=jax.ShapeDtypeStruct((n * SUB, LANES) if tiled_out else (n, D), F32),
        compiler_params=_params(("parallel",)),
        name="out_proj",
    )(*res, *[x for a in a_pairs for x in a], *ws)


def _ffn_dense_kernel(x_ref, g_ref, w1_ref, w3_ref, w2_ref, o_ref):
    x = x_ref[...]
    xn = _rms(x, g_ref[...]).astype(BF16)
    step = w1_ref.shape[1] // FF_SPLIT_DENSE
    acc = x
    for c in range(FF_SPLIT_DENSE):
        sl = slice(c * step, (c + 1) * step)
        a = (_silu(_dot(xn, w1_ref[:, sl])) * _dot(xn, w3_ref[:, sl])).astype(BF16)
        acc = acc + _dot(a, w2_ref[sl, :])
    o_ref[...] = acc


def _ffn_dense(x, g, w1, w3, w2, tm):
    n = x.shape[0]
    return pl.pallas_call(
        _ffn_dense_kernel,
        grid=(n // tm,),
        in_specs=[_rows(tm, D), _full((1, D)), _resident(w1.shape), _resident(w3.shape), _resident(w2.shape)],
        out_specs=_rows(tm, D),
        out_shape=jax.ShapeDtypeStruct((n, D), F32),
        compiler_params=_params(("parallel",)),
        name="ffn_dense",
    )(x, g, w1, w3, w2)


def _in_proj_att_kernel(x_ref, g_ref, w_ref, gq_ref, gk_ref, q_ref, kw_ref, vw_ref, k_ref, v_ref):
    xn = _rms(x_ref[...], g_ref[...]).astype(BF16)
    cw = D_KV
    r = lax.broadcasted_iota(jnp.int32, (cw, cw), 0) // HEAD_DIM
    c = lax.broadcasted_iota(jnp.int32, (cw, cw), 1) // HEAD_DIM
    group_ones = (r == c).astype(BF16)
    er = lax.broadcasted_iota(jnp.int32, (D_KV, D), 0)
    ec = lax.broadcasted_iota(jnp.int32, (D_KV, D), 1)
    widen = ((er // HEAD_DIM == ec // (GROUP * HEAD_DIM)) & (er % HEAD_DIM == ec % HEAD_DIM)).astype(BF16)

    def head_norm(y, gain):
        ms = _dot((y * y).astype(BF16), group_ones) * (1.0 / HEAD_DIM)
        return y * lax.rsqrt(ms + EPS) * gain

    for b in range(D // cw):
        y = _dot(xn, w_ref[:, b * cw:(b + 1) * cw])
        q_ref[:, b * cw:(b + 1) * cw] = (head_norm(y, gq_ref[...]) * (HEAD_DIM ** -0.5)).astype(BF16)
    k = head_norm(_dot(xn, w_ref[:, D:D + D_KV]), gk_ref[...])
    v = _dot(xn, w_ref[:, D + D_KV:D + 2 * D_KV])
    k_ref[...] = k
    v_ref[...] = v
    kw_ref[...] = _dot(k.astype(BF16), widen).astype(BF16)
    vw_ref[...] = _dot(v.astype(BF16), widen).astype(BF16)


def _in_proj_att(x, g, w, gq, gk, tm):
    n = x.shape[0]
    return pl.pallas_call(
        _in_proj_att_kernel,
        grid=(n // tm,),
        in_specs=[_rows(tm, D), _full((1, D)), _resident(w.shape), _full((1, D_KV)), _full((1, D_KV))],
        out_specs=[_rows(tm, D)] * 3 + [_rows(tm, D_KV)] * 2,
        out_shape=[jax.ShapeDtypeStruct((n, D), BF16)] * 3 + [jax.ShapeDtypeStruct((n, D_KV), F32)] * 2,
        compiler_params=_params(("parallel",)),
        name="in_proj_att",
    )(x, g, w, gq, gk)


def _causal_conv(x, xbuf, conv0_ref, wc_ref, bc_ref, first):
    tb = x.shape[0]

    @pl.when(first)
    def _():
        xbuf[5:8, :] = conv0_ref[0]

    xbuf[8:8 + tb, :] = x
    y = bc_ref[...] + xbuf[5:5 + tb, :] * wc_ref[0:1, :]
    for j in range(1, CONV_W):
        y = y + xbuf[5 + j:5 + j + tb, :] * wc_ref[j:j + 1, :]
    xbuf[5:8, :] = xbuf[5 + tb:8 + tb, :]
    return y


def _seq_specs(bsz, seq, tb, row0):
    nc = seq // tb
    off = row0 // tb
    return nc, (lambda bi, ci: (off + bi * nc + ci, 0)), (lambda bi, ci: (bi * nc + ci, 0))


def _scan_by_doubling(a, u, pos, axis, length):
    d = 1
    while d < length:
        keep = pos >= d
        u = jnp.where(keep, a * pltpu.roll(u, d, axis) + u, u)
        a = jnp.where(keep, a * pltpu.roll(a, d, axis), a)
        d *= 2
    return a, u


def _linear_scan(a, u, h0, row):
    tb = a.shape[0]
    sub = min(tb, SCAN_ROWS)
    out = []
    for k in range(tb // sub):
        rows = slice(k * sub, (k + 1) * sub)
        a_k, u_k = _scan_by_doubling(a[rows, :], u[rows, :], row[0:sub, :], 0, sub)
        h_k = a_k * h0 + u_k
        h0 = h_k[sub - 1:sub, :]
        out.append(h_k)
    return jnp.concatenate(out, axis=0)


def _rglru_kernel(xa_ref, ga_ref, conv0_ref, h0_ref, wc_ref, bc_ref, wa_ref, ba_ref, wx_ref, bx_ref, lam_ref,
                  ya_ref, hl_ref, xbuf):
    first = pl.program_id(1) == 0
    tb = xa_ref.shape[0]

    @pl.when(first)
    def _():
        hl_ref[0] = h0_ref[0]

    y = _causal_conv(xa_ref[...], xbuf, conv0_ref, wc_ref, bc_ref, first)
    row = lax.broadcasted_iota(jnp.int32, (tb, BS_A), 0)
    for n in range(NB_A):
        sl = slice(n * BS_A, (n + 1) * BS_A)
        yn = y[:, sl]
        yb = yn.astype(BF16)
        r = _sigmoid_tanh(_dot(yb, wa_ref[n]) + ba_ref[:, sl])
        ig = _sigmoid_tanh(_dot(yb, wx_ref[n]) + bx_ref[:, sl])
        log_a = (-RG_C) * r * _softplus(-lam_ref[:, sl])
        a = jnp.exp(log_a)
        th = jnp.tanh(log_a)
        gain2 = -2.0 * th / (1.0 - th)
        u = yn * ig * (gain2 * lax.rsqrt(jnp.maximum(gain2, MIN_NORMAL)))
        h = _linear_scan(a, u, hl_ref[0, :, sl], row)
        hl_ref[0, :, sl] = h[tb - 1:tb, :]
        ya_ref[:, sl] = (h * _gelu_tanh(ga_ref[:, sl])).astype(BF16)


def _rglru(xa, ga, row0, seq, conv0, h0, wc, bc, wa, ba, wx, bx, lam):
    bsz = conv0.shape[0]
    tb = math.gcd(seq, RGLRU_BLOCK)
    nc, in_map, out_map = _seq_specs(bsz, seq, tb, row0)
    state3 = lambda bi, ci: (bi, 0, 0)
    return pl.pallas_call(
        _rglru_kernel,
        grid=(bsz, nc),
        in_specs=[pl.BlockSpec((tb, D), in_map), pl.BlockSpec((tb, D), in_map),
                  pl.BlockSpec((1, CONV_W - 1, D), state3), pl.BlockSpec((1, 1, D), state3),
                  _full(wc.shape), _full(bc.shape), _full(wa.shape), _full(ba.shape), _full(wx.shape),
                  _full(bx.shape), _full(lam.shape)],
        out_specs=[pl.BlockSpec((tb, D), out_map), pl.BlockSpec((1, 1, D), state3)],
        out_shape=[jax.ShapeDtypeStruct((bsz * seq, D), BF16), jax.ShapeDtypeStruct((bsz, 1, D), F32)],
        scratch_shapes=[pltpu.VMEM((tb + 8, D), F32)],
        compiler_params=_params(("parallel", "arbitrary")),
        name="rglru",
    )(xa, ga, conv0, h0, wc, bc, wa, ba, wx, bx, lam)


def _mlstm_pre_kernel(chunk, xm_ref, conv0_ref, m0_ref, wc_ref, bc_ref, wq_ref, wk_ref, wv_ref, wif_ref, bif_ref,
                      wifc_ref, bifc_ref, xc_ref, qkv_ref, w_ref, col_ref, m_ref, xbuf):
    first = pl.program_id(1) == 0
    tb = xm_ref.shape[0]
    L = chunk

    @pl.when(first)
    def _():
        m_ref[...] = m0_ref[...]

    x = xm_ref[...]
    xc = _silu(_causal_conv(x, xbuf, conv0_ref, wc_ref, bc_ref, first))
    xc_ref[...] = xc
    g_row = bif_ref[...]
    g_col = bifc_ref[...]
    for h in range(NH_B):
        sl = slice(h * DH_B, (h + 1) * DH_B)
        xch = xc[:, sl].astype(BF16)
        q = _dot(xch, wq_ref[h]).astype(BF16)
        k = _dot(xch, wk_ref[h]).astype(BF16)
        v = _dot(x[:, sl].astype(BF16), wv_ref[h]).astype(BF16)
        for part, val in enumerate((q, k, v)):
            psl = slice(part * D + h * DH_B, part * D + (h + 1) * DH_B)
            qkv_ref[:, psl] = val
            g_row = g_row + _dot_nt(wif_ref[:, psl], val)
            g_col = g_col + _dot(val, wifc_ref[psl, :])

    def exact_dot(a, b01):
        p1 = a.astype(BF16)
        r1 = a - p1.astype(F32)
        p2 = r1.astype(BF16)
        p3 = (r1 - p2.astype(F32)).astype(BF16)
        return _dot(p1, b01) + _dot(p2, b01) + _dot(p3, b01)

    def exact_dot_left(b01, a):
        p1 = a.astype(BF16)
        r1 = a - p1.astype(F32)
        p2 = r1.astype(BF16)
        p3 = (r1 - p2.astype(F32)).astype(BF16)
        return _dot(b01, p1) + _dot(b01, p2) + _dot(b01, p3)

    tt = lax.broadcasted_iota(jnp.int32, (L, L), 0)
    ss = lax.broadcasted_iota(jnp.int32, (L, L), 1)
    upper = (tt <= ss).astype(BF16)
    lower = (ss <= tt).astype(BF16)
    i_row = g_row[0:NH_B, :]
    b_row = exact_dot(-_softplus(-g_row), upper)[NH_B:2 * NH_B, :]
    w_ref[0] = i_row - b_row
    i_col = g_col[:, 0:LANES]
    b_col = exact_dot_left(lower, -_softplus(-g_col[:, LANES:2 * LANES]))
    w_col = i_col - b_col
    trow = lax.broadcasted_iota(jnp.int32, (L, LANES), 0)
    run_max = w_col
    d = 1
    while d < L:
        run_max = jnp.where(trow >= d, jnp.maximum(run_max, pltpu.roll(run_max, d, 0)), run_max)
        d *= 2
    m_prev = m_ref[0]
    big_m = jnp.maximum(m_prev, run_max)
    m_last = big_m[L - 1:L, :]
    m_ref[0] = b_col[L - 1:L, :] + m_last
    head_lane = lax.broadcasted_iota(jnp.int32, (L, LANES), 1) < NH_B
    table = jnp.where(head_lane, big_m, 0.0)
    for c, col in enumerate((jnp.exp(m_prev - big_m), jnp.exp(-(b_col + big_m)), jnp.exp(w_col - m_last))):
        table = table + pltpu.roll(jnp.where(head_lane, col, 0.0), (c + 1) * NH_B, 1)
    col_ref[...] = table


def _mlstm_pre(xm, row0, seq, conv0, m0, wc, bc, wq, wk, wv, wif_t, bif, wif_c, bif_c):
    bsz = conv0.shape[0]
    tb = min(seq, SEQ_BLOCK)
    chunk = tb
    nc, in_map, out_map = _seq_specs(bsz, seq, tb, row0)
    n = bsz * seq
    st3 = lambda bi, ci: (bi, 0, 0)
    return pl.pallas_call(
        functools.partial(_mlstm_pre_kernel, chunk),
        grid=(bsz, nc),
        in_specs=[pl.BlockSpec((tb, D), in_map), pl.BlockSpec((1, CONV_W - 1, D), st3),
                  pl.BlockSpec((1, 1, LANES), st3),
                  _full(wc.shape), _full(bc.shape), _full(wq.shape), _full(wk.shape), _full(wv.shape),
                  _full(wif_t.shape), _full(bif.shape), _full(wif_c.shape), _full(bif_c.shape)],
        out_specs=[pl.BlockSpec((tb, D), out_map), pl.BlockSpec((tb, 3 * D), out_map)]
                  + [pl.BlockSpec((tb // chunk, NH_B, chunk), lambda bi, ci: (bi * nc + ci, 0, 0)),
                     pl.BlockSpec((tb, LANES), out_map), pl.BlockSpec((1, 1, LANES), st3)],
        out_shape=[jax.ShapeDtypeStruct((n, D), F32), jax.ShapeDtypeStruct((n, 3 * D), BF16)]
                  + [jax.ShapeDtypeStruct((n // chunk, NH_B, chunk), F32), jax.ShapeDtypeStruct((n, LANES), F32),
                     jax.ShapeDtypeStruct(m0.shape, F32)],
        scratch_shapes=[pltpu.VMEM((tb + 8, D), F32)],
        compiler_params=_params(("parallel", "arbitrary")),
        name="mlstm_pre",
    )(xm, conv0, m0, wc, bc, wq, wk, wv, wif_t, bif, wif_c, bif_c)


def _mlstm_kernel(chunk, qkv_ref, w_ref, col_ref, xc_ref, z_ref, c0_ref, n0_ref, gh_ref, skip_ref,
                  yb_ref, c_ref, n_ref):
    tb = qkv_ref.shape[0]
    L = chunk
    scale = DH_B ** -0.5

    @pl.when(pl.program_id(1) == 0)
    def _():
        c_ref[...] = c0_ref[...]
        n_ref[...] = n0_ref[...]

    causal = lax.broadcasted_iota(jnp.int32, (L, L), 1) <= lax.broadcasted_iota(jnp.int32, (L, L), 0)
    heads = range(NH_B)
    hsl = [slice(h * DH_B, (h + 1) * DH_B) for h in heads]
    for j in range(tb // L):
        rows = slice(j * L, (j + 1) * L)
        cols = col_ref[rows, :]
        col = lambda c, h: cols[:, c * NH_B + h:c * NH_B + h + 1]
        part = lambda p, h: qkv_ref[rows, p * D + h * DH_B:p * D + (h + 1) * DH_B]
        q = [part(0, h) for h in heads]
        k = [part(1, h) for h in heads]
        v = [part(2, h) for h in heads]
        s = [_dot_nt(q[h], k[h]) for h in heads]
        qc = [_dot(q[h], c_ref[0, h].astype(BF16)) for h in heads]
        qn = [_dot_nt(q[h], jnp.broadcast_to(n_ref[0, h:h + 1, :], (8, DH_B)).astype(BF16))[:, 0:1] for h in heads]
        p = [(s[h] * scale * jnp.where(causal, jnp.exp(w_ref[j, h:h + 1, :] - col(0, h)), 0.0)).astype(BF16)
             for h in heads]
        pv = [_dot(p[h], v[h]) for h in heads]
        psum = [_dot(p[h], jnp.ones((L, LANES), BF16))[:, 0:1] for h in heads]
        for h in heads:
            sc = col(1, h)
            keep = sc[L - 1:L, :]
            kd = k[h].astype(F32) * (col(3, h) * scale)
            c_ref[0, h] = keep * c_ref[0, h] + _dot_tn(kd.astype(BF16), v[h])
            n_ref[0, h:h + 1, :] = keep * n_ref[0, h:h + 1, :] + jnp.sum(kd, axis=0, keepdims=True)
            num = pv[h] + sc * qc[h]
            den = psum[h] + sc * qn[h]
            hs = num / jnp.maximum(jnp.abs(den), col(2, h))
            mu = jnp.mean(hs, axis=1, keepdims=True)
            dev = hs - mu
            var = jnp.mean(dev * dev, axis=1, keepdims=True)
            hn = dev * lax.rsqrt(var + EPS) * gh_ref[:, hsl[h]]
            out = (hn + skip_ref[:, hsl[h]] * xc_ref[rows, hsl[h]]) * _silu(z_ref[rows, hsl[h]])
            yb_ref[rows, hsl[h]] = out.astype(BF16)


def _mlstm(qkv, w, col, xc, z, row0, seq, c0, n0, gh, skip):
    bsz = c0.shape[0]
    tb = min(seq, SEQ_BLOCK)
    chunk = tb
    nc, z_map, own_map = _seq_specs(bsz, seq, tb, row0)
    st4 = lambda bi, ci: (bi, 0, 0, 0)
    st3 = lambda bi, ci: (bi, 0, 0)
    return pl.pallas_call(
        functools.partial(_mlstm_kernel, chunk),
        grid=(bsz, nc),
        in_specs=[pl.BlockSpec((tb, 3 * D), own_map)]
                 + [pl.BlockSpec((tb // chunk, NH_B, chunk), lambda bi, ci: (bi * nc + ci, 0, 0)),
                    pl.BlockSpec((tb, LANES), own_map),
                    pl.BlockSpec((tb, D), own_map), pl.BlockSpec((tb, D), z_map),
                    pl.BlockSpec((1, NH_B, DH_B, DH_B), st4), pl.BlockSpec((1, NH_B, DH_B), st3),
                    _full(gh.shape), _full(skip.shape)],
        out_specs=[pl.BlockSpec((tb, D), own_map), pl.BlockSpec((1, NH_B, DH_B, DH_B), st4),
                   pl.BlockSpec((1, NH_B, DH_B), st3)],
        out_shape=[jax.ShapeDtypeStruct((bsz * seq, D), BF16), jax.ShapeDtypeStruct(c0.shape, F32),
                   jax.ShapeDtypeStruct(n0.shape, F32)],
        compiler_params=_params(("parallel", "arbitrary")),
        name="mlstm",
    )(qkv, w, col, xc, z, c0, n0, gh, skip)


def _attn_kernel(masked, L, q_ref, kh_ref, ko_ref, vh_ref, vo_ref, bucket_ref, relb_ref, sink_ref,
                 o_ref, bias_s, s_scr, p_scr):
    nk = WINDOW + L
    ci = pl.program_id(1)
    n_sub = q_ref.shape[0] // L
    hl = N_HEADS * L
    kk = lax.broadcasted_iota(jnp.int32, (1, nk), 1)
    n_mask = min(n_sub, WINDOW // L) if masked else 0

    @pl.when((pl.program_id(0) == 0) & (ci == 0))
    def _():
        bucket = bucket_ref[...]
        for h in range(N_HEADS):
            acc = jnp.zeros((L, nk), F32)
            for b in range(NUM_BUCKETS):
                acc = jnp.where(bucket == b, relb_ref[b, h], acc)
            bias_s[h * L:(h + 1) * L, :] = acc
            for u in range(n_mask):
                bias_s[(1 + u) * hl + h * L:(1 + u) * hl + (h + 1) * L, :] = jnp.where(kk >= WINDOW - u * L, acc, NEG_INF)

    kall = jnp.concatenate([kh_ref[...], ko_ref[...]], axis=0)
    vall = jnp.concatenate([vh_ref[...], vo_ref[...]], axis=0)
    gw = GROUP * HEAD_DIM
    slot = lax.broadcasted_iota(jnp.int32, (L, gw), 1) // HEAD_DIM
    gl = GROUP * L
    for u in range(n_sub):
        kcat = kall[u * L:u * L + nk, :]
        for g in range(N_KV):
            gsl = slice(g * gw, (g + 1) * gw)
            qg = q_ref[u * L:(u + 1) * L, gsl]
            zero = jnp.zeros_like(qg)
            qs = jnp.concatenate([jnp.where(slot == j, qg, zero) for j in range(GROUP)], axis=0)
            s_scr[u * hl + g * gl:u * hl + (g + 1) * gl, :] = _dot_nt(qs, kcat[:, gsl])
    for u in range(n_sub):
        for h in range(N_HEADS):
            rows = slice(u * hl + h * L, u * hl + (h + 1) * L)
            if u < n_mask:
                first = jnp.where(ci == 0, (1 + u) * hl, 0)
                bias = bias_s[pl.ds(pl.multiple_of(first + h * L, L), L), :]
            else:
                bias = bias_s[h * L:(h + 1) * L, :]
            s = s_scr[rows, :] + bias
            sink = sink_ref[0, h]
            mx = jnp.maximum(jnp.max(s, axis=1, keepdims=True), sink)
            e = jnp.exp(s - mx)
            den = jnp.sum(e, axis=1, keepdims=True) + jnp.exp(sink - mx)
            p_scr[rows, :] = (e / den).astype(BF16)
    for u in range(n_sub):
        vcat = vall[u * L:u * L + nk, :]
        for g in range(N_KV):
            gsl = slice(g * gw, (g + 1) * gw)
            ow = _dot(p_scr[u * hl + g * gl:u * hl + (g + 1) * gl, :], vcat[:, gsl])
            og = jnp.where(slot == 0, ow[0:L, :], 0.0)
            for j in range(1, GROUP):
                og = jnp.where(slot == j, ow[j * L:(j + 1) * L, :], og)
            o_ref[u * L:(u + 1) * L, gsl] = og.astype(BF16)


def _attention(q, k_own, v_own, k_hist, v_hist, hist_map, row0, seq, bsz, masked, bucket, rel_bias, sinks):
    L = min(seq, CHUNK)
    lb = min(seq, ATTN_BLOCK)
    nc, in_map, out_map = _seq_specs(bsz, seq, lb, row0)
    n_sub = lb // L
    nk = WINDOW + L
    smem = functools.partial(pl.BlockSpec, memory_space=pltpu.SMEM)
    return pl.pallas_call(
        functools.partial(_attn_kernel, masked, L),
        grid=(bsz, nc),
        in_specs=[pl.BlockSpec((lb, D), in_map),
                  pl.BlockSpec((WINDOW, D), hist_map), pl.BlockSpec((lb, D), in_map),
                  pl.BlockSpec((WINDOW, D), hist_map), pl.BlockSpec((lb, D), in_map),
                  _full(bucket.shape), smem(), smem()],
        out_specs=pl.BlockSpec((lb, D), out_map),
        out_shape=jax.ShapeDtypeStruct((bsz * seq, D), BF16),
        scratch_shapes=[pltpu.VMEM(((1 + (min(n_sub, WINDOW // L) if masked else 0)) * N_HEADS * L, nk), F32),
                        pltpu.VMEM((n_sub * N_HEADS * L, nk), F32),
                        pltpu.VMEM((n_sub * N_HEADS * L, nk), BF16)],
        compiler_params=_params(("arbitrary", "arbitrary")),
        name="swa",
    )(q, k_hist, k_own, v_hist, v_own, bucket, rel_bias, sinks)


def _router_kernel(x_ref, g_ref, wr_ref, info_ref, cnt_ref):
    tm = x_ref.shape[0] // SUB

    @pl.when(pl.program_id(0) == 0)
    def _():
        cnt_ref[...] = jnp.zeros_like(cnt_ref)

    xn = _rms(_tile_rows_load(x_ref), g_ref[...])
    hi = xn.astype(BF16)
    lo = (xn - hi.astype(F32)).astype(BF16)
    w = wr_ref[...]
    whi = w.astype(BF16)
    wlo = (w - whi.astype(F32)).astype(BF16)
    logits = _dot(hi, whi) + _dot(hi, wlo) + _dot(lo, whi)
    lane = lax.broadcasted_iota(jnp.int32, (tm, LANES), 1)
    logits = jnp.where(lane < N_EXPERTS, logits, NEG_INF)
    m1 = jnp.max(logits, axis=1, keepdims=True)
    i1 = jnp.min(jnp.where(logits == m1, lane, LANES), axis=1, keepdims=True)
    rest = jnp.where(lane == i1, NEG_INF, logits)
    m2 = jnp.max(rest, axis=1, keepdims=True)
    i2 = jnp.min(jnp.where(rest == m2, lane, LANES), axis=1, keepdims=True)
    e2 = jnp.exp(m2 - m1)
    g1 = 1.0 / (1.0 + e2)
    g2 = e2 / (1.0 + e2)
    sel = (lane == i1) | (lane == i2)
    tri = (lax.broadcasted_iota(jnp.int32, (tm, tm), 0) > lax.broadcasted_iota(jnp.int32, (tm, tm), 1)).astype(BF16)
    rank = cnt_ref[...] + _dot(tri, sel.astype(BF16))
    r1 = jnp.sum(jnp.where(lane == i1, rank, 0.0), axis=1, keepdims=True)
    r2 = jnp.sum(jnp.where(lane == i2, rank, 0.0), axis=1, keepdims=True)
    cnt_ref[...] = cnt_ref[...] + jnp.sum(sel.astype(F32), axis=0, keepdims=True)
    info = jnp.where(lane == 0, i1.astype(F32), 0.0)
    info = jnp.where(lane == 1, i2.astype(F32), info)
    info = jnp.where(lane == 2, g1, info)
    info = jnp.where(lane == 3, g2, info)
    info = jnp.where(lane == 4, r1, info)
    info = jnp.where(lane == 5, r2, info)
    info_ref[...] = info


def _router(x, g, wr, tm):
    n = x.shape[0] // SUB
    return pl.pallas_call(
        _router_kernel,
        grid=(n // tm,),
        in_specs=[_tile_spec(tm), _full((1, D)), _full(wr.shape)],
        out_specs=[_rows(tm, LANES), _full((1, LANES))],
        out_shape=[jax.ShapeDtypeStruct((n, LANES), F32), jax.ShapeDtypeStruct((1, LANES), F32)],
        compiler_params=_params(("arbitrary",)),
        name="moe_router",
    )(x, g, wr)


def _token_tile(ref, t):
    return ref.at[pl.ds(pl.multiple_of(t * SUB, SUB), SUB)]


def _moe_scatter_kernel(te, zpos_ref, dest_ref, x_ref, w1_ref, w3_ref, w2_ref, xs_hbm, w1b_ref, w3b_ref, w2b_ref,
                        zero_s, sem):
    tm = x_ref.shape[0] // SUB
    span = te * SUB

    def rows_of(tile_start):
        return xs_hbm.at[pl.ds(pl.multiple_of(tile_start * SUB, span), span)]

    @pl.when(pl.program_id(0) == 0)
    def _():
        zero_s[...] = jnp.zeros_like(zero_s)
        for e in range(N_EXPERTS):
            pltpu.make_async_copy(zero_s, rows_of(zpos_ref[e]), sem).start()
        for e in range(N_EXPERTS):
            pltpu.make_async_copy(zero_s, rows_of(zpos_ref[e]), sem).wait()

        def clear_tail(t, carry):
            tail = pltpu.make_async_copy(zero_s, rows_of(t * te), sem)
            tail.start()
            tail.wait()
            return carry

        lax.fori_loop(zpos_ref[N_EXPERTS] // te, xs_hbm.shape[0] // span, clear_tail, 0)

    def copy(r, s):
        return pltpu.make_async_copy(_token_tile(x_ref, r), _token_tile(xs_hbm, dest_ref[s * tm + r]), sem)

    def start(r, carry):
        for s in range(2):
            copy(r, s).start(priority=s)
        return carry

    def wait(r, carry):
        for s in range(2):
            copy(r, s).wait()
        return carry

    lax.fori_loop(0, tm, start, 0)
    for src, dst in ((w1_ref, w1b_ref), (w3_ref, w3b_ref), (w2_ref, w2b_ref)):
        dst[...] = src[...].astype(BF16)
    lax.fori_loop(0, tm, wait, 0)


def _moe_scatter(x, dest, zpos, n_rows, tm, te, weights):
    n = x.shape[0] // SUB
    n_steps = n // tm
    n_slabs = 1 << (n_steps.bit_length() - 1)
    flat = [w.reshape(-1, w.shape[-1]) for w in weights]
    slab = lambda w: pl.BlockSpec((w.shape[0] // n_slabs, w.shape[1]), lambda i, zp: (jnp.minimum(i, n_slabs - 1), 0))
    out = pl.pallas_call(
        functools.partial(_moe_scatter_kernel, te),
        grid_spec=pltpu.PrefetchScalarGridSpec(
            num_scalar_prefetch=1,
            grid=(n_steps,),
            in_specs=[pl.BlockSpec((2 * tm,), lambda i, zp: (i,), memory_space=pltpu.SMEM),
                      pl.BlockSpec((tm * SUB, LANES), lambda i, zp: (i, 0))] + [slab(w) for w in flat],
            out_specs=[pl.BlockSpec(memory_space=pl.ANY)] + [slab(w) for w in flat],
            scratch_shapes=[pltpu.VMEM((te * SUB, LANES), F32), pltpu.SemaphoreType.DMA(())],
        ),
        out_shape=[jax.ShapeDtypeStruct((n_rows * SUB, LANES), F32)]
                  + [jax.ShapeDtypeStruct(w.shape, BF16) for w in flat],
        compiler_params=_params(("arbitrary",)),
        name="moe_scatter",
    )(zpos, dest, x, *flat)
    return out[0], [wb.reshape(w.shape) for wb, w in zip(out[1:], weights)]


def _moe_ffn_kernel(n_ff, te, pe_ref, na_ref, xs_ref, g_ref, w1_ref, w3_ref, w2_ref, ys_ref, xn_s, acc_s):
    i, j = pl.program_id(0), pl.program_id(1)
    span = te * SUB
    for s in range(2):
        active = s < na_ref[i]
        rows = slice(s * te, (s + 1) * te)

        @pl.when(active & (j == 0))
        def _():
            xn_s[rows, :] = _rms(_tile_rows_load(xs_ref.at[pl.ds(s * span, span)]), g_ref[...]).astype(BF16)
            acc_s[rows, :] = jnp.zeros((te, D), F32)

        @pl.when(active)
        def _():
            xn = xn_s[rows, :]
            a = (_silu(_dot(xn, w1_ref[0])) * _dot(xn, w3_ref[0])).astype(BF16)
            acc_s[rows, :] += _dot(a, w2_ref[0])

        @pl.when(j == n_ff - 1)
        def _():
            _tile_rows_store(ys_ref.at[pl.ds(s * span, span)], jnp.where(active, acc_s[rows, :], 0.0))


def _moe_ffn(xs, g, w1, w3, w2, pair_expert, pair_active, te):
    n_pairs = pair_expert.shape[0]
    d_ff = w1.shape[2]
    n_ff = FF_SPLIT_MOE
    tf = d_ff // n_ff

    def ff(i, j, na_r):
        return jnp.where(na_r[i] > 0, j, n_ff - 1)

    rows = pl.BlockSpec((2 * te * SUB, LANES), lambda i, j, pe_r, na_r: (i, 0))
    return pl.pallas_call(
        functools.partial(_moe_ffn_kernel, n_ff, te),
        grid_spec=pltpu.PrefetchScalarGridSpec(
            num_scalar_prefetch=2,
            grid=(n_pairs, n_ff),
            in_specs=[rows,
                      pl.BlockSpec((1, D), lambda i, j, pe_r, na_r: (0, 0)),
                      pl.BlockSpec((1, D, tf), lambda i, j, pe_r, na_r: (pe_r[i], 0, ff(i, j, na_r))),
                      pl.BlockSpec((1, D, tf), lambda i, j, pe_r, na_r: (pe_r[i], 0, ff(i, j, na_r))),
                      pl.BlockSpec((1, tf, D), lambda i, j, pe_r, na_r: (pe_r[i], ff(i, j, na_r), 0))],
            out_specs=rows,
            scratch_shapes=[pltpu.VMEM((2 * te, D), BF16), pltpu.VMEM((2 * te, D), F32)],
        ),
        out_shape=jax.ShapeDtypeStruct((n_pairs * 2 * te * SUB, LANES), F32),
        compiler_params=_params(("arbitrary", "arbitrary")),
        name="moe_ffn",
    )(pair_expert, pair_active, xs, g, w1, w3, w2)


def _moe_combine_kernel(n_p, dest_ref, info_ref, x_ref, ys_hbm, op_ref, os_ref, buf, sem):
    tm = x_ref.shape[0] // SUB
    i = pl.program_id(0)

    def copy(r, s):
        return pltpu.make_async_copy(_token_tile(ys_hbm, dest_ref[s * tm + r]), _token_tile(buf.at[s], r), sem)

    def start(r, carry):
        for s in range(2):
            copy(r, s).start(priority=s)
        return carry

    def wait(r, carry):
        for s in range(2):
            copy(r, s).wait()
        return carry

    lax.fori_loop(0, tm, start, 0)
    lax.fori_loop(0, tm, wait, 0)
    out = (_tile_rows_load(x_ref) + info_ref[:, 2:3] * _tile_rows_load(buf.at[0])
           + info_ref[:, 3:4] * _tile_rows_load(buf.at[1]))

    @pl.when(i < n_p)
    def _():
        op_ref[...] = out

    @pl.when(i >= n_p)
    def _():
        os_ref[...] = out


def _moe_combine(x, info, dest, ys, tm, n_p, n_s):
    out_p, out_s = _pair(tm, D, n_p)
    return pl.pallas_call(
        functools.partial(_moe_combine_kernel, n_p),
        grid=(n_p + n_s,),
        in_specs=[pl.BlockSpec((2 * tm,), lambda i: (i,), memory_space=pltpu.SMEM),
                  _rows(tm, LANES), _tile_spec(tm), pl.BlockSpec(memory_space=pl.ANY)],
        out_specs=[out_p, out_s],
        out_shape=[jax.ShapeDtypeStruct((n_p * tm, D), F32), jax.ShapeDtypeStruct((n_s * tm, D), F32)],
        scratch_shapes=[pltpu.VMEM((2, tm * SUB, LANES), F32), pltpu.SemaphoreType.DMA(())],
        compiler_params=_params(("arbitrary",)),
        name="moe_combine",
    )(dest, info, x, ys)


def _moe(x, g, w_router, w1, w3, w2, tm, n_p, n_s):
    n = x.shape[0] // SUB
    te = min(EXPERT_TILE, tm)
    wr = jnp.zeros((D, LANES), F32).at[:, :N_EXPERTS].set(w_router)
    info, cnt = _router(x, g, wr, tm)
    e1, e2 = info[:, 0].astype(jnp.int32), info[:, 1].astype(jnp.int32)
    r1, r2 = info[:, 4].astype(jnp.int32), info[:, 5].astype(jnp.int32)
    counts = cnt[0, :N_EXPERTS].astype(jnp.int32)
    region = 2 * te
    padded = (counts + region - 1) // region * region
    ends = jnp.cumsum(padded)
    starts = ends - padded
    dest = jnp.stack([(starts[e1] + r1).reshape(n // tm, tm), (starts[e2] + r2).reshape(n // tm, tm)], axis=1)
    dest = dest.reshape(-1)
    n_pairs = (2 * n + N_EXPERTS * (region - 1)) // region
    pair_start = jnp.arange(n_pairs, dtype=jnp.int32) * region
    pair_expert = jnp.minimum(jnp.sum(pair_start[:, None] >= ends[None, :], axis=1), N_EXPERTS - 1).astype(jnp.int32)
    used_end = starts + (counts + te - 1) // te * te
    pair_active = jnp.clip((used_end[pair_expert] - pair_start) // te, 0, 2).astype(jnp.int32)
    last_region = jnp.maximum(ends - region, 0).astype(jnp.int32)
    xs, (w1b, w3b, w2b) = _moe_scatter(x, dest, jnp.concatenate([last_region, ends[-1:].astype(jnp.int32)]),
                                       n_pairs * region, tm, region, (w1, w3, w2))
    ys = _moe_ffn(xs, g, w1b, w3b, w2b, pair_expert, pair_active, te)
    return _moe_combine(x, info, dest, ys, tm, n_p, n_s)


def _t5_bucket(rel):
    n = -rel
    half = NUM_BUCKETS // 2
    ret = jnp.where(n < 0, half, 0)
    n = jnp.abs(n)
    max_exact = half // 2
    nf = jnp.maximum(n, 1).astype(F32)
    large = max_exact + (jnp.log(nf / max_exact) / math.log(MAX_DISTANCE / max_exact)
                         * (half - max_exact)).astype(jnp.int32)
    large = jnp.minimum(large, half - 1)
    return ret + jnp.where(n < max_exact, n, large)


def _bucket_table(length):
    kpos = jnp.arange(WINDOW + length) - WINDOW
    return _t5_bucket(kpos[None, :] - jnp.arange(length)[:, None]).astype(jnp.int32)


def kernel(x_prompt, x_sample, state_conv_a, state_rglru_h, state_conv_b, state_mlstm_c, state_mlstm_n, state_mlstm_m, cache_swa_k, cache_swa_v, norm_mix, norm_ffn, w_in_ab, w_conv_a, b_conv_a, w_rg_a, b_rg_a, w_rg_x, b_rg_x, rg_lambda, w_conv_b, b_conv_b, w_q_b, w_k_b, w_v_b, w_if_b, b_if_b, g_hnorm_b, skip_b, w_out_ab, w1_dense, w3_dense, w2_dense, w_in_att, g_qnorm, g_knorm, sinks, w_out_att, rel_bias, w_router, w1_moe, w3_moe, w2_moe):
    bp, tp, _ = x_prompt.shape
    bs, ts, _ = x_sample.shape
    assert norm_mix.shape[0] == 2 and w_in_ab.shape[0] == 1 and w_in_att.shape[0] == 1
    assert tp % CHUNK == 0 and ts <= CHUNK and cache_swa_k.shape[2] == WINDOW
    rows_p, rows_s = bp * tp, bs * ts
    tm = math.gcd(math.gcd(rows_p, rows_s), TOKEN_TILE)
    n_p, n_s = rows_p // tm, rows_s // tm
    xp = x_prompt.reshape(rows_p, D)
    xs = x_sample.reshape(rows_s, D)
    bf = lambda w: w.astype(BF16)
    vec = lambda v: v.reshape(1, -1)

    later = (w_out_ab[0], w1_dense[0], w3_dense[0], w2_dense[0], w_in_att[0], w_out_att[0])
    xa, ga, xm, z, w_out, w1_d, w3_d, w2_d, w_att_in, w_att_out = _in_proj_ab(
        xp, xs, vec(norm_mix[0]), bf(w_in_ab[0]), tm, later)
    rg_w = (w_conv_a[0], vec(b_conv_a[0]), bf(w_rg_a[0]), vec(b_rg_a[0]), bf(w_rg_x[0]), vec(b_rg_x[0]),
            vec(rg_lambda[0]))
    ya_p, hl_p = _rglru(xa, ga, 0, tp, jnp.zeros((bp, CONV_W - 1, D), F32), jnp.zeros((bp, 1, D), F32), *rg_w)
    ya_s, hl_s = _rglru(xa, ga, rows_p, ts, state_conv_a[0], state_rglru_h[0].reshape(bs, 1, D), *rg_w)

    def gate_cols(w):
        pad = ((0, 0), (0, LANES - NH_B))
        return jnp.concatenate([jnp.pad(w[:, :NH_B], pad), jnp.pad(w[:, NH_B:], pad)], axis=1)

    pre_w = (w_conv_b[0], vec(b_conv_b[0]), bf(w_q_b[0]), bf(w_k_b[0]), bf(w_v_b[0]), bf(w_if_b[0].T),
             b_if_b[0].reshape(2 * NH_B, 1), bf(gate_cols(w_if_b[0])), gate_cols(b_if_b[0][None]))
    rec_w = (g_hnorm_b[0].reshape(1, D), vec(skip_b[0]))
    lanes = lambda m: jnp.pad(m, ((0, 0), (0, LANES - NH_B)))[:, None, :]
    pre_p = _mlstm_pre(xm, 0, tp, jnp.zeros((bp, CONV_W - 1, D), F32), jnp.zeros((bp, 1, LANES), F32), *pre_w)
    pre_s = _mlstm_pre(xm, rows_p, ts, state_conv_b[0], lanes(state_mlstm_m[0]), *pre_w)
    m_p, m_s = pre_p[4], pre_s[4]
    yb_p, c_p, nn_p = _mlstm(*pre_p[1:4], pre_p[0], z, 0, tp, jnp.zeros((bp, NH_B, DH_B, DH_B), F32),
                             jnp.zeros((bp, NH_B, DH_B), F32), *rec_w)
    yb_s, c_s, nn_s = _mlstm(*pre_s[1:4], pre_s[0], z, rows_p, ts, state_mlstm_c[0], state_mlstm_n[0], *rec_w)

    y = _out_proj((xp, xs), [(ya_p, ya_s), (yb_p, yb_s)], [w_out[:D], w_out[D:]], tm, n_p)
    y = _ffn_dense(y, vec(norm_ffn[0]), w1_d, w3_d, w2_d, tm)

    tile2 = lambda gain: jnp.tile(gain, D_KV // HEAD_DIM).reshape(1, D_KV)
    q, kw, vw, k, v = _in_proj_att(y, vec(norm_mix[1]), w_att_in, tile2(g_qnorm[0]), tile2(g_knorm[0]), tm)
    sinks2 = sinks[0].reshape(1, N_HEADS)
    wpb = min(tp, ATTN_BLOCK) // WINDOW
    assert wpb >= 1 and tp % (wpb * WINDOW) == 0
    hist_p = lambda bi, ci: (bi * (tp // WINDOW) + jnp.maximum(ci * wpb - 1, 0), 0)
    o_p = _attention(q, kw, vw, kw, vw, hist_p, 0, tp, bp, True, _bucket_table(CHUNK), rel_bias, sinks2)

    def widen(cache):
        wide = jnp.broadcast_to(cache[:, :, :, None, :], (bs, WINDOW, N_KV, GROUP, HEAD_DIM))
        return wide.reshape(bs * WINDOW, D).astype(BF16)

    ck, cv = widen(cache_swa_k[0]), widen(cache_swa_v[0])
    hist_s = lambda bi, ci: (bi, 0)
    o_s = _attention(q, kw, vw, ck, cv, hist_s, rows_p, ts, bs, False, _bucket_table(ts), rel_bias, sinks2)
    y = _out_proj(y, [(o_p, o_s)], [w_att_out], tm, n_p, tiled_out=True)
    yp, ys = _moe(y, vec(norm_ffn[1]), w_router[0], w1_moe[0], w3_moe[0], w2_moe[0], tm, n_p, n_s)

    def tail(a, rows, b, t, keep):
        if rows == 0:
            return jnp.stack([a[(i + 1) * t - keep:(i + 1) * t] for i in range(b)])
        return a[rows:rows + b * t].reshape(b, t, -1)[:, t - keep:]

    kv4 = lambda a, b: a.reshape(b, -1, N_KV, HEAD_DIM)
    one = lambda a: a[None]
    k_s = jnp.concatenate([cache_swa_k[0][:, ts:], kv4(k[rows_p:], bs)], axis=1)
    v_s = jnp.concatenate([cache_swa_v[0][:, ts:], kv4(v[rows_p:], bs)], axis=1)
    return (yp.reshape(bp, tp, D), ys.reshape(bs, ts, D),
            one(tail(xa, 0, bp, tp, CONV_W - 1)), one(hl_p.reshape(bp, D)), one(tail(xm, 0, bp, tp, CONV_W - 1)),
            one(c_p), one(nn_p), one(m_p[:, 0, :NH_B]),
            one(kv4(tail(k, 0, bp, tp, WINDOW), bp)), one(kv4(tail(v, 0, bp, tp, WINDOW), bp)),
            one(tail(xa, rows_p, bs, ts, CONV_W - 1)), one(hl_s.reshape(bs, D)),
            one(tail(xm, rows_p, bs, ts, CONV_W - 1)), one(c_s), one(nn_s), one(m_s[:, 0, :NH_B]),
            one(k_s), one(v_s))
```
